```python
import jax, jax.numpy as jnp
from jax import lax
import numpy as np

D_MODEL = 1024
BATCH = 8
SEQ = 2048
DEPTH = 1
DEC_BATCH = 128
DEC_SEQ = 4
PAST_LEN = 16384
PAGE_SIZE = 128

D_MIX = D_MODEL
N_HEADS = 4
D_MLSTM = D_MIX // 2
DK = D_MLSTM // N_HEADS
DV = DK
D_CONV = D_MIX - D_MLSTM
CONV_WIDTH = 31
CONV_BUF = CONV_WIDTH - 1
D_FF = 4 * D_MODEL
D_PLE = 256
CHUNK = 64
EPS = 1e-6
IN_WIDTHS = (D_MLSTM, D_MLSTM, D_MLSTM, D_MLSTM, N_HEADS, N_HEADS, D_CONV, D_CONV)
D_IN = sum(IN_WIDTHS)
IN_SPLITS = tuple(int(s) for s in np.cumsum(IN_WIDTHS)[:-1])
F_OFF = 4 * D_MLSTM + N_HEADS

kernel_name = 'hymba_mlstm_conformer_step'


def rms_norm(x, g):
    xf = x.astype(jnp.float32)
    y = xf * lax.rsqrt(jnp.mean(xf * xf, axis=-1, keepdims=True) + EPS)
    return (y * g.astype(jnp.float32)).astype(x.dtype)


def layer_norm(xf):
    mu = jnp.mean(xf, axis=-1, keepdims=True)
    xc = xf - mu
    return xc * lax.rsqrt(jnp.mean(xc * xc, axis=-1, keepdims=True) + EPS)


def mlstm_chunk(carry, xs):
    C, n, m = carry
    q, k, v, ig, lf = xs
    L = q.shape[1]
    b = jnp.cumsum(lf, axis=1).transpose(0, 2, 1)
    i_h = ig.transpose(0, 2, 1)
    causal = jnp.arange(L)[:, None] >= jnp.arange(L)[None, :]
    logw = jnp.where(causal, b[..., :, None] - b[..., None, :] + i_h[..., None, :], -jnp.inf)
    log_inter = b + m[..., None]
    m_t = jnp.maximum(log_inter, jnp.max(logw, axis=-1))
    w_intra = jnp.exp(logw - m_t[..., None])
    w_inter = jnp.exp(log_inter - m_t)
    a = w_intra * jnp.einsum('blhd,bshd->bhls', q, k)
    num = (jnp.einsum('bhls,bshe->bhle', a, v)
           + w_inter[..., None] * jnp.einsum('bhed,blhd->bhle', C, q))
    den = jnp.sum(a, axis=-1) + w_inter * jnp.einsum('bhd,blhd->bhl', n, q)
    h = num / jnp.maximum(jnp.abs(den), jnp.exp(-m_t))[..., None]
    m_new = m_t[..., -1]
    g = jnp.exp(b[..., -1:] - b + i_h - m_new[..., None])
    decay = jnp.exp(b[..., -1] + m - m_new)
    C_new = decay[..., None, None] * C + jnp.einsum('bhs,bshe,bshd->bhed', g, v, k)
    n_new = decay[..., None] * n + jnp.einsum('bhs,bshd->bhd', g, k)
    return (C_new, n_new, m_new), h.transpose(0, 2, 1, 3)


def mlstm_run(q, k, v, ig, lf, C0, n0, m0):
    B, T = q.shape[0], q.shape[1]
    chunk = CHUNK if T % CHUNK == 0 else T
    nc = T // chunk

    def to_chunks(a):
        return a.reshape((B, nc, chunk) + a.shape[2:]).swapaxes(0, 1)

    xs = tuple(to_chunks(a) for a in (q, k, v, ig, lf))
    (C, n, m), h = lax.scan(mlstm_chunk, (C0, n0, m0), xs)
    h = h.swapaxes(0, 1).reshape(B, T, N_HEADS, DV)
    return h, C, n, m


def decoder_layer(x, p, C0, n0, m0, conv_buf, g_mix, w_in, b_in, g_head, w_dw, b_dw,
                  g_cn, b_cn, w_out, g_ffn, w_ff1, w_ff2, g_ple, w_ple_gate, w_ple_proj):
    B, T, _ = x.shape
    f32 = jnp.float32
    z = rms_norm(x, g_mix) @ w_in + b_in
    q, k, v, o, ig, fg, ga, gg = jnp.split(z, IN_SPLITS, axis=-1)
    q = q.reshape(B, T, N_HEADS, DK).astype(f32)
    k = k.reshape(B, T, N_HEADS, DK).astype(f32) * (DK ** -0.5)
    v = v.reshape(B, T, N_HEADS, DV).astype(f32)
    lf = jax.nn.log_sigmoid(fg.astype(f32))
    h, C, n, m = mlstm_run(q, k, v, ig.astype(f32), lf,
                           C0.astype(f32), n0.astype(f32), m0.astype(f32))
    h = layer_norm(h).reshape(B, T, D_MLSTM) * g_head.astype(f32)
    h = h * jax.nn.sigmoid(o.astype(f32))
    u = ga * jax.nn.sigmoid(gg)
    ext = jnp.concatenate([conv_buf.astype(u.dtype), u], axis=1)
    c = lax.conv_general_dilated(ext, w_dw[:, None, :].astype(ext.dtype), (1,), 'VALID',
                                 dimension_numbers=('NWC', 'WIO', 'NWC'),
                                 feature_group_count=D_CONV) + b_dw
    c = layer_norm(c.astype(f32)) * g_cn.astype(f32) + b_cn.astype(f32)
    c = jax.nn.silu(c)
    new_buf = ext[:, T:]
    mix = jnp.concatenate([h.astype(x.dtype), c.astype(x.dtype)], axis=-1)
    x = x + mix @ w_out
    f = jax.nn.relu(rms_norm(x, g_ffn) @ w_ff1)
    x = x + (f * f) @ w_ff2
    gate = jax.nn.sigmoid(rms_norm(x, g_ple) @ w_ple_gate)
    x = x + gate * (p @ w_ple_proj)
    return x, C, n, m, new_buf


def setup_inputs(seed: int = 0) -> dict:
    key = jax.random.key(seed)
    ks = jax.random.split(key, 24)
    nrm = jax.random.normal
    f32 = jnp.float32
    b_in = 0.02 * nrm(ks[9], (DEPTH, D_IN), f32)
    b_in = b_in.at[:, F_OFF:F_OFF + N_HEADS].add(jnp.linspace(3.0, 6.0, N_HEADS, dtype=f32))
    return {
        'x_prompt': nrm(ks[0], (BATCH, SEQ, D_MODEL), f32),
        'x_sample': nrm(ks[1], (DEC_BATCH, DEC_SEQ, D_MODEL), f32),
        'state_mlstm_C': 0.05 * nrm(ks[2], (DEPTH, DEC_BATCH, N_HEADS, DV, DK), f32),
        'state_mlstm_n': 0.1 * nrm(ks[3], (DEPTH, DEC_BATCH, N_HEADS, DK), f32),
        'state_mlstm_m': nrm(ks[4], (DEPTH, DEC_BATCH, N_HEADS), f32),
        'cache_conv': 0.5 * nrm(ks[5], (DEPTH, DEC_BATCH, CONV_BUF, D_CONV), f32),
        'p_prompt': nrm(ks[6], (DEPTH, BATCH, SEQ, D_PLE), f32),
        'p_sample': nrm(ks[7], (DEPTH, DEC_BATCH, DEC_SEQ, D_PLE), f32),
        'g_mix': 1.0 + 0.01 * nrm(ks[8], (DEPTH, D_MODEL), f32),
        'w_in': nrm(ks[10], (DEPTH, D_MODEL, D_IN), f32) * D_MODEL ** -0.5,
        'b_in': b_in,
        'g_head': 1.0 + 0.01 * nrm(ks[11], (DEPTH, D_MLSTM), f32),
        'w_dw': nrm(ks[12], (DEPTH, CONV_WIDTH, D_CONV), f32) * CONV_WIDTH ** -0.5,
        'b_dw': 0.02 * nrm(ks[13], (DEPTH, D_CONV), f32),
        'g_cn': 1.0 + 0.01 * nrm(ks[14], (DEPTH, D_CONV), f32),
        'b_cn': 0.02 * nrm(ks[15], (DEPTH, D_CONV), f32),
        'w_out': nrm(ks[16], (DEPTH, D_MIX, D_MODEL), f32) * D_MIX ** -0.5,
        'g_ffn': 1.0 + 0.01 * nrm(ks[17], (DEPTH, D_MODEL), f32),
        'w_ff1': nrm(ks[18], (DEPTH, D_MODEL, D_FF), f32) * D_MODEL ** -0.5,
        'w_ff2': nrm(ks[19], (DEPTH, D_FF, D_MODEL), f32) * D_FF ** -0.5,
        'g_ple': 1.0 + 0.01 * nrm(ks[20], (DEPTH, D_MODEL), f32),
        'w_ple_gate': nrm(ks[21], (DEPTH, D_MODEL, D_MODEL), f32) * D_MODEL ** -0.5,
        'w_ple_proj': nrm(ks[22], (DEPTH, D_PLE, D_MODEL), f32) * D_PLE ** -0.5,
        'g_final': 1.0 + 0.01 * nrm(ks[23], (D_MODEL,), f32),
    }


def reference(x_prompt, x_sample, state_mlstm_C, state_mlstm_n, state_mlstm_m, cache_conv,
              p_prompt, p_sample, g_mix, w_in, b_in, g_head, w_dw, b_dw, g_cn, b_cn, w_out,
              g_ffn, w_ff1, w_ff2, g_ple, w_ple_gate, w_ple_proj, g_final):
    Bp = x_prompt.shape[0]
    f32 = jnp.float32
    hp, hs = x_prompt, x_sample
    Cp_l, np_l, mp_l, bp_l = [], [], [], []
    Cs_l, ns_l, ms_l, bs_l = [], [], [], []
    for i in range(DEPTH):
        lw = (g_mix[i], w_in[i], b_in[i], g_head[i], w_dw[i], b_dw[i], g_cn[i], b_cn[i],
              w_out[i], g_ffn[i], w_ff1[i], w_ff2[i], g_ple[i], w_ple_gate[i], w_ple_proj[i])
        hp, Cp, np_, mp, bp = decoder_layer(
            hp, p_prompt[i],
            jnp.zeros((Bp, N_HEADS, DV, DK), f32), jnp.zeros((Bp, N_HEADS, DK), f32),
            jnp.zeros((Bp, N_HEADS), f32), jnp.zeros((Bp, CONV_BUF, D_CONV), x_prompt.dtype),
            *lw)
        hs, Cs, ns, ms, bs = decoder_layer(
            hs, p_sample[i], state_mlstm_C[i], state_mlstm_n[i], state_mlstm_m[i],
            cache_conv[i], *lw)
        Cp_l.append(Cp); np_l.append(np_); mp_l.append(mp); bp_l.append(bp)
        Cs_l.append(Cs); ns_l.append(ns); ms_l.append(ms); bs_l.append(bs)
    y_prompt = rms_norm(hp, g_final)
    y_sample = rms_norm(hs, g_final)
    return (y_prompt, y_sample,
            jnp.stack(Cp_l), jnp.stack(np_l), jnp.stack(mp_l), jnp.stack(bp_l),
            jnp.stack(Cs_l), jnp.stack(ns_l), jnp.stack(ms_l), jnp.stack(bs_l))
```

```python
import functools

import jax
import jax.numpy as jnp
from jax import lax
from jax.experimental import pallas as pl
from jax.experimental.pallas import tpu as pltpu

f32 = jnp.float32
bf16 = jnp.bfloat16

N_HEADS = 4
DK = 128
D_MLSTM = N_HEADS * DK
CONV_WIDTH = 31
CONV_BUF = CONV_WIDTH - 1
EPS = 1e-6
GATE_LANES = 128
F_LANE = N_HEADS
CHUNK = 128
TILE_ROWS = 256
HIST_ROWS = 32
SAMPLE_GROUP = 32
VMEM_LIMIT_BYTES = 56 * 1024 * 1024


def _bdot(a, b):
    return jnp.dot(a, b, preferred_element_type=f32)


def _dot_nt(a, b):
    return lax.dot_general(a, b, (((1,), (1,)), ((), ())), preferred_element_type=f32)


def _dot_tn(a, b):
    return lax.dot_general(a, b, (((0,), (0,)), ((), ())), preferred_element_type=f32)


def _rms(x, g):
    y = x * lax.rsqrt(jnp.mean(x * x, axis=-1, keepdims=True) + EPS)
    return y * g


def _ln(x):
    mu = jnp.mean(x, axis=-1, keepdims=True)
    xc = x - mu
    return xc * lax.rsqrt(jnp.mean(xc * xc, axis=-1, keepdims=True) + EPS)


def _exact_left(sel16, x):
    hi = x.astype(bf16)
    r = x - hi.astype(f32)
    mid = r.astype(bf16)
    lo = (r - mid.astype(f32)).astype(bf16)
    return _bdot(sel16, hi) + _bdot(sel16, mid) + _bdot(sel16, lo)


def _lane_pick(rows, lane_ids):
    out = jnp.zeros((rows[0].shape[0], GATE_LANES), f32)
    for h, r in enumerate(rows):
        out = jnp.where(lane_ids == F_LANE + h, r, out)
    return out


def _in_proj(x, g_mix_ref, w_in_ref, b_in_ref):
    xn = _rms(x, g_mix_ref[...]).astype(bf16)

    def proj(a, b):
        return _bdot(xn, w_in_ref[:, a:b]) + b_in_ref[:, a:b]

    d = D_MLSTM
    q = proj(0, d)
    k = proj(d, 2 * d) * (DK ** -0.5)
    v = proj(2 * d, 3 * d)
    o = proj(3 * d, 4 * d)
    gates = proj(4 * d, 4 * d + GATE_LANES)
    c0 = 4 * d + GATE_LANES
    ga = proj(c0, c0 + d)
    gg = proj(c0 + d, c0 + 2 * d)
    u = ga * jax.nn.sigmoid(gg)
    return q, k, v, o, gates, u


def _gate_cumsum(gates, lcum16):
    lf = jax.nn.log_sigmoid(gates)
    bc = _exact_left(lcum16, lf)
    return lf, bc, gates.T, bc.T


def _chunk_weights(gates, bc, gates_t, bc_t, h, mask, m_prev):
    icol = gates[:, h:h + 1]
    bcol = bc[:, F_LANE + h:F_LANE + h + 1]
    irow = gates_t[h:h + 1, :]
    brow = bc_t[F_LANE + h:F_LANE + h + 1, :]
    logw = jnp.where(mask, (bcol - brow) + irow, -jnp.inf)
    m_intra = jnp.max(logw, axis=-1, keepdims=True)
    log_inter = bcol + m_prev
    m_t = jnp.maximum(log_inter, m_intra)
    w_intra = jnp.exp(logw - m_t)
    w_inter = jnp.exp(log_inter - m_t)
    return icol, bcol, m_t, w_intra, w_inter


def _v_aug(v):
    lane = lax.broadcasted_iota(jnp.int32, v.shape, 1)
    return jnp.concatenate([v, jnp.where(lane == 0, 1.0, 0.0)], axis=1)


def _intra(q16, k16, vaug16, w_intra):
    s = _dot_nt(q16, k16)
    a = (w_intra * s).astype(bf16)
    return _bdot(a, vaug16)


def _finish(nd, w_inter, num_inter, den_inter, m_t):
    num = nd[:, :DK] + w_inter * num_inter
    den = nd[:, DK:DK + 1] + w_inter * den_inter
    return num / jnp.maximum(jnp.abs(den), jnp.exp(-m_t))


def _head_out(h, o, g_head):
    return (_ln(h) * g_head) * jax.nn.sigmoid(o)


def _conv_post(c, g_cn_ref, b_cn_ref):
    c = _ln(c) * g_cn_ref[...] + b_cn_ref[...]
    return c * jax.nn.sigmoid(c)


def _tail(x, hm, c, p, w_out_ref, g_ffn_ref, w_ff1_ref, w_ff2_ref, g_ple_ref,
          w_gate_ref, w_proj_ref, g_final_ref):
    d = D_MLSTM
    x = x + (_bdot(hm.astype(bf16), w_out_ref[0:d, :]) + _bdot(c.astype(bf16), w_out_ref[d:, :]))
    xn = _rms(x, g_ffn_ref[...]).astype(bf16)
    d_ff = w_ff1_ref.shape[1]
    step = 1024
    acc = None
    for j in range(0, d_ff, step):
        f = jnp.maximum(_bdot(xn, w_ff1_ref[:, j:j + step]), 0.0)
        part = _bdot((f * f).astype(bf16), w_ff2_ref[j:j + step, :])
        acc = part if acc is None else acc + part
    x = x + acc
    gate = jax.nn.sigmoid(_bdot(_rms(x, g_ple_ref[...]).astype(bf16), w_gate_ref[...]))
    x = x + gate * _bdot(p.astype(bf16), w_proj_ref[...])
    return _rms(x, g_final_ref[...])


def _prompt_kernel(x_ref, p_ref, g_mix_ref, w_in_ref, b_in_ref, g_head_ref, w_dw_ref, b_dw_ref,
                   g_cn_ref, b_cn_ref, w_out_ref, g_ffn_ref, w_ff1_ref, w_ff2_ref, g_ple_ref,
                   w_gate_ref, w_proj_ref, g_final_ref,
                   y_ref, c_out_ref, n_out_ref, m_out_ref, conv_out_ref,
                   ct_s, m_s, ext_s):
    t = pl.program_id(1)
    nt = pl.num_programs(1)
    tm = x_ref.shape[0]

    @pl.when(t == 0)
    def _():
        ct_s[...] = jnp.zeros_like(ct_s)
        m_s[...] = jnp.zeros_like(m_s)
        ext_s[0:HIST_ROWS, :] = jnp.zeros((HIST_ROWS, ext_s.shape[1]), f32)

    x = x_ref[...]
    q, k, v, o, gates, u = _in_proj(x, g_mix_ref, w_in_ref, b_in_ref)

    rid = lax.broadcasted_iota(jnp.int32, (CHUNK, CHUNK), 0)
    cid = lax.broadcasted_iota(jnp.int32, (CHUNK, CHUNK), 1)
    causal = cid <= rid
    lcum16 = jnp.where(causal, 1.0, 0.0).astype(bf16)
    lane_row = lax.broadcasted_iota(jnp.int32, (1, GATE_LANES), 1)
    q16 = q.astype(bf16)
    k16 = k.astype(bf16)
    h_chunks = []
    for c0 in range(0, tm, CHUNK):
        rows = slice(c0, c0 + CHUNK)
        _, bc, gates_t, bc_t = _gate_cumsum(gates[rows], lcum16)
        m_row = m_s[...]
        m_new_cols = []
        h_heads = []
        for h in range(N_HEADS):
            cols = slice(h * DK, (h + 1) * DK)
            m_prev = m_row[:, F_LANE + h:F_LANE + h + 1]
            icol, bcol, m_t, w_intra, w_inter = _chunk_weights(
                gates[rows], bc, gates_t, bc_t, h, causal, m_prev)
            vaug = _v_aug(v[rows, cols])
            nd = _intra(q16[rows, cols], k16[rows, cols], vaug.astype(bf16), w_intra)
            ct = ct_s[h]
            qc = _bdot(q16[rows, cols], ct.astype(bf16))
            h_heads.append(_finish(nd, w_inter, qc[:, :DK], qc[:, DK:DK + 1], m_t))
            m_new = m_t[CHUNK - 1:CHUNK]
            b_last = bcol[CHUNK - 1:CHUNK]
            gcol = jnp.exp(((b_last - bcol) + icol) - m_new)
            decay = jnp.exp(b_last + m_prev - m_new)
            ct_s[h] = decay * ct + _dot_tn(k16[rows, cols], (gcol * vaug).astype(bf16))
            m_new_cols.append(m_new)
        m_s[...] = _lane_pick(m_new_cols, lane_row)
        h_chunks.append(h_heads)
    hm = jnp.concatenate(
        [_head_out(jnp.concatenate([hc[h] for hc in h_chunks], axis=0),
                   o[:, h * DK:(h + 1) * DK], g_head_ref[:, h * DK:(h + 1) * DK])
         for h in range(N_HEADS)], axis=1)

    ext_s[HIST_ROWS:HIST_ROWS + tm, :] = u
    off = HIST_ROWS - CONV_BUF
    rb = 32
    c_blocks = []
    for r0 in range(0, tm, rb):
        acc = w_dw_ref[0:1, :] * ext_s[pl.ds(off + r0, rb), :]
        for kk in range(1, CONV_WIDTH):
            acc = acc + w_dw_ref[kk:kk + 1, :] * ext_s[pl.ds(off + r0 + kk, rb), :]
        c_blocks.append(acc + b_dw_ref[...])
    c = _conv_post(jnp.concatenate(c_blocks, axis=0), g_cn_ref, b_cn_ref)
    ext_s[0:HIST_ROWS, :] = ext_s[tm:tm + HIST_ROWS, :]

    y_ref[...] = _tail(x, hm, c, p_ref[...], w_out_ref, g_ffn_ref, w_ff1_ref, w_ff2_ref,
                       g_ple_ref, w_gate_ref, w_proj_ref, g_final_ref)

    @pl.when(t == nt - 1)
    def _():
        for h in range(N_HEADS):
            ctt = ct_s[h].T
            c_out_ref[h] = ctt[:DK]
            n_out_ref[h:h + 1, :] = ctt[DK:DK + 1]
        m_out_ref[...] = jnp.broadcast_to(m_s[...], m_out_ref.shape)
        conv_out_ref[...] = ext_s[off:HIST_ROWS, :]


def _const_spec(shape):
    return pl.BlockSpec(shape, lambda *_: (0,) * len(shape), pipeline_mode=pl.Buffered(1))


def _weight_specs(ws):
    return [_const_spec(w.shape) for w in ws]


def _prompt_call(x, p, ws):
    B, T, D = x.shape
    tm = TILE_ROWS
    d_conv = ws[4].shape[1]
    row = lambda b, t: (b, t, 0)
    sd = jax.ShapeDtypeStruct
    return pl.pallas_call(
        _prompt_kernel,
        grid=(B, T // tm),
        in_specs=[pl.BlockSpec((None, tm, D), row), pl.BlockSpec((None, tm, p.shape[2]), row)]
        + _weight_specs(ws),
        out_specs=(
            pl.BlockSpec((None, tm, D), row),
            pl.BlockSpec((None, N_HEADS, DK, DK), lambda b, t: (b, 0, 0, 0)),
            pl.BlockSpec((None, N_HEADS, DK), lambda b, t: (b, 0, 0)),
            pl.BlockSpec((None, 8, GATE_LANES), lambda b, t: (b, 0, 0)),
            pl.BlockSpec((None, CONV_BUF, d_conv), lambda b, t: (b, 0, 0)),
        ),
        out_shape=(
            sd((B, T, D), f32),
            sd((B, N_HEADS, DK, DK), f32),
            sd((B, N_HEADS, DK), f32),
            sd((B, 8, GATE_LANES), f32),
            sd((B, CONV_BUF, d_conv), f32),
        ),
        scratch_shapes=[
            pltpu.VMEM((N_HEADS, DK, 2 * DK), f32),
            pltpu.VMEM((1, GATE_LANES), f32),
            pltpu.VMEM((HIST_ROWS + tm, d_conv), f32),
        ],
        compiler_params=pltpu.CompilerParams(
            dimension_semantics=("arbitrary", "arbitrary"),
            vmem_limit_bytes=VMEM_LIMIT_BYTES),
        name="prompt_layer",
    )(x, p, *ws)


def _sample_in_kernel(x_ref, g_mix_ref, w_in_ref, b_in_ref, qkvo_ref, gates_ref, u_ref):
    q, k, v, o, gates, u = _in_proj(x_ref[...], g_mix_ref, w_in_ref, b_in_ref)
    qkvo_ref[...] = jnp.concatenate([q, k, v, o], axis=1)
    gates_ref[...] = gates
    u_ref[...] = u


def _sample_in_call(x, g_mix, w_in, b_in):
    R, D = x.shape
    tm = TILE_ROWS
    d_conv = (w_in.shape[1] - 4 * D_MLSTM - GATE_LANES) // 2
    sd = jax.ShapeDtypeStruct
    return pl.pallas_call(
        _sample_in_kernel,
        grid=(R // tm,),
        in_specs=[pl.BlockSpec((tm, D), lambda i: (i, 0))] + _weight_specs((g_mix, w_in, b_in)),
        out_specs=(pl.BlockSpec((tm, 4 * D_MLSTM), lambda i: (i, 0)),
                   pl.BlockSpec((tm, GATE_LANES), lambda i: (i, 0)),
                   pl.BlockSpec((tm, d_conv), lambda i: (i, 0))),
        out_shape=(sd((R, 4 * D_MLSTM), f32), sd((R, GATE_LANES), f32), sd((R, d_conv), f32)),
        compiler_params=pltpu.CompilerParams(
            dimension_semantics=("arbitrary",), vmem_limit_bytes=VMEM_LIMIT_BYTES),
        name="sample_in_proj",
    )(x, g_mix, w_in, b_in)


def _sample_rec_kernel(seq_len, qkvo_ref, gates_ref, u_ref, mrep_ref, c_ref, n_ref, hist_ref,
                       g_head_ref, w_sh_ref, b_dw_ref, g_cn_ref, b_cn_ref,
                       hm_ref, cact_ref, c_out_ref, n_out_ref, mt_ref, hist_out_ref,
                       numi_s, dec_s, gk_s, ch_s):
    rows = qkvo_ref.shape[0]
    nb = rows // seq_len
    shift = seq_len.bit_length() - 1
    d = D_MLSTM

    rid = lax.broadcasted_iota(jnp.int32, (rows, rows), 0)
    cid = lax.broadcasted_iota(jnp.int32, (rows, rows), 1)
    same = (rid >> shift) == (cid >> shift)
    mask = jnp.logical_and(same, cid <= rid)
    lcum16 = jnp.where(mask, 1.0, 0.0).astype(bf16)
    bsum16 = jnp.where(same, 1.0, 0.0).astype(bf16)
    plast16 = jnp.where(cid == ((rid >> shift) << shift) + (seq_len - 1), 1.0, 0.0).astype(bf16)
    lane_g = lax.broadcasted_iota(jnp.int32, (rows, GATE_LANES), 1)

    gates = gates_ref[...]
    lf, bc, gates_t, bc_t = _gate_cumsum(gates, lcum16)
    blast = _exact_left(bsum16, lf)
    mrep = mrep_ref[...]

    q = qkvo_ref[:, 0:d]
    k = qkvo_ref[:, d:2 * d]
    v = qkvo_ref[:, 2 * d:3 * d]
    o = qkvo_ref[:, 3 * d:4 * d]
    q16 = q.astype(bf16)
    k16 = k.astype(bf16)

    per_head = []
    for h in range(N_HEADS):
        cols = slice(h * DK, (h + 1) * DK)
        m_prev = mrep[:, F_LANE + h:F_LANE + h + 1]
        icol, bcol, m_t, w_intra, w_inter = _chunk_weights(gates, bc, gates_t, bc_t, h, mask, m_prev)
        vaug = _v_aug(v[:, cols])
        nd = _intra(q16[:, cols], k16[:, cols], vaug.astype(bf16), w_intra)
        per_head.append((icol, bcol, m_t, w_inter, nd))
    mt_all = _lane_pick([ph[2] for ph in per_head], lane_g)
    mt_ref[...] = mt_all
    mnew = _exact_left(plast16, mt_all)
    dec_s[...] = jnp.exp(blast + mrep - mnew)
    gvt16 = []
    for h in range(N_HEADS):
        cols = slice(h * DK, (h + 1) * DK)
        icol, bcol, _, _, _ = per_head[h]
        lanes = slice(F_LANE + h, F_LANE + h + 1)
        gcol = jnp.exp(((blast[:, lanes] - bcol) + icol) - mnew[:, lanes])
        gvt16.append((gcol * v[:, cols]).T.astype(bf16))
        gk_s[:, cols] = gcol * k[:, cols]

    rsel = lax.broadcasted_iota(jnp.int32, (rows, nb), 0)
    bsel = lax.broadcasted_iota(jnp.int32, (rows, nb), 1)
    expand16 = jnp.where((rsel >> shift) == bsel, 1.0, 0.0).astype(bf16)
    nrep = _exact_left(expand16, n_ref[...])

    grp = lax.broadcasted_iota(jnp.int32, (rows, DK), 0) >> shift
    numi_s[...] = jnp.zeros_like(numi_s)

    def per_seq(b, carry):
        mb = grp == b
        dec_b = dec_s[pl.ds(b * seq_len, 1), :]
        n_b = n_ref[pl.ds(b, 1), :]
        n_new = []
        for h in range(N_HEADS):
            cols = slice(h * DK, (h + 1) * DK)
            cb = c_ref[b, h]
            ni = _dot_nt(q16[:, cols], cb.astype(bf16))
            numi_s[:, cols] = jnp.where(mb, ni, numi_s[:, cols])
            kb = jnp.where(mb, k[:, cols], 0.0).astype(bf16)
            dec = dec_b[:, F_LANE + h:F_LANE + h + 1]
            c_out_ref[b, h] = dec * cb + _bdot(gvt16[h], kb)
            gk = jnp.sum(jnp.where(mb, gk_s[:, cols], 0.0), axis=0, keepdims=True)
            n_new.append(dec * n_b[:, cols] + gk)
        n_out_ref[pl.ds(b, 1), :] = jnp.concatenate(n_new, axis=1)
        return carry

    lax.fori_loop(0, nb, per_seq, 0)

    hm = []
    for h in range(N_HEADS):
        cols = slice(h * DK, (h + 1) * DK)
        _, _, m_t, w_inter, nd = per_head[h]
        den_i = jnp.sum(q[:, cols] * nrep[:, cols], axis=-1, keepdims=True)
        hh = _finish(nd, w_inter, numi_s[:, cols], den_i, m_t)
        hm.append(_head_out(hh, o[:, cols], g_head_ref[:, cols]))
    hm_ref[...] = jnp.concatenate(hm, axis=1)

    u = u_ref[...]
    for b in range(nb):
        hb = hist_ref[b]
        for t in range(seq_len):
            r = b * seq_len + t
            ch_s[r:r + 1, :] = jnp.sum(w_sh_ref[t, 0:CONV_BUF] * hb, axis=0, keepdims=True)
        hist_out_ref[b, 0:CONV_BUF - seq_len, :] = hist_ref[b, seq_len:CONV_BUF, :]
        hist_out_ref[b, CONV_BUF - seq_len:CONV_BUF, :] = u_ref[b * seq_len:(b + 1) * seq_len, :]
    tpos = lax.broadcasted_iota(jnp.int32, u.shape, 0) & (seq_len - 1)
    cu = w_sh_ref[0, CONV_BUF:CONV_BUF + 1, :] * u
    for dl in range(1, seq_len):
        cu = cu + w_sh_ref[dl, CONV_BUF:CONV_BUF + 1, :] * jnp.where(
            tpos >= dl, pltpu.roll(u, dl, axis=0), 0.0)
    c = (ch_s[...] + cu) + b_dw_ref[...]
    cact_ref[...] = _conv_post(c, g_cn_ref, b_cn_ref)


def _sample_rec_call(seq_len, qkvo, gates, u, mrep, c_state, n_state, hist,
                     g_head, w_sh, b_dw, g_cn, b_cn):
    R = qkvo.shape[0]
    nb_total = c_state.shape[0]
    gb = SAMPLE_GROUP
    rows = gb * seq_len
    d_conv = u.shape[1]
    sd = jax.ShapeDtypeStruct
    rspec = lambda w: pl.BlockSpec((rows, w), lambda i: (i, 0))
    return pl.pallas_call(
        functools.partial(_sample_rec_kernel, seq_len),
        grid=(nb_total // gb,),
        in_specs=[rspec(qkvo.shape[1]), rspec(GATE_LANES), rspec(d_conv), rspec(GATE_LANES),
                  pl.BlockSpec((gb, N_HEADS, DK, DK), lambda i: (i, 0, 0, 0)),
                  pl.BlockSpec((gb, D_MLSTM), lambda i: (i, 0)),
                  pl.BlockSpec((gb, CONV_BUF, d_conv), lambda i: (i, 0, 0))]
        + _weight_specs((g_head, w_sh, b_dw, g_cn, b_cn)),
        out_specs=(rspec(D_MLSTM), rspec(d_conv),
                   pl.BlockSpec((gb, N_HEADS, DK, DK), lambda i: (i, 0, 0, 0)),
                   pl.BlockSpec((gb, D_MLSTM), lambda i: (i, 0)),
                   rspec(GATE_LANES),
                   pl.BlockSpec((gb, CONV_BUF, d_conv), lambda i: (i, 0, 0))),
        out_shape=(sd((R, D_MLSTM), f32), sd((R, d_conv), f32),
                   sd(c_state.shape, f32), sd(n_state.shape, f32),
                   sd((R, GATE_LANES), f32), sd(hist.shape, f32)),
        scratch_shapes=[pltpu.VMEM((rows, D_MLSTM), f32), pltpu.VMEM((rows, GATE_LANES), f32),
                        pltpu.VMEM((rows, D_MLSTM), f32), pltpu.VMEM((rows, d_conv), f32)],
        compiler_params=pltpu.CompilerParams(
            dimension_semantics=("arbitrary",), vmem_limit_bytes=VMEM_LIMIT_BYTES),
        name="sample_recurrent",
    )(qkvo, gates, u, mrep, c_state, n_state, hist, g_head, w_sh, b_dw, g_cn, b_cn)


def _sample_tail_kernel(x_ref, hm_ref, cact_ref, p_ref, w_out_ref, g_ffn_ref, w_ff1_ref, w_ff2_ref,
                        g_ple_ref, w_gate_ref, w_proj_ref, g_final_ref, y_ref):
    y_ref[...] = _tail(x_ref[...], hm_ref[...], cact_ref[...], p_ref[...], w_out_ref, g_ffn_ref,
                       w_ff1_ref, w_ff2_ref, g_ple_ref, w_gate_ref, w_proj_ref, g_final_ref)


def _sample_tail_call(x, hm, cact, p, ws):
    R, D = x.shape
    tm = TILE_ROWS
    rspec = lambda w: pl.BlockSpec((tm, w), lambda i: (i, 0))
    return pl.pallas_call(
        _sample_tail_kernel,
        grid=(R // tm,),
        in_specs=[rspec(D), rspec(hm.shape[1]), rspec(cact.shape[1]), rspec(p.shape[1])]
        + _weight_specs(ws),
        out_specs=rspec(D),
        out_shape=jax.ShapeDtypeStruct((R, D), f32),
        compiler_params=pltpu.CompilerParams(
            dimension_semantics=("arbitrary",), vmem_limit_bytes=VMEM_LIMIT_BYTES),
        name="sample_tail",
    )(x, hm, cact, p, *ws)


def _layer_weights(i, g_mix, w_in, b_in, g_head, w_dw, b_dw, g_cn, b_cn, w_out, g_ffn, w_ff1,
                   w_ff2, g_ple, w_ple_gate, w_ple_proj):
    row = lambda a: a[i].reshape(1, -1).astype(f32)
    d4 = 4 * D_MLSTM
    n_gate = 2 * N_HEADS
    pad = GATE_LANES - n_gate
    w = w_in[i]
    w_in_p = jnp.concatenate(
        [w[:, :d4], jnp.pad(w[:, d4:d4 + n_gate], ((0, 0), (0, pad))), w[:, d4 + n_gate:]],
        axis=1).astype(bf16)
    b = b_in[i]
    b_in_p = jnp.concatenate(
        [b[:d4], jnp.pad(b[d4:d4 + n_gate], (0, pad)), b[d4 + n_gate:]]).reshape(1, -1).astype(f32)
    w_dw_p = jnp.pad(w_dw[i].astype(f32), ((0, HIST_ROWS - CONV_WIDTH), (0, 0)))
    return dict(
        g_mix=row(g_mix), w_in=w_in_p, b_in=b_in_p, g_head=row(g_head), w_dw=w_dw_p,
        b_dw=row(b_dw), g_cn=row(g_cn), b_cn=row(b_cn), w_out=w_out[i].astype(bf16),
        g_ffn=row(g_ffn), w_ff1=w_ff1[i].astype(bf16), w_ff2=w_ff2[i].astype(bf16),
        g_ple=row(g_ple), w_gate=w_ple_gate[i].astype(bf16), w_proj=w_ple_proj[i].astype(bf16))


def _shifted_taps(w_dw, seq_len):
    taps = []
    for t in range(seq_len):
        hist_part = jnp.pad(w_dw[:CONV_BUF - t], ((t, 0), (0, 0)))
        taps.append(jnp.concatenate([hist_part, w_dw[CONV_BUF - t:CONV_BUF - t + 1]], axis=0))
    w_sh = jnp.stack(taps).astype(f32)
    return jnp.pad(w_sh, ((0, 0), (0, HIST_ROWS - w_sh.shape[1]), (0, 0)))


def kernel(x_prompt, x_sample, state_mlstm_C, state_mlstm_n, state_mlstm_m, cache_conv, p_prompt,
           p_sample, g_mix, w_in, b_in, g_head, w_dw, b_dw, g_cn, b_cn, w_out, g_ffn, w_ff1, w_ff2,
           g_ple, w_ple_gate, w_ple_proj, g_final):
    depth = w_in.shape[0]
    assert depth == 1, "the final norm is fused into the layer kernels"
    bs, seq_len, d_model = x_sample.shape
    assert seq_len & (seq_len - 1) == 0 and seq_len <= CHUNK
    g_fin = g_final.reshape(1, -1).astype(f32)

    i = 0
    lw = _layer_weights(i, g_mix, w_in, b_in, g_head, w_dw, b_dw, g_cn, b_cn, w_out, g_ffn,
                        w_ff1, w_ff2, g_ple, w_ple_gate, w_ple_proj)
    tail_ws = (lw["w_out"], lw["g_ffn"], lw["w_ff1"], lw["w_ff2"], lw["g_ple"], lw["w_gate"],
               lw["w_proj"], g_fin)

    prompt_ws = (lw["g_mix"], lw["w_in"], lw["b_in"], lw["g_head"], lw["w_dw"], lw["b_dw"],
                 lw["g_cn"], lw["b_cn"]) + tail_ws
    y_p, c_p, n_p, m_p, conv_p = _prompt_call(x_prompt, p_prompt[i], prompt_ws)
    m_p = m_p[:, 0, F_LANE:F_LANE + N_HEADS]

    xs = x_sample.reshape(bs * seq_len, d_model)
    ps = p_sample[i].reshape(bs * seq_len, -1)
    qkvo, gates, u = _sample_in_call(xs, lw["g_mix"], lw["w_in"], lw["b_in"])
    m0 = jnp.pad(state_mlstm_m[i].astype(f32), ((0, 0), (F_LANE, GATE_LANES - F_LANE - N_HEADS)))
    mrep = jnp.repeat(m0, seq_len, axis=0)
    hm, cact, c_s, n_s, mt, conv_s = _sample_rec_call(
        seq_len, qkvo, gates, u, mrep, state_mlstm_C[i], state_mlstm_n[i].reshape(bs, -1),
        cache_conv[i], lw["g_head"], _shifted_taps(w_dw[i], seq_len), lw["b_dw"], lw["g_cn"],
        lw["b_cn"])
    y_s = _sample_tail_call(xs, hm, cact, ps, tail_ws).reshape(bs, seq_len, d_model)
    m_s = mt[seq_len - 1::seq_len, F_LANE:F_LANE + N_HEADS]
    n_s = n_s.reshape(bs, N_HEADS, DK)

    stack = lambda a: a[None]
    return (y_p, y_s, stack(c_p), stack(n_p), stack(m_p), stack(conv_p),
            stack(c_s), stack(n_s), stack(m_s), stack(conv_s))
```

```python
import functools

import jax
import jax.numpy as jnp
from jax import lax
from jax.experimental import pallas as pl
from jax.experimental.pallas import tpu as pltpu

f32 = jnp.float32
bf16 = jnp.bfloat16

N_HEADS = 4
DK = 128
D_MLSTM = N_HEADS * DK
CONV_WIDTH = 31
CONV_BUF = CONV_WIDTH - 1
EPS = 1e-6
GATE_LANES = 128
F_LANE = N_HEADS
CHUNK = 128
TILE_ROWS = 256
HIST_ROWS = 32
SAMPLE_GROUP = 32
VMEM_LIMIT_BYTES = 56 * 1024 * 1024


def _bdot(a, b):
    return jnp.dot(a, b, preferred_element_type=f32)


def _dot_nt(a, b):
    return lax.dot_general(a, b, (((1,), (1,)), ((), ())), preferred_element_type=f32)


def _dot_tn(a, b):
    return lax.dot_general(a, b, (((0,), (0,)), ((), ())), preferred_element_type=f32)


def _rms(x, g):
    y = x * lax.rsqrt(jnp.mean(x * x, axis=-1, keepdims=True) + EPS)
    return y * g


def _ln(x):
    mu = jnp.mean(x, axis=-1, keepdims=True)
    xc = x - mu
    return xc * lax.rsqrt(jnp.mean(xc * xc, axis=-1, keepdims=True) + EPS)


def _exact_left(sel16, x):
    hi = x.astype(bf16)
    r = x - hi.astype(f32)
    mid = r.astype(bf16)
    lo = (r - mid.astype(f32)).astype(bf16)
    return _bdot(sel16, hi) + _bdot(sel16, mid) + _bdot(sel16, lo)


def _lane_pick(rows, lane_ids):
    out = jnp.zeros((rows[0].shape[0], GATE_LANES), f32)
    for h, r in enumerate(rows):
        out = jnp.where(lane_ids == F_LANE + h, r, out)
    return out


def _in_proj(x, g_mix_ref, w_in_ref, b_in_ref):
    xn = _rms(x, g_mix_ref[...]).astype(bf16)

    def proj(a, b):
        return _bdot(xn, w_in_ref[:, a:b]) + b_in_ref[:, a:b]

    d = D_MLSTM
    q = proj(0, d)
    k = proj(d, 2 * d) * (DK ** -0.5)
    v = proj(2 * d, 3 * d)
    o = proj(3 * d, 4 * d)
    gates = proj(4 * d, 4 * d + GATE_LANES)
    c0 = 4 * d + GATE_LANES
    ga = proj(c0, c0 + d)
    gg = proj(c0 + d, c0 + 2 * d)
    u = ga * jax.nn.sigmoid(gg)
    return q, k, v, o, gates, u


def _gate_cumsum(gates, lcum16):
    lf = jax.nn.log_sigmoid(gates)
    bc = _exact_left(lcum16, lf)
    return lf, bc, gates.T, bc.T


def _chunk_weights(gates, bc, gates_t, bc_t, h, mask, m_prev):
    icol = gates[:, h:h + 1]
    bcol = bc[:, F_LANE + h:F_LANE + h + 1]
    irow = gates_t[h:h + 1, :]
    brow = bc_t[F_LANE + h:F_LANE + h + 1, :]
    logw = jnp.where(mask, (bcol - brow) + irow, -jnp.inf)
    m_intra = jnp.max(logw, axis=-1, keepdims=True)
    log_inter = bcol + m_prev
    m_t = jnp.maximum(log_inter, m_intra)
    w_intra = jnp.exp(logw - m_t)
    w_inter = jnp.exp(log_inter - m_t)
    return icol, bcol, m_t, w_intra, w_inter


def _v_aug(v):
    lane = lax.broadcasted_iota(jnp.int32, v.shape, 1)
    return jnp.concatenate([v, jnp.where(lane == 0, 1.0, 0.0)], axis=1)


def _intra(q16, k16, vaug16, w_intra):
    s = _dot_nt(q16, k16)
    a = (w_intra * s).astype(bf16)
    return _bdot(a, vaug16)


def _finish(nd, w_inter, num_inter, den_inter, m_t):
    num = nd[:, :DK] + w_inter * num_inter
    den = nd[:, DK:DK + 1] + w_inter * den_inter
    return num / jnp.maximum(jnp.abs(den), jnp.exp(-m_t))


def _head_out(h, o, g_head):
    return (_ln(h) * g_head) * jax.nn.sigmoid(o)


def _conv_post(c, g_cn_ref, b_cn_ref):
    c = _ln(c) * g_cn_ref[...] + b_cn_ref[...]
    return c * jax.nn.sigmoid(c)


def _out_proj(x, hm, c, w_out_ref):
    d = D_MLSTM
    return x + (_bdot(hm.astype(bf16), w_out_ref[0:d, :]) + _bdot(c.astype(bf16), w_out_ref[d:, :]))


def _ffn_part(xn16, w_ff1_ref, w_ff2_ref, j0, j1):
    f = jnp.maximum(_bdot(xn16, w_ff1_ref[:, j0:j1]), 0.0)
    return _bdot((f * f).astype(bf16), w_ff2_ref[j0:j1, :])


def _ple_final(x, p, g_ple_ref, w_gate_ref, w_proj_ref, g_final_ref):
    gate = jax.nn.sigmoid(_bdot(_rms(x, g_ple_ref[...]).astype(bf16), w_gate_ref[...]))
    x = x + gate * _bdot(p.astype(bf16), w_proj_ref[...])
    return _rms(x, g_final_ref[...])


def _tail(x, hm, c, p, w_out_ref, g_ffn_ref, w_ff1_ref, w_ff2_ref, g_ple_ref,
          w_gate_ref, w_proj_ref, g_final_ref):
    x = _out_proj(x, hm, c, w_out_ref)
    xn = _rms(x, g_ffn_ref[...]).astype(bf16)
    d_ff = w_ff1_ref.shape[1]
    step = 1024
    acc = None
    for j in range(0, d_ff, step):
        part = _ffn_part(xn, w_ff1_ref, w_ff2_ref, j, j + step)
        acc = part if acc is None else acc + part
    return _ple_final(x + acc, p, g_ple_ref, w_gate_ref, w_proj_ref, g_final_ref)


def _prompt_kernel(nt, xf_ref, xb_ref, p_ref, g_mix_ref, w_in_ref, b_in_ref, g_head_ref, w_dw_ref,
                   b_dw_ref, g_cn_ref, b_cn_ref, w_out_ref, g_ffn_ref, w_ff1_ref, w_ff2_ref,
                   g_ple_ref, w_gate_ref, w_proj_ref, g_final_ref,
                   y_ref, c_out_ref, n_out_ref, m_out_ref, conv_out_ref,
                   ct_s, m_s, ext_s, hm_s, cact_s, acc_s, xn_s, q_s, k_s, v_s, o_s, g_s, gc_s,
                   hh_s, cc_s):
    s = pl.program_id(0)
    n_tiles = pl.num_programs(0) - 1
    t = lax.rem(lax.rem(s, n_tiles), nt)
    tm = xf_ref.shape[0]
    off = HIST_ROWS - CONV_BUF

    @pl.when(s == 0)
    def _():
        hm_s[...] = jnp.zeros_like(hm_s)
        cact_s[...] = jnp.zeros_like(cact_s)

    @pl.when(t == 0)
    def _():
        ct_s[...] = jnp.zeros_like(ct_s)
        m_s[...] = jnp.zeros_like(m_s)
        ext_s[0:HIST_ROWS, :] = jnp.zeros((HIST_ROWS, ext_s.shape[1]), f32)

    x1 = _out_proj(xb_ref[...], hm_s[...], cact_s[...], w_out_ref)
    acc_s[...] = x1
    xn_s[...] = _rms(x1, g_ffn_ref[...]).astype(bf16)
    q, k, v, o, gates, u = _in_proj(xf_ref[...], g_mix_ref, w_in_ref, b_in_ref)
    q_s[...] = q.astype(bf16)
    k_s[...] = k.astype(bf16)
    v_s[...] = v
    o_s[...] = o
    g_s[...] = gates
    ext_s[HIST_ROWS:HIST_ROWS + tm, :] = u

    rid = lax.broadcasted_iota(jnp.int32, (CHUNK, CHUNK), 0)
    cid = lax.broadcasted_iota(jnp.int32, (CHUNK, CHUNK), 1)
    causal = cid <= rid
    lcum16 = jnp.where(causal, 1.0, 0.0).astype(bf16)
    lane_row = lax.broadcasted_iota(jnp.int32, (1, GATE_LANES), 1)

    def ffn_piece(j0, j1):
        acc_s[...] += _ffn_part(xn_s[...], w_ff1_ref, w_ff2_ref, j0, j1)

    def conv_block(r0, rb):
        sub = 8
        acc = None
        for ph in range(sub):
            n_rows = rb if ph == 0 else rb + sub
            grp = None
            for j in range(ph, off + CONV_WIDTH, sub):
                if j < off:
                    continue
                term = w_dw_ref[j - off:j - off + 1, :] * ext_s[r0 + j - ph:r0 + j - ph + n_rows, :]
                grp = term if grp is None else grp + term
            part = grp if ph == 0 else grp[ph:ph + rb]
            acc = part if acc is None else acc + part
        cc_s[r0:r0 + rb, :] = acc + b_dw_ref[...]

    def mlstm_piece(ci, h):
        rows = slice(ci * CHUNK, (ci + 1) * CHUNK)
        cols = slice(h * DK, (h + 1) * DK)
        gates_c = g_s[rows, :]
        if h == 0:
            _, bc, gates_t, bc_t = _gate_cumsum(gates_c, lcum16)
            gc_s[0] = bc
            gc_s[1] = gates_t
            gc_s[2] = bc_t
        m_row = m_s[...]
        m_prev = m_row[:, F_LANE + h:F_LANE + h + 1]
        icol, bcol, m_t, w_intra, w_inter = _chunk_weights(
            gates_c, gc_s[0], gc_s[1], gc_s[2], h, causal, m_prev)
        q16 = q_s[rows, cols]
        k16 = k_s[rows, cols]
        vaug = _v_aug(v_s[rows, cols])
        nd = _intra(q16, k16, vaug.astype(bf16), w_intra)
        ct = ct_s[h]
        qc = _bdot(q16, ct.astype(bf16))
        hh_s[rows, cols] = _finish(nd, w_inter, qc[:, :DK], qc[:, DK:DK + 1], m_t)
        m_new = m_t[CHUNK - 1:CHUNK]
        b_last = bcol[CHUNK - 1:CHUNK]
        gcol = jnp.exp(((b_last - bcol) + icol) - m_new)
        decay = jnp.exp(b_last + m_prev - m_new)
        ct_s[h] = decay * ct + _dot_tn(k16, (gcol * vaug).astype(bf16))
        m_s[...] = jnp.where(lane_row == F_LANE + h, m_new, m_row)

    n_piece = (tm // CHUNK) * N_HEADS
    d_ff = w_ff1_ref.shape[1]
    fstep, rb = d_ff // n_piece, tm // n_piece
    for j in range(n_piece):
        ffn_piece(j * fstep, (j + 1) * fstep)
        conv_block(j * rb, rb)
        mlstm_piece(j // N_HEADS, j % N_HEADS)

    y_ref[...] = _ple_final(acc_s[...], p_ref[...], g_ple_ref, w_gate_ref, w_proj_ref,
                            g_final_ref)
    hm = jnp.concatenate(
        [_head_out(hh_s[:, h * DK:(h + 1) * DK], o_s[:, h * DK:(h + 1) * DK],
                   g_head_ref[:, h * DK:(h + 1) * DK]) for h in range(N_HEADS)], axis=1)
    hm_s[...] = hm.astype(bf16)
    cact_s[...] = _conv_post(cc_s[...], g_cn_ref, b_cn_ref).astype(bf16)
    ext_s[0:HIST_ROWS, :] = ext_s[tm:tm + HIST_ROWS, :]

    @pl.when(jnp.logical_and(t == nt - 1, s < n_tiles))
    def _():
        for h in range(N_HEADS):
            ctt = ct_s[h].T
            c_out_ref[h] = ctt[:DK]
            n_out_ref[h:h + 1, :] = ctt[DK:DK + 1]
        m_out_ref[...] = jnp.broadcast_to(m_s[...], m_out_ref.shape)
        conv_out_ref[...] = ext_s[off:HIST_ROWS, :]


def _const_spec(shape):
    return pl.BlockSpec(shape, lambda *_: (0,) * len(shape), pipeline_mode=pl.Buffered(1))


def _weight_specs(ws):
    return [_const_spec(w.shape) for w in ws]


def _prompt_call(x, p, ws):
    B, T, D = x.shape
    tm = TILE_ROWS
    nt = T // tm
    n_tiles = B * nt
    d_conv = ws[4].shape[1]
    front = lambda s: (lax.rem(s, n_tiles) // nt, lax.rem(lax.rem(s, n_tiles), nt), 0)
    back = lambda s: (jnp.maximum(s - 1, 0) // nt, lax.rem(jnp.maximum(s - 1, 0), nt), 0)
    state = lambda s: (jnp.minimum(s, n_tiles - 1) // nt, 0, 0)
    sd = jax.ShapeDtypeStruct
    rows_buf = lambda w, dt: pltpu.VMEM((tm, w), dt)
    return pl.pallas_call(
        functools.partial(_prompt_kernel, nt),
        grid=(n_tiles + 1,),
        in_specs=[pl.BlockSpec((None, tm, D), front), pl.BlockSpec((None, tm, D), back),
                  pl.BlockSpec((None, tm, p.shape[2]), back)] + _weight_specs(ws),
        out_specs=(
            pl.BlockSpec((None, tm, D), back),
            pl.BlockSpec((None, N_HEADS, DK, DK), lambda s: state(s) + (0,)),
            pl.BlockSpec((None, N_HEADS, DK), state),
            pl.BlockSpec((None, 8, GATE_LANES), state),
            pl.BlockSpec((None, CONV_BUF, d_conv), state),
        ),
        out_shape=(
            sd((B, T, D), f32),
            sd((B, N_HEADS, DK, DK), f32),
            sd((B, N_HEADS, DK), f32),
            sd((B, 8, GATE_LANES), f32),
            sd((B, CONV_BUF, d_conv), f32),
        ),
        scratch_shapes=[
            pltpu.VMEM((N_HEADS, DK, 2 * DK), f32),
            pltpu.VMEM((1, GATE_LANES), f32),
            pltpu.VMEM((HIST_ROWS + tm, d_conv), f32),
            rows_buf(D_MLSTM, bf16), rows_buf(d_conv, bf16),
            rows_buf(D, f32), rows_buf(D, bf16),
            rows_buf(D_MLSTM, bf16), rows_buf(D_MLSTM, bf16),
            rows_buf(D_MLSTM, f32), rows_buf(D_MLSTM, f32),
            rows_buf(GATE_LANES, f32),
            pltpu.VMEM((3, CHUNK, GATE_LANES), f32),
            rows_buf(D_MLSTM, f32), rows_buf(d_conv, f32),
        ],
        compiler_params=pltpu.CompilerParams(
            dimension_semantics=("arbitrary",), vmem_limit_bytes=VMEM_LIMIT_BYTES),
        name="prompt_layer",
    )(x, x, p, *ws)


def _sample_in_kernel(x_ref, g_mix_ref, w_in_ref, b_in_ref, qkvo_ref, gates_ref, u_ref):
    q, k, v, o, gates, u = _in_proj(x_ref[...], g_mix_ref, w_in_ref, b_in_ref)
    qkvo_ref[...] = jnp.concatenate([q, k, v, o], axis=1)
    gates_ref[...] = gates
    u_ref[...] = u


def _sample_in_call(x, g_mix, w_in, b_in):
    R, D = x.shape
    tm = TILE_ROWS
    d_conv = (w_in.shape[1] - 4 * D_MLSTM - GATE_LANES) // 2
    sd = jax.ShapeDtypeStruct
    return pl.pallas_call(
        _sample_in_kernel,
        grid=(R // tm,),
        in_specs=[pl.BlockSpec((tm, D), lambda i: (i, 0))] + _weight_specs((g_mix, w_in, b_in)),
        out_specs=(pl.BlockSpec((tm, 4 * D_MLSTM), lambda i: (i, 0)),
                   pl.BlockSpec((tm, GATE_LANES), lambda i: (i, 0)),
                   pl.BlockSpec((tm, d_conv), lambda i: (i, 0))),
        out_shape=(sd((R, 4 * D_MLSTM), f32), sd((R, GATE_LANES), f32), sd((R, d_conv), f32)),
        compiler_params=pltpu.CompilerParams(
            dimension_semantics=("arbitrary",), vmem_limit_bytes=VMEM_LIMIT_BYTES),
        name="sample_in_proj",
    )(x, g_mix, w_in, b_in)


def _sample_rec_kernel(seq_len, qkvo_ref, gates_ref, u_ref, mrep_ref, c_ref, n_ref, hist_ref,
                       g_head_ref, w_sh_ref, b_dw_ref, g_cn_ref, b_cn_ref,
                       hm_ref, cact_ref, c_out_ref, n_out_ref, mt_ref, hist_out_ref,
                       numi_s, dec_s, gk_s, ch_s):
    rows = qkvo_ref.shape[0]
    nb = rows // seq_len
    shift = seq_len.bit_length() - 1
    d = D_MLSTM

    rid = lax.broadcasted_iota(jnp.int32, (rows, rows), 0)
    cid = lax.broadcasted_iota(jnp.int32, (rows, rows), 1)
    same = (rid >> shift) == (cid >> shift)
    mask = jnp.logical_and(same, cid <= rid)
    lcum16 = jnp.where(mask, 1.0, 0.0).astype(bf16)
    bsum16 = jnp.where(same, 1.0, 0.0).astype(bf16)
    plast16 = jnp.where(cid == ((rid >> shift) << shift) + (seq_len - 1), 1.0, 0.0).astype(bf16)
    lane_g = lax.broadcasted_iota(jnp.int32, (rows, GATE_LANES), 1)

    gates = gates_ref[...]
    lf, bc, gates_t, bc_t = _gate_cumsum(gates, lcum16)
    blast = _exact_left(bsum16, lf)
    mrep = mrep_ref[...]

    q = qkvo_ref[:, 0:d]
    k = qkvo_ref[:, d:2 * d]
    v = qkvo_ref[:, 2 * d:3 * d]
    o = qkvo_ref[:, 3 * d:4 * d]
    q16 = q.astype(bf16)
    k16 = k.astype(bf16)

    per_head = []
    for h in range(N_HEADS):
        cols = slice(h * DK, (h + 1) * DK)
        m_prev = mrep[:, F_LANE + h:F_LANE + h + 1]
        icol, bcol, m_t, w_intra, w_inter = _chunk_weights(gates, bc, gates_t, bc_t, h, mask, m_prev)
        vaug = _v_aug(v[:, cols])
        nd = _intra(q16[:, cols], k16[:, cols], vaug.astype(bf16), w_intra)
        per_head.append((icol, bcol, m_t, w_inter, nd))
    mt_all = _lane_pick([ph[2] for ph in per_head], lane_g)
    mt_ref[...] = mt_all
    mnew = _exact_left(plast16, mt_all)
    dec_s[...] = jnp.exp(blast + mrep - mnew)
    gvt16 = []
    for h in range(N_HEADS):
        cols = slice(h * DK, (h + 1) * DK)
        icol, bcol, _, _, _ = per_head[h]
        lanes = slice(F_LANE + h, F_LANE + h + 1)
        gcol = jnp.exp(((blast[:, lanes] - bcol) + icol) - mnew[:, lanes])
        gvt16.append((gcol * v[:, cols]).T.astype(bf16))
        gk_s[:, cols] = gcol * k[:, cols]

    rsel = lax.broadcasted_iota(jnp.int32, (rows, nb), 0)
    bsel = lax.broadcasted_iota(jnp.int32, (rows, nb), 1)
    expand16 = jnp.where((rsel >> shift) == bsel, 1.0, 0.0).astype(bf16)
    nrep = _exact_left(expand16, n_ref[...])

    grp = lax.broadcasted_iota(jnp.int32, (rows, DK), 0) >> shift
    numi_s[...] = jnp.zeros_like(numi_s)

    def per_seq(b, carry):
        mb = grp == b
        dec_b = dec_s[pl.ds(b * seq_len, 1), :]
        n_b = n_ref[pl.ds(b, 1), :]
        n_new = []
        for h in range(N_HEADS):
            cols = slice(h * DK, (h + 1) * DK)
            cb = c_ref[b, h]
            ni = _dot_nt(q16[:, cols], cb.astype(bf16))
            numi_s[:, cols] = jnp.where(mb, ni, numi_s[:, cols])
            kb = jnp.where(mb, k[:, cols], 0.0).astype(bf16)
            dec = dec_b[:, F_LANE + h:F_LANE + h + 1]
            c_out_ref[b, h] = dec * cb + _bdot(gvt16[h], kb)
            gk = jnp.sum(jnp.where(mb, gk_s[:, cols], 0.0), axis=0, keepdims=True)
            n_new.append(dec * n_b[:, cols] + gk)
        n_out_ref[pl.ds(b, 1), :] = jnp.concatenate(n_new, axis=1)
        return carry

    lax.fori_loop(0, nb, per_seq, 0)

    hm = []
    for h in range(N_HEADS):
        cols = slice(h * DK, (h + 1) * DK)
        _, _, m_t, w_inter, nd = per_head[h]
        den_i = jnp.sum(q[:, cols] * nrep[:, cols], axis=-1, keepdims=True)
        hh = _finish(nd, w_inter, numi_s[:, cols], den_i, m_t)
        hm.append(_head_out(hh, o[:, cols], g_head_ref[:, cols]))
    hm_ref[...] = jnp.concatenate(hm, axis=1)

    u = u_ref[...]
    for b in range(nb):
        hb = hist_ref[b]
        for t in range(seq_len):
            r = b * seq_len + t
            ch_s[r:r + 1, :] = jnp.sum(w_sh_ref[t, 0:CONV_BUF] * hb, axis=0, keepdims=True)
        hist_out_ref[b, 0:CONV_BUF - seq_len, :] = hist_ref[b, seq_len:CONV_BUF, :]
        hist_out_ref[b, CONV_BUF - seq_len:CONV_BUF, :] = u_ref[b * seq_len:(b + 1) * seq_len, :]
    tpos = lax.broadcasted_iota(jnp.int32, u.shape, 0) & (seq_len - 1)
    cu = w_sh_ref[0, CONV_BUF:CONV_BUF + 1, :] * u
    for dl in range(1, seq_len):
        cu = cu + w_sh_ref[dl, CONV_BUF:CONV_BUF + 1, :] * jnp.where(
            tpos >= dl, pltpu.roll(u, dl, axis=0), 0.0)
    c = (ch_s[...] + cu) + b_dw_ref[...]
    cact_ref[...] = _conv_post(c, g_cn_ref, b_cn_ref)


def _sample_rec_call(seq_len, qkvo, gates, u, mrep, c_state, n_state, hist,
                     g_head, w_sh, b_dw, g_cn, b_cn):
    R = qkvo.shape[0]
    nb_total = c_state.shape[0]
    gb = SAMPLE_GROUP
    rows = gb * seq_len
    d_conv = u.shape[1]
    sd = jax.ShapeDtypeStruct
    rspec = lambda w: pl.BlockSpec((rows, w), lambda i: (i, 0))
    return pl.pallas_call(
        functools.partial(_sample_rec_kernel, seq_len),
        grid=(nb_total // gb,),
        in_specs=[rspec(qkvo.shape[1]), rspec(GATE_LANES), rspec(d_conv), rspec(GATE_LANES),
                  pl.BlockSpec((gb, N_HEADS, DK, DK), lambda i: (i, 0, 0, 0)),
                  pl.BlockSpec((gb, D_MLSTM), lambda i: (i, 0)),
                  pl.BlockSpec((gb, CONV_BUF, d_conv), lambda i: (i, 0, 0))]
        + _weight_specs((g_head, w_sh, b_dw, g_cn, b_cn)),
        out_specs=(rspec(D_MLSTM), rspec(d_conv),
                   pl.BlockSpec((gb, N_HEADS, DK, DK), lambda i: (i, 0, 0, 0)),
                   pl.BlockSpec((gb, D_MLSTM), lambda i: (i, 0)),
                   rspec(GATE_LANES),
                   pl.BlockSpec((gb, CONV_BUF, d_conv), lambda i: (i, 0, 0))),
        out_shape=(sd((R, D_MLSTM), f32), sd((R, d_conv), f32),
                   sd(c_state.shape, f32), sd(n_state.shape, f32),
                   sd((R, GATE_LANES), f32), sd(hist.shape, f32)),
        scratch_shapes=[pltpu.VMEM((rows, D_MLSTM), f32), pltpu.VMEM((rows, GATE_LANES), f32),
                        pltpu.VMEM((rows, D_MLSTM), f32), pltpu.VMEM((rows, d_conv), f32)],
        compiler_params=pltpu.CompilerParams(
            dimension_semantics=("arbitrary",), vmem_limit_bytes=VMEM_LIMIT_BYTES),
        name="sample_recurrent",
    )(qkvo, gates, u, mrep, c_state, n_state, hist, g_head, w_sh, b_dw, g_cn, b_cn)


def _sample_tail_kernel(x_ref, hm_ref, cact_ref, p_ref, w_out_ref, g_ffn_ref, w_ff1_ref, w_ff2_ref,
                        g_ple_ref, w_gate_ref, w_proj_ref, g_final_ref, y_ref):
    y_ref[...] = _tail(x_ref[...], hm_ref[...], cact_ref[...], p_ref[...], w_out_ref, g_ffn_ref,
                       w_ff1_ref, w_ff2_ref, g_ple_ref, w_gate_ref, w_proj_ref, g_final_ref)


def _sample_tail_call(x, hm, cact, p, ws):
    R, D = x.shape
    tm = TILE_ROWS
    rspec = lambda w: pl.BlockSpec((tm, w), lambda i: (i, 0))
    return pl.pallas_call(
        _sample_tail_kernel,
        grid=(R // tm,),
        in_specs=[rspec(D), rspec(hm.shape[1]), rspec(cact.shape[1]), rspec(p.shape[1])]
        + _weight_specs(ws),
        out_specs=rspec(D),
        out_shape=jax.ShapeDtypeStruct((R, D), f32),
        compiler_params=pltpu.CompilerParams(
            dimension_semantics=("arbitrary",), vmem_limit_bytes=VMEM_LIMIT_BYTES),
        name="sample_tail",
    )(x, hm, cact, p, *ws)


def _layer_weights(i, g_mix, w_in, b_in, g_head, w_dw, b_dw, g_cn, b_cn, w_out, g_ffn, w_ff1,
                   w_ff2, g_ple, w_ple_gate, w_ple_proj):
    row = lambda a: a[i].reshape(1, -1).astype(f32)
    d4 = 4 * D_MLSTM
    n_gate = 2 * N_HEADS
    pad = GATE_LANES - n_gate
    w = w_in[i]
    w_in_p = jnp.concatenate(
        [w[:, :d4], jnp.pad(w[:, d4:d4 + n_gate], ((0, 0), (0, pad))), w[:, d4 + n_gate:]],
        axis=1).astype(bf16)
    b = b_in[i]
    b_in_p = jnp.concatenate(
        [b[:d4], jnp.pad(b[d4:d4 + n_gate], (0, pad)), b[d4 + n_gate:]]).reshape(1, -1).astype(f32)
    w_dw_p = jnp.pad(w_dw[i].astype(f32), ((0, HIST_ROWS - CONV_WIDTH), (0, 0)))
    return dict(
        g_mix=row(g_mix), w_in=w_in_p, b_in=b_in_p, g_head=row(g_head), w_dw=w_dw_p,
        b_dw=row(b_dw), g_cn=row(g_cn), b_cn=row(b_cn), w_out=w_out[i].astype(bf16),
        g_ffn=row(g_ffn), w_ff1=w_ff1[i].astype(bf16), w_ff2=w_ff2[i].astype(bf16),
        g_ple=row(g_ple), w_gate=w_ple_gate[i].astype(bf16), w_proj=w_ple_proj[i].astype(bf16))


def _shifted_taps(w_dw, seq_len):
    taps = []
    for t in range(seq_len):
        hist_part = jnp.pad(w_dw[:CONV_BUF - t], ((t, 0), (0, 0)))
        taps.append(jnp.concatenate([hist_part, w_dw[CONV_BUF - t:CONV_BUF - t + 1]], axis=0))
    w_sh = jnp.stack(taps).astype(f32)
    return jnp.pad(w_sh, ((0, 0), (0, HIST_ROWS - w_sh.shape[1]), (0, 0)))


def kernel(x_prompt, x_sample, state_mlstm_C, state_mlstm_n, state_mlstm_m, cache_conv, p_prompt,
           p_sample, g_mix, w_in, b_in, g_head, w_dw, b_dw, g_cn, b_cn, w_out, g_ffn, w_ff1, w_ff2,
           g_ple, w_ple_gate, w_ple_proj, g_final):
    depth = w_in.shape[0]
    assert depth == 1, "the final norm is fused into the layer kernels"
    bs, seq_len, d_model = x_sample.shape
    assert seq_len & (seq_len - 1) == 0 and seq_len <= CHUNK
    g_fin = g_final.reshape(1, -1).astype(f32)

    i = 0
    lw = _layer_weights(i, g_mix, w_in, b_in, g_head, w_dw, b_dw, g_cn, b_cn, w_out, g_ffn,
                        w_ff1, w_ff2, g_ple, w_ple_gate, w_ple_proj)
    tail_ws = (lw["w_out"], lw["g_ffn"], lw["w_ff1"], lw["w_ff2"], lw["g_ple"], lw["w_gate"],
               lw["w_proj"], g_fin)

    prompt_ws = (lw["g_mix"], lw["w_in"], lw["b_in"], lw["g_head"], lw["w_dw"], lw["b_dw"],
                 lw["g_cn"], lw["b_cn"]) + tail_ws
    y_p, c_p, n_p, m_p, conv_p = _prompt_call(x_prompt, p_prompt[i], prompt_ws)
    m_p = m_p[:, 0, F_LANE:F_LANE + N_HEADS]

    xs = x_sample.reshape(bs * seq_len, d_model)
    ps = p_sample[i].reshape(bs * seq_len, -1)
    qkvo, gates, u = _sample_in_call(xs, lw["g_mix"], lw["w_in"], lw["b_in"])
    m0 = jnp.pad(state_mlstm_m[i].astype(f32), ((0, 0), (F_LANE, GATE_LANES - F_LANE - N_HEADS)))
    mrep = jnp.repeat(m0, seq_len, axis=0)
    hm, cact, c_s, n_s, mt, conv_s = _sample_rec_call(
        seq_len, qkvo, gates, u, mrep, state_mlstm_C[i], state_mlstm_n[i].reshape(bs, -1),
        cache_conv[i], lw["g_head"], _shifted_taps(w_dw[i], seq_len), lw["b_dw"], lw["g_cn"],
        lw["b_cn"])
    y_s = _sample_tail_call(xs, hm, cact, ps, tail_ws).reshape(bs, seq_len, d_model)
    m_s = mt[seq_len - 1::seq_len, F_LANE:F_LANE + N_HEADS]
    n_s = n_s.reshape(bs, N_HEADS, DK)

    stack = lambda a: a[None]
    return (y_p, y_s, stack(c_p), stack(n_p), stack(m_p), stack(conv_p),
            stack(c_s), stack(n_s), stack(m_s), stack(conv_s))
```

```python
import functools

import jax
import jax.numpy as jnp
from jax import lax
from jax.experimental import pallas as pl
from jax.experimental.pallas import tpu as pltpu

f32 = jnp.float32
bf16 = jnp.bfloat16

N_HEADS = 4
DK = 128
D_MLSTM = N_HEADS * DK
CONV_WIDTH = 31
CONV_BUF = CONV_WIDTH - 1
EPS = 1e-6
GATE_LANES = 128
F_LANE = N_HEADS
CHUNK = 128
TILE_ROWS = 256
HIST_ROWS = 32
SAMPLE_GROUP = 32
VMEM_LIMIT_BYTES = 56 * 1024 * 1024


def _bdot(a, b):
    return jnp.dot(a, b, preferred_element_type=f32)


def _dot_nt(a, b):
    return lax.dot_general(a, b, (((1,), (1,)), ((), ())), preferred_element_type=f32)


def _dot_tn(a, b):
    return lax.dot_general(a, b, (((0,), (0,)), ((), ())), preferred_element_type=f32)


def _rms(x, g):
    y = x * lax.rsqrt(jnp.mean(x * x, axis=-1, keepdims=True) + EPS)
    return y * g


def _ln(x):
    mu = jnp.mean(x, axis=-1, keepdims=True)
    xc = x - mu
    return xc * lax.rsqrt(jnp.mean(xc * xc, axis=-1, keepdims=True) + EPS)


def _exact_left(sel16, x):
    hi = x.astype(bf16)
    r = x - hi.astype(f32)
    mid = r.astype(bf16)
    lo = (r - mid.astype(f32)).astype(bf16)
    return _bdot(sel16, hi) + _bdot(sel16, mid) + _bdot(sel16, lo)


def _lane_pick(rows, lane_ids):
    out = jnp.zeros((rows[0].shape[0], GATE_LANES), f32)
    for h, r in enumerate(rows):
        out = jnp.where(lane_ids == F_LANE + h, r, out)
    return out


def _in_proj(x, g_mix_ref, w_in_ref, b_in_ref):
    xn = _rms(x, g_mix_ref[...]).astype(bf16)

    def proj(a, b):
        return _bdot(xn, w_in_ref[:, a:b]) + b_in_ref[:, a:b]

    d = D_MLSTM
    q = proj(0, d)
    k = proj(d, 2 * d) * (DK ** -0.5)
    v = proj(2 * d, 3 * d)
    o = proj(3 * d, 4 * d)
    gates = proj(4 * d, 4 * d + GATE_LANES)
    c0 = 4 * d + GATE_LANES
    ga = proj(c0, c0 + d)
    gg = proj(c0 + d, c0 + 2 * d)
    u = ga * jax.nn.sigmoid(gg)
    return q, k, v, o, gates, u


def _gate_cumsum(gates, lcum16):
    lf = jax.nn.log_sigmoid(gates)
    bc = _exact_left(lcum16, lf)
    return lf, bc, gates.T, bc.T


def _chunk_weights(gates, bc, gates_t, bc_t, h, mask, m_prev):
    icol = gates[:, h:h + 1]
    bcol = bc[:, F_LANE + h:F_LANE + h + 1]
    irow = gates_t[h:h + 1, :]
    brow = bc_t[F_LANE + h:F_LANE + h + 1, :]
    logw = jnp.where(mask, (bcol - brow) + irow, -jnp.inf)
    m_intra = jnp.max(logw, axis=-1, keepdims=True)
    log_inter = bcol + m_prev
    m_t = jnp.maximum(log_inter, m_intra)
    w_intra = jnp.exp(logw - m_t)
    w_inter = jnp.exp(log_inter - m_t)
    return icol, bcol, m_t, w_intra, w_inter


def _v_aug(v):
    lane = lax.broadcasted_iota(jnp.int32, v.shape, 1)
    return jnp.concatenate([v, jnp.where(lane == 0, 1.0, 0.0)], axis=1)


def _intra(q16, k16, vaug16, w_intra):
    s = _dot_nt(q16, k16)
    a = (w_intra * s).astype(bf16)
    return _bdot(a, vaug16)


def _finish(nd, w_inter, num_inter, den_inter, m_t):
    num = nd[:, :DK] + w_inter * num_inter
    den = nd[:, DK:DK + 1] + w_inter * den_inter
    return num / jnp.maximum(jnp.abs(den), jnp.exp(-m_t))


def _head_out(h, o, g_head):
    return (_ln(h) * g_head) * jax.nn.sigmoid(o)


def _conv_post(c, g_cn_ref, b_cn_ref):
    c = _ln(c) * g_cn_ref[...] + b_cn_ref[...]
    return c * jax.nn.sigmoid(c)


def _out_proj(x, hm, c, w_out_ref):
    d = D_MLSTM
    return x + (_bdot(hm.astype(bf16), w_out_ref[0:d, :]) + _bdot(c.astype(bf16), w_out_ref[d:, :]))


def _ffn_part(xn16, w_ff1_ref, w_ff2_ref, j0, j1):
    f = jnp.maximum(_bdot(xn16, w_ff1_ref[:, j0:j1]), 0.0)
    return _bdot((f * f).astype(bf16), w_ff2_ref[j0:j1, :])


def _ple_final(x, p, g_ple_ref, w_gate_ref, w_proj_ref, g_final_ref):
    gate = jax.nn.sigmoid(_bdot(_rms(x, g_ple_ref[...]).astype(bf16), w_gate_ref[...]))
    x = x + gate * _bdot(p.astype(bf16), w_proj_ref[...])
    return _rms(x, g_final_ref[...])


def _tail(x, hm, c, p, w_out_ref, g_ffn_ref, w_ff1_ref, w_ff2_ref, g_ple_ref,
          w_gate_ref, w_proj_ref, g_final_ref):
    x = _out_proj(x, hm, c, w_out_ref)
    xn = _rms(x, g_ffn_ref[...]).astype(bf16)
    d_ff = w_ff1_ref.shape[1]
    step = 1024
    acc = None
    for j in range(0, d_ff, step):
        part = _ffn_part(xn, w_ff1_ref, w_ff2_ref, j, j + step)
        acc = part if acc is None else acc + part
    return _ple_final(x + acc, p, g_ple_ref, w_gate_ref, w_proj_ref, g_final_ref)


def _cumsum_rows(x):
    sub = 8
    pos = lax.broadcasted_iota(jnp.int32, x.shape, 0) & (sub - 1)
    sh = 1
    while sh < sub:
        x = x + jnp.where(pos >= sh, pltpu.roll(x, sh, axis=0), 0.0)
        sh *= 2
    tiles = []
    carry = None
    for r0 in range(0, x.shape[0], sub):
        tile = x[r0:r0 + sub]
        if carry is not None:
            tile = tile + carry
        carry = tile[sub - 1:sub]
        tiles.append(tile)
    return jnp.concatenate(tiles, axis=0)


def _prompt_kernel(nt, xf_ref, xb_ref, p_ref, g_mix_ref, w_in_ref, b_in_ref, g_head_ref, w_dw_ref,
                   b_dw_ref, g_cn_ref, b_cn_ref, w_out_ref, g_ffn_ref, w_ff1_ref, w_ff2_ref,
                   g_ple_ref, w_gate_ref, w_proj_ref, g_final_ref,
                   y_ref, c_out_ref, n_out_ref, m_out_ref, conv_out_ref,
                   ct_s, m_s, ext_s, hm_s, cact_s, acc_s, xn_s, q_s, k_s, v_s, o_s, g_s, gc_s,
                   hh_s, cc_s):
    s = pl.program_id(0)
    n_tiles = pl.num_programs(0) - 1
    t = lax.rem(lax.rem(s, n_tiles), nt)
    tm = xf_ref.shape[0]
    off = HIST_ROWS - CONV_BUF

    @pl.when(s == 0)
    def _():
        hm_s[...] = jnp.zeros_like(hm_s)
        cact_s[...] = jnp.zeros_like(cact_s)

    @pl.when(t == 0)
    def _():
        ct_s[...] = jnp.zeros_like(ct_s)
        m_s[...] = jnp.zeros_like(m_s)
        ext_s[0:HIST_ROWS, :] = jnp.zeros((HIST_ROWS, ext_s.shape[1]), f32)

    x1 = _out_proj(xb_ref[...], hm_s[...], cact_s[...], w_out_ref)
    acc_s[...] = x1
    xn_s[...] = _rms(x1, g_ffn_ref[...]).astype(bf16)
    q, k, v, o, gates, u = _in_proj(xf_ref[...], g_mix_ref, w_in_ref, b_in_ref)
    q_s[...] = q.astype(bf16)
    k_s[...] = k.astype(bf16)
    v_s[...] = v
    o_s[...] = o
    g_s[...] = gates
    ext_s[HIST_ROWS:HIST_ROWS + tm, :] = u

    rid = lax.broadcasted_iota(jnp.int32, (CHUNK, CHUNK), 0)
    cid = lax.broadcasted_iota(jnp.int32, (CHUNK, CHUNK), 1)
    causal = cid <= rid
    lane_row = lax.broadcasted_iota(jnp.int32, (1, GATE_LANES), 1)

    def ffn_piece(j0, j1):
        acc_s[...] += _ffn_part(xn_s[...], w_ff1_ref, w_ff2_ref, j0, j1)

    def conv_block(r0, rb):
        sub = 8
        acc = None
        for ph in range(sub):
            n_rows = rb if ph == 0 else rb + sub
            grp = None
            for j in range(ph, off + CONV_WIDTH, sub):
                if j < off:
                    continue
                term = w_dw_ref[j - off:j - off + 1, :] * ext_s[r0 + j - ph:r0 + j - ph + n_rows, :]
                grp = term if grp is None else grp + term
            part = grp if ph == 0 else grp[ph:ph + rb]
            acc = part if acc is None else acc + part
        cc_s[r0:r0 + rb, :] = acc + b_dw_ref[...]

    def mlstm_piece(ci, h):
        rows = slice(ci * CHUNK, (ci + 1) * CHUNK)
        cols = slice(h * DK, (h + 1) * DK)
        gates_c = g_s[rows, :]
        if h == 0:
            bc = _cumsum_rows(jax.nn.log_sigmoid(gates_c))
            gc_s[0] = bc
            gc_s[1] = gates_c.T
            gc_s[2] = bc.T
        m_row = m_s[...]
        m_prev = m_row[:, F_LANE + h:F_LANE + h + 1]
        icol, bcol, m_t, w_intra, w_inter = _chunk_weights(
            gates_c, gc_s[0], gc_s[1], gc_s[2], h, causal, m_prev)
        q16 = q_s[rows, cols]
        k16 = k_s[rows, cols]
        vaug = _v_aug(v_s[rows, cols])
        nd = _intra(q16, k16, vaug.astype(bf16), w_intra)
        ct = ct_s[h]
        qc = _bdot(q16, ct.astype(bf16))
        hh_s[rows, cols] = _finish(nd, w_inter, qc[:, :DK], qc[:, DK:DK + 1], m_t)
        m_new = m_t[CHUNK - 1:CHUNK]
        b_last = bcol[CHUNK - 1:CHUNK]
        gcol = jnp.exp(((b_last - bcol) + icol) - m_new)
        decay = jnp.exp(b_last + m_prev - m_new)
        ct_s[h] = decay * ct + _dot_tn(k16, (gcol * vaug).astype(bf16))
        m_s[...] = jnp.where(lane_row == F_LANE + h, m_new, m_row)

    n_piece = (tm // CHUNK) * N_HEADS
    d_ff = w_ff1_ref.shape[1]
    fstep, rb = d_ff // n_piece, tm // n_piece
    for j in range(n_piece):
        ffn_piece(j * fstep, (j + 1) * fstep)
        conv_block(j * rb, rb)
        mlstm_piece(j // N_HEADS, j % N_HEADS)

    y_ref[...] = _ple_final(acc_s[...], p_ref[...], g_ple_ref, w_gate_ref, w_proj_ref,
                            g_final_ref)
    hm = jnp.concatenate(
        [_head_out(hh_s[:, h * DK:(h + 1) * DK], o_s[:, h * DK:(h + 1) * DK],
                   g_head_ref[:, h * DK:(h + 1) * DK]) for h in range(N_HEADS)], axis=1)
    hm_s[...] = hm.astype(bf16)
    cact_s[...] = _conv_post(cc_s[...], g_cn_ref, b_cn_ref).astype(bf16)
    ext_s[0:HIST_ROWS, :] = ext_s[tm:tm + HIST_ROWS, :]

    @pl.when(jnp.logical_and(t == nt - 1, s < n_tiles))
    def _():
        for h in range(N_HEADS):
            ctt = ct_s[h].T
            c_out_ref[h] = ctt[:DK]
            n_out_ref[h:h + 1, :] = ctt[DK:DK + 1]
        m_out_ref[...] = jnp.broadcast_to(m_s[...], m_out_ref.shape)
        conv_out_ref[...] = ext_s[off:HIST_ROWS, :]


def _const_spec(shape):
    return pl.BlockSpec(shape, lambda *_: (0,) * len(shape), pipeline_mode=pl.Buffered(1))


def _weight_specs(ws):
    return [_const_spec(w.shape) for w in ws]


def _prompt_call(x, p, ws):
    B, T, D = x.shape
    tm = TILE_ROWS
    nt = T // tm
    n_tiles = B * nt
    d_conv = ws[4].shape[1]
    front = lambda s: (lax.rem(s, n_tiles) // nt, lax.rem(lax.rem(s, n_tiles), nt), 0)
    back = lambda s: (jnp.maximum(s - 1, 0) // nt, lax.rem(jnp.maximum(s - 1, 0), nt), 0)
    state = lambda s: (jnp.minimum(s, n_tiles - 1) // nt, 0, 0)
    sd = jax.ShapeDtypeStruct
    rows_buf = lambda w, dt: pltpu.VMEM((tm, w), dt)
    return pl.pallas_call(
        functools.partial(_prompt_kernel, nt),
        grid=(n_tiles + 1,),
        in_specs=[pl.BlockSpec((None, tm, D), front), pl.BlockSpec((None, tm, D), back),
                  pl.BlockSpec((None, tm, p.shape[2]), back)] + _weight_specs(ws),
        out_specs=(
            pl.BlockSpec((None, tm, D), back),
            pl.BlockSpec((None, N_HEADS, DK, DK), lambda s: state(s) + (0,)),
            pl.BlockSpec((None, N_HEADS, DK), state),
            pl.BlockSpec((None, 8, GATE_LANES), state),
            pl.BlockSpec((None, CONV_BUF, d_conv), state),
        ),
        out_shape=(
            sd((B, T, D), f32),
            sd((B, N_HEADS, DK, DK), f32),
            sd((B, N_HEADS, DK), f32),
            sd((B, 8, GATE_LANES), f32),
            sd((B, CONV_BUF, d_conv), f32),
        ),
        scratch_shapes=[
            pltpu.VMEM((N_HEADS, DK, 2 * DK), f32),
            pltpu.VMEM((1, GATE_LANES), f32),
            pltpu.VMEM((HIST_ROWS + tm, d_conv), f32),
            rows_buf(D_MLSTM, bf16), rows_buf(d_conv, bf16),
            rows_buf(D, f32), rows_buf(D, bf16),
            rows_buf(D_MLSTM, bf16), rows_buf(D_MLSTM, bf16),
            rows_buf(D_MLSTM, f32), rows_buf(D_MLSTM, f32),
            rows_buf(GATE_LANES, f32),
            pltpu.VMEM((3, CHUNK, GATE_LANES), f32),
            rows_buf(D_MLSTM, f32), rows_buf(d_conv, f32),
        ],
        compiler_params=pltpu.CompilerParams(
            dimension_semantics=("arbitrary",), vmem_limit_bytes=VMEM_LIMIT_BYTES),
        name="prompt_layer",
    )(x, x, p, *ws)


def _sample_in_kernel(x_ref, g_mix_ref, w_in_ref, b_in_ref, qkvo_ref, gates_ref, u_ref):
    q, k, v, o, gates, u = _in_proj(x_ref[...], g_mix_ref, w_in_ref, b_in_ref)
    qkvo_ref[...] = jnp.concatenate([q, k, v, o], axis=1)
    gates_ref[...] = gates
    u_ref[...] = u


def _sample_in_call(x, g_mix, w_in, b_in):
    R, D = x.shape
    tm = TILE_ROWS
    d_conv = (w_in.shape[1] - 4 * D_MLSTM - GATE_LANES) // 2
    sd = jax.ShapeDtypeStruct
    return pl.pallas_call(
        _sample_in_kernel,
        grid=(R // tm,),
        in_specs=[pl.BlockSpec((tm, D), lambda i: (i, 0))] + _weight_specs((g_mix, w_in, b_in)),
        out_specs=(pl.BlockSpec((tm, 4 * D_MLSTM), lambda i: (i, 0)),
                   pl.BlockSpec((tm, GATE_LANES), lambda i: (i, 0)),
                   pl.BlockSpec((tm, d_conv), lambda i: (i, 0))),
        out_shape=(sd((R, 4 * D_MLSTM), f32), sd((R, GATE_LANES), f32), sd((R, d_conv), f32)),
        compiler_params=pltpu.CompilerParams(
            dimension_semantics=("arbitrary",), vmem_limit_bytes=VMEM_LIMIT_BYTES),
        name="sample_in_proj",
    )(x, g_mix, w_in, b_in)


def _sample_rec_kernel(seq_len, qkvo_ref, gates_ref, u_ref, mrep_ref, c_ref, n_ref, hist_ref,
                       g_head_ref, w_sh_ref, b_dw_ref, g_cn_ref, b_cn_ref,
                       hm_ref, cact_ref, c_out_ref, n_out_ref, mt_ref, hist_out_ref,
                       numi_s, dec_s, gk_s, ch_s):
    rows = qkvo_ref.shape[0]
    nb = rows // seq_len
    shift = seq_len.bit_length() - 1
    d = D_MLSTM

    rid = lax.broadcasted_iota(jnp.int32, (rows, rows), 0)
    cid = lax.broadcasted_iota(jnp.int32, (rows, rows), 1)
    same = (rid >> shift) == (cid >> shift)
    mask = jnp.logical_and(same, cid <= rid)
    lcum16 = jnp.where(mask, 1.0, 0.0).astype(bf16)
    bsum16 = jnp.where(same, 1.0, 0.0).astype(bf16)
    plast16 = jnp.where(cid == ((rid >> shift) << shift) + (seq_len - 1), 1.0, 0.0).astype(bf16)
    lane_g = lax.broadcasted_iota(jnp.int32, (rows, GATE_LANES), 1)

    gates = gates_ref[...]
    lf, bc, gates_t, bc_t = _gate_cumsum(gates, lcum16)
    blast = _exact_left(bsum16, lf)
    mrep = mrep_ref[...]

    q = qkvo_ref[:, 0:d]
    k = qkvo_ref[:, d:2 * d]
    v = qkvo_ref[:, 2 * d:3 * d]
    o = qkvo_ref[:, 3 * d:4 * d]
    q16 = q.astype(bf16)
    k16 = k.astype(bf16)

    per_head = []
    for h in range(N_HEADS):
        cols = slice(h * DK, (h + 1) * DK)
        m_prev = mrep[:, F_LANE + h:F_LANE + h + 1]
        icol, bcol, m_t, w_intra, w_inter = _chunk_weights(gates, bc, gates_t, bc_t, h, mask, m_prev)
        vaug = _v_aug(v[:, cols])
        nd = _intra(q16[:, cols], k16[:, cols], vaug.astype(bf16), w_intra)
        per_head.append((icol, bcol, m_t, w_inter, nd))
    mt_all = _lane_pick([ph[2] for ph in per_head], lane_g)
    mt_ref[...] = mt_all
    mnew = _exact_left(plast16, mt_all)
    dec_s[...] = jnp.exp(blast + mrep - mnew)
    gvt16 = []
    for h in range(N_HEADS):
        cols = slice(h * DK, (h + 1) * DK)
        icol, bcol, _, _, _ = per_head[h]
        lanes = slice(F_LANE + h, F_LANE + h + 1)
        gcol = jnp.exp(((blast[:, lanes] - bcol) + icol) - mnew[:, lanes])
        gvt16.append((gcol * v[:, cols]).T.astype(bf16))
        gk_s[:, cols] = gcol * k[:, cols]

    rsel = lax.broadcasted_iota(jnp.int32, (rows, nb), 0)
    bsel = lax.broadcasted_iota(jnp.int32, (rows, nb), 1)
    expand16 = jnp.where((rsel >> shift) == bsel, 1.0, 0.0).astype(bf16)
    nrep = _exact_left(expand16, n_ref[...])

    grp = lax.broadcasted_iota(jnp.int32, (rows, DK), 0) >> shift
    numi_s[...] = jnp.zeros_like(numi_s)

    def per_seq(b, carry):
        mb = grp == b
        dec_b = dec_s[pl.ds(b * seq_len, 1), :]
        n_b = n_ref[pl.ds(b, 1), :]
        n_new = []
        for h in range(N_HEADS):
            cols = slice(h * DK, (h + 1) * DK)
            cb = c_ref[b, h]
            ni = _dot_nt(q16[:, cols], cb.astype(bf16))
            numi_s[:, cols] = jnp.where(mb, ni, numi_s[:, cols])
            kb = jnp.where(mb, k[:, cols], 0.0).astype(bf16)
            dec = dec_b[:, F_LANE + h:F_LANE + h + 1]
            c_out_ref[b, h] = dec * cb + _bdot(gvt16[h], kb)
            gk = jnp.sum(jnp.where(mb, gk_s[:, cols], 0.0), axis=0, keepdims=True)
            n_new.append(dec * n_b[:, cols] + gk)
        n_out_ref[pl.ds(b, 1), :] = jnp.concatenate(n_new, axis=1)
        return carry

    lax.fori_loop(0, nb, per_seq, 0)

    hm = []
    for h in range(N_HEADS):
        cols = slice(h * DK, (h + 1) * DK)
        _, _, m_t, w_inter, nd = per_head[h]
        den_i = jnp.sum(q[:, cols] * nrep[:, cols], axis=-1, keepdims=True)
        hh = _finish(nd, w_inter, numi_s[:, cols], den_i, m_t)
        hm.append(_head_out(hh, o[:, cols], g_head_ref[:, cols]))
    hm_ref[...] = jnp.concatenate(hm, axis=1)

    u = u_ref[...]
    for b in range(nb):
        hb = hist_ref[b]
        for t in range(seq_len):
            r = b * seq_len + t
            ch_s[r:r + 1, :] = jnp.sum(w_sh_ref[t, 0:CONV_BUF] * hb, axis=0, keepdims=True)
        hist_out_ref[b, 0:CONV_BUF - seq_len, :] = hist_ref[b, seq_len:CONV_BUF, :]
        hist_out_ref[b, CONV_BUF - seq_len:CONV_BUF, :] = u_ref[b * seq_len:(b + 1) * seq_len, :]
    tpos = lax.broadcasted_iota(jnp.int32, u.shape, 0) & (seq_len - 1)
    cu = w_sh_ref[0, CONV_BUF:CONV_BUF + 1, :] * u
    for dl in range(1, seq_len):
        cu = cu + w_sh_ref[dl, CONV_BUF:CONV_BUF + 1, :] * jnp.where(
            tpos >= dl, pltpu.roll(u, dl, axis=0), 0.0)
    c = (ch_s[...] + cu) + b_dw_ref[...]
    cact_ref[...] = _conv_post(c, g_cn_ref, b_cn_ref)


def _sample_rec_call(seq_len, qkvo, gates, u, mrep, c_state, n_state, hist,
                     g_head, w_sh, b_dw, g_cn, b_cn):
    R = qkvo.shape[0]
    nb_total = c_state.shape[0]
    gb = SAMPLE_GROUP
    rows = gb * seq_len
    d_conv = u.shape[1]
    sd = jax.ShapeDtypeStruct
    rspec = lambda w: pl.BlockSpec((rows, w), lambda i: (i, 0))
    return pl.pallas_call(
        functools.partial(_sample_rec_kernel, seq_len),
        grid=(nb_total // gb,),
        in_specs=[rspec(qkvo.shape[1]), rspec(GATE_LANES), rspec(d_conv), rspec(GATE_LANES),
                  pl.BlockSpec((gb, N_HEADS, DK, DK), lambda i: (i, 0, 0, 0)),
                  pl.BlockSpec((gb, D_MLSTM), lambda i: (i, 0)),
                  pl.BlockSpec((gb, CONV_BUF, d_conv), lambda i: (i, 0, 0))]
        + _weight_specs((g_head, w_sh, b_dw, g_cn, b_cn)),
        out_specs=(rspec(D_MLSTM), rspec(d_conv),
                   pl.BlockSpec((gb, N_HEADS, DK, DK), lambda i: (i, 0, 0, 0)),
                   pl.BlockSpec((gb, D_MLSTM), lambda i: (i, 0)),
                   rspec(GATE_LANES),
                   pl.BlockSpec((gb, CONV_BUF, d_conv), lambda i: (i, 0, 0))),
        out_shape=(sd((R, D_MLSTM), f32), sd((R, d_conv), f32),
                   sd(c_state.shape, f32), sd(n_state.shape, f32),
                   sd((R, GATE_LANES), f32), sd(hist.shape, f32)),
        scratch_shapes=[pltpu.VMEM((rows, D_MLSTM), f32), pltpu.VMEM((rows, GATE_LANES), f32),
                        pltpu.VMEM((rows, D_MLSTM), f32), pltpu.VMEM((rows, d_conv), f32)],
        compiler_params=pltpu.CompilerParams(
            dimension_semantics=("arbitrary",), vmem_limit_bytes=VMEM_LIMIT_BYTES),
        name="sample_recurrent",
    )(qkvo, gates, u, mrep, c_state, n_state, hist, g_head, w_sh, b_dw, g_cn, b_cn)


def _sample_tail_kernel(x_ref, hm_ref, cact_ref, p_ref, w_out_ref, g_ffn_ref, w_ff1_ref, w_ff2_ref,
                        g_ple_ref, w_gate_ref, w_proj_ref, g_final_ref, y_ref):
    y_ref[...] = _tail(x_ref[...], hm_ref[...], cact_ref[...], p_ref[...], w_out_ref, g_ffn_ref,
                       w_ff1_ref, w_ff2_ref, g_ple_ref, w_gate_ref, w_proj_ref, g_final_ref)


def _sample_tail_call(x, hm, cact, p, ws):
    R, D = x.shape
    tm = TILE_ROWS
    rspec = lambda w: pl.BlockSpec((tm, w), lambda i: (i, 0))
    return pl.pallas_call(
        _sample_tail_kernel,
        grid=(R // tm,),
        in_specs=[rspec(D), rspec(hm.shape[1]), rspec(cact.shape[1]), rspec(p.shape[1])]
        + _weight_specs(ws),
        out_specs=rspec(D),
        out_shape=jax.ShapeDtypeStruct((R, D), f32),
        compiler_params=pltpu.CompilerParams(
            dimension_semantics=("arbitrary",), vmem_limit_bytes=VMEM_LIMIT_BYTES),
        name="sample_tail",
    )(x, hm, cact, p, *ws)


def _layer_weights(i, g_mix, w_in, b_in, g_head, w_dw, b_dw, g_cn, b_cn, w_out, g_ffn, w_ff1,
                   w_ff2, g_ple, w_ple_gate, w_ple_proj):
    row = lambda a: a[i].reshape(1, -1).astype(f32)
    d4 = 4 * D_MLSTM
    n_gate = 2 * N_HEADS
    pad = GATE_LANES - n_gate
    w = w_in[i]
    w_in_p = jnp.concatenate(
        [w[:, :d4], jnp.pad(w[:, d4:d4 + n_gate], ((0, 0), (0, pad))), w[:, d4 + n_gate:]],
        axis=1).astype(bf16)
    b = b_in[i]
    b_in_p = jnp.concatenate(
        [b[:d4], jnp.pad(b[d4:d4 + n_gate], (0, pad)), b[d4 + n_gate:]]).reshape(1, -1).astype(f32)
    w_dw_p = jnp.pad(w_dw[i].astype(f32), ((0, HIST_ROWS - CONV_WIDTH), (0, 0)))
    return dict(
        g_mix=row(g_mix), w_in=w_in_p, b_in=b_in_p, g_head=row(g_head), w_dw=w_dw_p,
        b_dw=row(b_dw), g_cn=row(g_cn), b_cn=row(b_cn), w_out=w_out[i].astype(bf16),
        g_ffn=row(g_ffn), w_ff1=w_ff1[i].astype(bf16), w_ff2=w_ff2[i].astype(bf16),
        g_ple=row(g_ple), w_gate=w_ple_gate[i].astype(bf16), w_proj=w_ple_proj[i].astype(bf16))


def _shifted_taps(w_dw, seq_len):
    taps = []
    for t in range(seq_len):
        hist_part = jnp.pad(w_dw[:CONV_BUF - t], ((t, 0), (0, 0)))
        taps.append(jnp.concatenate([hist_part, w_dw[CONV_BUF - t:CONV_BUF - t + 1]], axis=0))
    w_sh = jnp.stack(taps).astype(f32)
    return jnp.pad(w_sh, ((0, 0), (0, HIST_ROWS - w_sh.shape[1]), (0, 0)))


def kernel(x_prompt, x_sample, state_mlstm_C, state_mlstm_n, state_mlstm_m, cache_conv, p_prompt,
           p_sample, g_mix, w_in, b_in, g_head, w_dw, b_dw, g_cn, b_cn, w_out, g_ffn, w_ff1, w_ff2,
           g_ple, w_ple_gate, w_ple_proj, g_final):
    depth = w_in.shape[0]
    assert depth == 1, "the final norm is fused into the layer kernels"
    bs, seq_len, d_model = x_sample.shape
    assert seq_len & (seq_len - 1) == 0 and seq_len <= CHUNK
    g_fin = g_final.reshape(1, -1).astype(f32)

    i = 0
    lw = _layer_weights(i, g_mix, w_in, b_in, g_head, w_dw, b_dw, g_cn, b_cn, w_out, g_ffn,
                        w_ff1, w_ff2, g_ple, w_ple_gate, w_ple_proj)
    tail_ws = (lw["w_out"], lw["g_ffn"], lw["w_ff1"], lw["w_ff2"], lw["g_ple"], lw["w_gate"],
               lw["w_proj"], g_fin)

    prompt_ws = (lw["g_mix"], lw["w_in"], lw["b_in"], lw["g_head"], lw["w_dw"], lw["b_dw"],
                 lw["g_cn"], lw["b_cn"]) + tail_ws
    y_p, c_p, n_p, m_p, conv_p = _prompt_call(x_prompt, p_prompt[i], prompt_ws)
    m_p = m_p[:, 0, F_LANE:F_LANE + N_HEADS]

    xs = x_sample.reshape(bs * seq_len, d_model)
    ps = p_sample[i].reshape(bs * seq_len, -1)
    qkvo, gates, u = _sample_in_call(xs, lw["g_mix"], lw["w_in"], lw["b_in"])
    m0 = jnp.pad(state_mlstm_m[i].astype(f32), ((0, 0), (F_LANE, GATE_LANES - F_LANE - N_HEADS)))
    mrep = jnp.repeat(m0, seq_len, axis=0)
    hm, cact, c_s, n_s, mt, conv_s = _sample_rec_call(
        seq_len, qkvo, gates, u, mrep, state_mlstm_C[i], state_mlstm_n[i].reshape(bs, -1),
        cache_conv[i], lw["g_head"], _shifted_taps(w_dw[i], seq_len), lw["b_dw"], lw["g_cn"],
        lw["b_cn"])
    y_s = _sample_tail_call(xs, hm, cact, ps, tail_ws).reshape(bs, seq_len, d_model)
    m_s = mt[seq_len - 1::seq_len, F_LANE:F_LANE + N_HEADS]
    n_s = n_s.reshape(bs, N_HEADS, DK)

    stack = lambda a: a[None]
    return (y_p, y_s, stack(c_p), stack(n_p), stack(m_p), stack(conv_p),
            stack(c_s), stack(n_s), stack(m_s), stack(conv_s))
```

```python
import functools

import jax
import jax.numpy as jnp
from jax import lax
from jax.experimental import pallas as pl
from jax.experimental.pallas import tpu as pltpu

f32 = jnp.float32
bf16 = jnp.bfloat16

N_HEADS = 4
DK = 128
D_MLSTM = N_HEADS * DK
CONV_WIDTH = 31
CONV_BUF = CONV_WIDTH - 1
EPS = 1e-6
GATE_LANES = 128
F_LANE = N_HEADS
CHUNK = 128
TILE_ROWS = 256
HIST_ROWS = 32
SAMPLE_GROUP = 32
VMEM_LIMIT_BYTES = 56 * 1024 * 1024


def _bdot(a, b):
    return jnp.dot(a, b, preferred_element_type=f32)


def _dot_nt(a, b):
    return lax.dot_general(a, b, (((1,), (1,)), ((), ())), preferred_element_type=f32)


def _dot_tn(a, b):
    return lax.dot_general(a, b, (((0,), (0,)), ((), ())), preferred_element_type=f32)


def _rms(x, g):
    y = x * lax.rsqrt(jnp.mean(x * x, axis=-1, keepdims=True) + EPS)
    return y * g


def _ln(x):
    mu = jnp.mean(x, axis=-1, keepdims=True)
    xc = x - mu
    return xc * lax.rsqrt(jnp.mean(xc * xc, axis=-1, keepdims=True) + EPS)


def _exact_left(sel16, x):
    hi = x.astype(bf16)
    r = x - hi.astype(f32)
    mid = r.astype(bf16)
    lo = (r - mid.astype(f32)).astype(bf16)
    return _bdot(sel16, hi) + _bdot(sel16, mid) + _bdot(sel16, lo)


def _lane_pick(rows, lane_ids):
    out = jnp.zeros((rows[0].shape[0], GATE_LANES), f32)
    for h, r in enumerate(rows):
        out = jnp.where(lane_ids == F_LANE + h, r, out)
    return out


def _in_proj(x, g_mix_ref, w_in_ref, b_in_ref):
    xn = _rms(x, g_mix_ref[...]).astype(bf16)

    def proj(a, b):
        return _bdot(xn, w_in_ref[:, a:b]) + b_in_ref[:, a:b]

    d = D_MLSTM
    q = proj(0, d)
    k = proj(d, 2 * d) * (DK ** -0.5)
    v = proj(2 * d, 3 * d)
    o = proj(3 * d, 4 * d)
    gates = proj(4 * d, 4 * d + GATE_LANES)
    c0 = 4 * d + GATE_LANES
    ga = proj(c0, c0 + d)
    gg = proj(c0 + d, c0 + 2 * d)
    u = ga * jax.nn.sigmoid(gg)
    return q, k, v, o, gates, u


def _gate_cumsum(gates, lcum16):
    lf = jax.nn.log_sigmoid(gates)
    bc = _exact_left(lcum16, lf)
    return lf, bc, gates.T, bc.T


def _chunk_weights(gates, bc, gates_t, bc_t, h, mask, m_prev):
    icol = gates[:, h:h + 1]
    bcol = bc[:, F_LANE + h:F_LANE + h + 1]
    irow = gates_t[h:h + 1, :]
    brow = bc_t[F_LANE + h:F_LANE + h + 1, :]
    logw = jnp.where(mask, (bcol - brow) + irow, -jnp.inf)
    m_intra = jnp.max(logw, axis=-1, keepdims=True)
    log_inter = bcol + m_prev
    m_t = jnp.maximum(log_inter, m_intra)
    w_intra = jnp.exp(logw - m_t)
    w_inter = jnp.exp(log_inter - m_t)
    return icol, bcol, m_t, w_intra, w_inter


def _v_aug(v):
    lane = lax.broadcasted_iota(jnp.int32, v.shape, 1)
    return jnp.concatenate([v, jnp.where(lane == 0, 1.0, 0.0)], axis=1)


def _intra(q16, k16, vaug16, w_intra):
    s = _dot_nt(q16, k16)
    a = (w_intra * s).astype(bf16)
    return _bdot(a, vaug16)


def _finish(nd, w_inter, num_inter, den_inter, m_t):
    num = nd[:, :DK] + w_inter * num_inter
    den = nd[:, DK:DK + 1] + w_inter * den_inter
    return num / jnp.maximum(jnp.abs(den), jnp.exp(-m_t))


def _head_out(h, o, g_head):
    return (_ln(h) * g_head) * jax.nn.sigmoid(o)


def _conv_post(c, g_cn_ref, b_cn_ref):
    c = _ln(c) * g_cn_ref[...] + b_cn_ref[...]
    return c * jax.nn.sigmoid(c)


def _out_proj(x, hm, c, w_out_ref):
    d = D_MLSTM
    return x + (_bdot(hm.astype(bf16), w_out_ref[0:d, :]) + _bdot(c.astype(bf16), w_out_ref[d:, :]))


def _ffn_part(xn16, w_ff1_ref, w_ff2_ref, j0, j1):
    f = jnp.maximum(_bdot(xn16, w_ff1_ref[:, j0:j1]), 0.0)
    return _bdot((f * f).astype(bf16), w_ff2_ref[j0:j1, :])


def _ple_final(x, p, g_ple_ref, w_gate_ref, w_proj_ref, g_final_ref):
    gate = jax.nn.sigmoid(_bdot(_rms(x, g_ple_ref[...]).astype(bf16), w_gate_ref[...]))
    x = x + gate * _bdot(p.astype(bf16), w_proj_ref[...])
    return _rms(x, g_final_ref[...])


def _tail(x, hm, c, p, w_out_ref, g_ffn_ref, w_ff1_ref, w_ff2_ref, g_ple_ref,
          w_gate_ref, w_proj_ref, g_final_ref):
    x = _out_proj(x, hm, c, w_out_ref)
    xn = _rms(x, g_ffn_ref[...]).astype(bf16)
    d_ff = w_ff1_ref.shape[1]
    step = 1024
    acc = None
    for j in range(0, d_ff, step):
        part = _ffn_part(xn, w_ff1_ref, w_ff2_ref, j, j + step)
        acc = part if acc is None else acc + part
    return _ple_final(x + acc, p, g_ple_ref, w_gate_ref, w_proj_ref, g_final_ref)


def _cumsum_rows(x):
    sub = 8
    pos = lax.broadcasted_iota(jnp.int32, x.shape, 0) & (sub - 1)
    sh = 1
    while sh < sub:
        x = x + jnp.where(pos >= sh, pltpu.roll(x, sh, axis=0), 0.0)
        sh *= 2
    tiles = []
    carry = None
    for r0 in range(0, x.shape[0], sub):
        tile = x[r0:r0 + sub]
        if carry is not None:
            tile = tile + carry
        carry = tile[sub - 1:sub]
        tiles.append(tile)
    return jnp.concatenate(tiles, axis=0)


def _prompt_kernel(nt, x_ref, p_ref, g_mix_ref, w_in_ref, b_in_ref, g_head_ref, w_dw_ref,
                   b_dw_ref, g_cn_ref, b_cn_ref, w_out_ref, g_ffn_ref, w_ff1_ref, w_ff2_ref,
                   g_ple_ref, w_gate_ref, w_proj_ref, g_final_ref,
                   y_ref, c_out_ref, n_out_ref, m_out_ref, conv_out_ref,
                   ct_s, m_s, ext_s, hm_s, cact_s, acc_s, xn_s, q_s, k_s, v_s, o_s, g_s, gc_s,
                   hh_s, cc_s, x_s):
    s = pl.program_id(0)
    n_tiles = pl.num_programs(0) - 1
    t = lax.rem(lax.rem(s, n_tiles), nt)
    tm = x_ref.shape[0]
    off = HIST_ROWS - CONV_BUF

    @pl.when(s == 0)
    def _():
        hm_s[...] = jnp.zeros_like(hm_s)
        cact_s[...] = jnp.zeros_like(cact_s)
        x_s[...] = jnp.zeros_like(x_s)

    @pl.when(t == 0)
    def _():
        ct_s[...] = jnp.zeros_like(ct_s)
        m_s[...] = jnp.zeros_like(m_s)
        ext_s[0:HIST_ROWS, :] = jnp.zeros((HIST_ROWS, ext_s.shape[1]), f32)

    x1 = _out_proj(x_s[...], hm_s[...], cact_s[...], w_out_ref)
    acc_s[...] = x1
    xn_s[...] = _rms(x1, g_ffn_ref[...]).astype(bf16)
    x = x_ref[...]
    x_s[...] = x
    q, k, v, o, gates, u = _in_proj(x, g_mix_ref, w_in_ref, b_in_ref)
    q_s[...] = q.astype(bf16)
    k_s[...] = k.astype(bf16)
    v_s[...] = v
    o_s[...] = o
    g_s[...] = gates
    ext_s[HIST_ROWS:HIST_ROWS + tm, :] = u

    rid = lax.broadcasted_iota(jnp.int32, (CHUNK, CHUNK), 0)
    cid = lax.broadcasted_iota(jnp.int32, (CHUNK, CHUNK), 1)
    causal = cid <= rid
    lane_row = lax.broadcasted_iota(jnp.int32, (1, GATE_LANES), 1)

    def ffn_piece(j0, j1):
        acc_s[...] += _ffn_part(xn_s[...], w_ff1_ref, w_ff2_ref, j0, j1)

    def conv_block(r0, rb):
        sub = 8
        acc = None
        for ph in range(sub):
            n_rows = rb if ph == 0 else rb + sub
            grp = None
            for j in range(ph, off + CONV_WIDTH, sub):
                if j < off:
                    continue
                term = w_dw_ref[j - off:j - off + 1, :] * ext_s[r0 + j - ph:r0 + j - ph + n_rows, :]
                grp = term if grp is None else grp + term
            part = grp if ph == 0 else grp[ph:ph + rb]
            acc = part if acc is None else acc + part
        cc_s[r0:r0 + rb, :] = acc + b_dw_ref[...]

    def mlstm_piece(ci, h):
        rows = slice(ci * CHUNK, (ci + 1) * CHUNK)
        cols = slice(h * DK, (h + 1) * DK)
        gates_c = g_s[rows, :]
        if h == 0:
            bc = _cumsum_rows(jax.nn.log_sigmoid(gates_c))
            gc_s[0] = bc
            gc_s[1] = gates_c.T
            gc_s[2] = bc.T
        m_row = m_s[...]
        m_prev = m_row[:, F_LANE + h:F_LANE + h + 1]
        icol, bcol, m_t, w_intra, w_inter = _chunk_weights(
            gates_c, gc_s[0], gc_s[1], gc_s[2], h, causal, m_prev)
        q16 = q_s[rows, cols]
        k16 = k_s[rows, cols]
        vaug = _v_aug(v_s[rows, cols])
        nd = _intra(q16, k16, vaug.astype(bf16), w_intra)
        ct = ct_s[h]
        qc = _bdot(q16, ct.astype(bf16))
        hh_s[rows, cols] = _finish(nd, w_inter, qc[:, :DK], qc[:, DK:DK + 1], m_t)
        m_new = m_t[CHUNK - 1:CHUNK]
        b_last = bcol[CHUNK - 1:CHUNK]
        gcol = jnp.exp(((b_last - bcol) + icol) - m_new)
        decay = jnp.exp(b_last + m_prev - m_new)
        ct_s[h] = decay * ct + _dot_tn(k16, (gcol * vaug).astype(bf16))
        m_s[...] = jnp.where(lane_row == F_LANE + h, m_new, m_row)

    n_piece = (tm // CHUNK) * N_HEADS
    d_ff = w_ff1_ref.shape[1]
    fstep, rb = d_ff // n_piece, tm // n_piece
    for j in range(n_piece):
        ffn_piece(j * fstep, (j + 1) * fstep)
        conv_block(j * rb, rb)
        mlstm_piece(j // N_HEADS, j % N_HEADS)

    y_ref[...] = _ple_final(acc_s[...], p_ref[...], g_ple_ref, w_gate_ref, w_proj_ref,
                            g_final_ref)
    hm = jnp.concatenate(
        [_head_out(hh_s[:, h * DK:(h + 1) * DK], o_s[:, h * DK:(h + 1) * DK],
                   g_head_ref[:, h * DK:(h + 1) * DK]) for h in range(N_HEADS)], axis=1)
    hm_s[...] = hm.astype(bf16)
    cact_s[...] = _conv_post(cc_s[...], g_cn_ref, b_cn_ref).astype(bf16)
    ext_s[0:HIST_ROWS, :] = ext_s[tm:tm + HIST_ROWS, :]

    @pl.when(jnp.logical_and(t == nt - 1, s < n_tiles))
    def _():
        for h in range(N_HEADS):
            ctt = ct_s[h].T
            c_out_ref[h] = ctt[:DK]
            n_out_ref[h:h + 1, :] = ctt[DK:DK + 1]
        m_out_ref[...] = jnp.broadcast_to(m_s[...], m_out_ref.shape)
        conv_out_ref[...] = ext_s[off:HIST_ROWS, :]


def _const_spec(shape):
    return pl.BlockSpec(shape, lambda *_: (0,) * len(shape), pipeline_mode=pl.Buffered(1))


def _weight_specs(ws):
    return [_const_spec(w.shape) for w in ws]


def _prompt_call(x, p, ws):
    B, T, D = x.shape
    tm = TILE_ROWS
    nt = T // tm
    n_tiles = B * nt
    d_conv = ws[4].shape[1]
    front = lambda s: (lax.rem(s, n_tiles) // nt, lax.rem(lax.rem(s, n_tiles), nt), 0)
    back = lambda s: (jnp.maximum(s - 1, 0) // nt, lax.rem(jnp.maximum(s - 1, 0), nt), 0)
    state = lambda s: (jnp.minimum(s, n_tiles - 1) // nt, 0, 0)
    sd = jax.ShapeDtypeStruct
    rows_buf = lambda w, dt: pltpu.VMEM((tm, w), dt)
    return pl.pallas_call(
        functools.partial(_prompt_kernel, nt),
        grid=(n_tiles + 1,),
        in_specs=[pl.BlockSpec((None, tm, D), front), pl.BlockSpec((None, tm, p.shape[2]), back)]
        + _weight_specs(ws),
        out_specs=(
            pl.BlockSpec((None, tm, D), back),
            pl.BlockSpec((None, N_HEADS, DK, DK), lambda s: state(s) + (0,)),
            pl.BlockSpec((None, N_HEADS, DK), state),
            pl.BlockSpec((None, 8, GATE_LANES), state),
            pl.BlockSpec((None, CONV_BUF, d_conv), state),
        ),
        out_shape=(
            sd((B, T, D), f32),
            sd((B, N_HEADS, DK, DK), f32),
            sd((B, N_HEADS, DK), f32),
            sd((B, 8, GATE_LANES), f32),
            sd((B, CONV_BUF, d_conv), f32),
        ),
        scratch_shapes=[
            pltpu.VMEM((N_HEADS, DK, 2 * DK), f32),
            pltpu.VMEM((1, GATE_LANES), f32),
            pltpu.VMEM((HIST_ROWS + tm, d_conv), f32),
            rows_buf(D_MLSTM, bf16), rows_buf(d_conv, bf16),
            rows_buf(D, f32), rows_buf(D, bf16),
            rows_buf(D_MLSTM, bf16), rows_buf(D_MLSTM, bf16),
            rows_buf(D_MLSTM, f32), rows_buf(D_MLSTM, f32),
            rows_buf(GATE_LANES, f32),
            pltpu.VMEM((3, CHUNK, GATE_LANES), f32),
            rows_buf(D_MLSTM, f32), rows_buf(d_conv, f32),
            rows_buf(D, f32),
        ],
        compiler_params=pltpu.CompilerParams(
            dimension_semantics=("arbitrary",), vmem_limit_bytes=VMEM_LIMIT_BYTES),
        name="prompt_layer",
    )(x, p, *ws)


def _sample_in_kernel(x_ref, g_mix_ref, w_in_ref, b_in_ref, qkvo_ref, gates_ref, u_ref):
    q, k, v, o, gates, u = _in_proj(x_ref[...], g_mix_ref, w_in_ref, b_in_ref)
    qkvo_ref[...] = jnp.concatenate([q, k, v, o], axis=1)
    gates_ref[...] = gates
    u_ref[...] = u


def _sample_in_call(x, g_mix, w_in, b_in):
    R, D = x.shape
    tm = TILE_ROWS
    d_conv = (w_in.shape[1] - 4 * D_MLSTM - GATE_LANES) // 2
    sd = jax.ShapeDtypeStruct
    return pl.pallas_call(
        _sample_in_kernel,
        grid=(R // tm,),
        in_specs=[pl.BlockSpec((tm, D), lambda i: (i, 0))] + _weight_specs((g_mix, w_in, b_in)),
        out_specs=(pl.BlockSpec((tm, 4 * D_MLSTM), lambda i: (i, 0)),
                   pl.BlockSpec((tm, GATE_LANES), lambda i: (i, 0)),
                   pl.BlockSpec((tm, d_conv), lambda i: (i, 0))),
        out_shape=(sd((R, 4 * D_MLSTM), f32), sd((R, GATE_LANES), f32), sd((R, d_conv), f32)),
        compiler_params=pltpu.CompilerParams(
            dimension_semantics=("arbitrary",), vmem_limit_bytes=VMEM_LIMIT_BYTES),
        name="sample_in_proj",
    )(x, g_mix, w_in, b_in)


def _sample_rec_kernel(seq_len, qkvo_ref, gates_ref, u_ref, mrep_ref, c_ref, n_ref, hist_ref,
                       g_head_ref, w_dw_ref, b_dw_ref, g_cn_ref, b_cn_ref,
                       hm_ref, cact_ref, c_out_ref, n_out_ref, mt_ref, hist_out_ref,
                       numi_s, dec_s, gk_s):
    rows = qkvo_ref.shape[0]
    nb = rows // seq_len
    shift = seq_len.bit_length() - 1
    d = D_MLSTM

    rid = lax.broadcasted_iota(jnp.int32, (rows, rows), 0)
    cid = lax.broadcasted_iota(jnp.int32, (rows, rows), 1)
    same = (rid >> shift) == (cid >> shift)
    mask = jnp.logical_and(same, cid <= rid)
    lcum16 = jnp.where(mask, 1.0, 0.0).astype(bf16)
    bsum16 = jnp.where(same, 1.0, 0.0).astype(bf16)
    plast16 = jnp.where(cid == ((rid >> shift) << shift) + (seq_len - 1), 1.0, 0.0).astype(bf16)
    lane_g = lax.broadcasted_iota(jnp.int32, (rows, GATE_LANES), 1)

    gates = gates_ref[...]
    lf, bc, gates_t, bc_t = _gate_cumsum(gates, lcum16)
    blast = _exact_left(bsum16, lf)
    mrep = mrep_ref[...]

    q = qkvo_ref[:, 0:d]
    k = qkvo_ref[:, d:2 * d]
    v = qkvo_ref[:, 2 * d:3 * d]
    o = qkvo_ref[:, 3 * d:4 * d]
    q16 = q.astype(bf16)
    k16 = k.astype(bf16)

    per_head = []
    for h in range(N_HEADS):
        cols = slice(h * DK, (h + 1) * DK)
        m_prev = mrep[:, F_LANE + h:F_LANE + h + 1]
        icol, bcol, m_t, w_intra, w_inter = _chunk_weights(gates, bc, gates_t, bc_t, h, mask, m_prev)
        vaug = _v_aug(v[:, cols])
        nd = _intra(q16[:, cols], k16[:, cols], vaug.astype(bf16), w_intra)
        per_head.append((icol, bcol, m_t, w_inter, nd))
    mt_all = _lane_pick([ph[2] for ph in per_head], lane_g)
    mt_ref[...] = mt_all
    mnew = _exact_left(plast16, mt_all)
    dec_s[...] = jnp.exp(blast + mrep - mnew)
    gvt16 = []
    for h in range(N_HEADS):
        cols = slice(h * DK, (h + 1) * DK)
        icol, bcol, _, _, _ = per_head[h]
        lanes = slice(F_LANE + h, F_LANE + h + 1)
        gcol = jnp.exp(((blast[:, lanes] - bcol) + icol) - mnew[:, lanes])
        gvt16.append((gcol * v[:, cols]).T.astype(bf16))
        gk_s[:, cols] = gcol * k[:, cols]

    rsel = lax.broadcasted_iota(jnp.int32, (rows, nb), 0)
    bsel = lax.broadcasted_iota(jnp.int32, (rows, nb), 1)
    expand16 = jnp.where((rsel >> shift) == bsel, 1.0, 0.0).astype(bf16)
    nrep = _exact_left(expand16, n_ref[...])

    grp = lax.broadcasted_iota(jnp.int32, (rows, DK), 0) >> shift
    numi_s[...] = jnp.zeros_like(numi_s)

    def per_seq(b, carry):
        mb = grp == b
        dec_b = dec_s[pl.ds(b * seq_len, 1), :]
        n_b = n_ref[pl.ds(b, 1), :]
        n_new = []
        for h in range(N_HEADS):
            cols = slice(h * DK, (h + 1) * DK)
            cb = c_ref[b, h]
            ni = _dot_nt(q16[:, cols], cb.astype(bf16))
            numi_s[:, cols] = jnp.where(mb, ni, numi_s[:, cols])
            kb = jnp.where(mb, k[:, cols], 0.0).astype(bf16)
            dec = dec_b[:, F_LANE + h:F_LANE + h + 1]
            c_out_ref[b, h] = dec * cb + _bdot(gvt16[h], kb)
            gk = jnp.sum(jnp.where(mb, gk_s[:, cols], 0.0), axis=0, keepdims=True)
            n_new.append(dec * n_b[:, cols] + gk)
        n_out_ref[pl.ds(b, 1), :] = jnp.concatenate(n_new, axis=1)
        return carry

    lax.fori_loop(0, nb, per_seq, 0)

    hm = []
    for h in range(N_HEADS):
        cols = slice(h * DK, (h + 1) * DK)
        _, _, m_t, w_inter, nd = per_head[h]
        den_i = jnp.sum(q[:, cols] * nrep[:, cols], axis=-1, keepdims=True)
        hh = _finish(nd, w_inter, numi_s[:, cols], den_i, m_t)
        hm.append(_head_out(hh, o[:, cols], g_head_ref[:, cols]))
    hm_ref[...] = jnp.concatenate(hm, axis=1)

    rt = lax.broadcasted_iota(jnp.int32, (rows, rows), 0)
    rbm = lax.broadcasted_iota(jnp.int32, (rows, rows), 1)
    nb_shift = nb.bit_length() - 1
    to_tm16 = jnp.where(rbm == ((rt & (nb - 1)) << shift) + (rt >> nb_shift), 1.0, 0.0).astype(bf16)
    to_bm16 = jnp.where(rt == ((rbm & (nb - 1)) << shift) + (rbm >> nb_shift), 1.0, 0.0).astype(bf16)
    u_tm = _exact_left(to_tm16, u_ref[...])
    u_steps = [u_tm[s * nb:(s + 1) * nb] for s in range(seq_len)]
    c_steps = []
    for t in range(seq_len):
        acc = None
        for j in range(t, CONV_BUF):
            term = w_dw_ref[j - t:j - t + 1, :] * hist_ref[j]
            acc = term if acc is None else acc + term
        for s in range(t + 1):
            acc = acc + w_dw_ref[CONV_BUF + s - t:CONV_BUF + s - t + 1, :] * u_steps[s]
        c_steps.append(acc + b_dw_ref[...])
    cact_tm = _conv_post(jnp.concatenate(c_steps, axis=0), g_cn_ref, b_cn_ref)
    cact_ref[...] = _bdot(to_bm16, cact_tm.astype(bf16))
    hist_out_ref[0:CONV_BUF - seq_len] = hist_ref[seq_len:CONV_BUF]
    for s in range(seq_len):
        hist_out_ref[CONV_BUF - seq_len + s] = u_steps[s]


def _sample_rec_call(seq_len, qkvo, gates, u, mrep, c_state, n_state, hist,
                     g_head, w_dw, b_dw, g_cn, b_cn):
    R = qkvo.shape[0]
    nb_total = c_state.shape[0]
    gb = SAMPLE_GROUP
    rows = gb * seq_len
    d_conv = u.shape[1]
    sd = jax.ShapeDtypeStruct
    rspec = lambda w: pl.BlockSpec((rows, w), lambda i: (i, 0))
    return pl.pallas_call(
        functools.partial(_sample_rec_kernel, seq_len),
        grid=(nb_total // gb,),
        in_specs=[rspec(qkvo.shape[1]), rspec(GATE_LANES), rspec(d_conv), rspec(GATE_LANES),
                  pl.BlockSpec((gb, N_HEADS, DK, DK), lambda i: (i, 0, 0, 0)),
                  pl.BlockSpec((gb, D_MLSTM), lambda i: (i, 0)),
                  pl.BlockSpec((CONV_BUF, gb, d_conv), lambda i: (0, i, 0))]
        + _weight_specs((g_head, w_dw, b_dw, g_cn, b_cn)),
        out_specs=(rspec(D_MLSTM), rspec(d_conv),
                   pl.BlockSpec((gb, N_HEADS, DK, DK), lambda i: (i, 0, 0, 0)),
                   pl.BlockSpec((gb, D_MLSTM), lambda i: (i, 0)),
                   rspec(GATE_LANES),
                   pl.BlockSpec((CONV_BUF, gb, d_conv), lambda i: (0, i, 0))),
        out_shape=(sd((R, D_MLSTM), f32), sd((R, d_conv), f32),
                   sd(c_state.shape, f32), sd(n_state.shape, f32),
                   sd((R, GATE_LANES), f32), sd(hist.shape, f32)),
        scratch_shapes=[pltpu.VMEM((rows, D_MLSTM), f32), pltpu.VMEM((rows, GATE_LANES), f32),
                        pltpu.VMEM((rows, D_MLSTM), f32)],
        compiler_params=pltpu.CompilerParams(
            dimension_semantics=("arbitrary",), vmem_limit_bytes=VMEM_LIMIT_BYTES),
        name="sample_recurrent",
    )(qkvo, gates, u, mrep, c_state, n_state, hist, g_head, w_dw, b_dw, g_cn, b_cn)


def _sample_tail_kernel(x_ref, hm_ref, cact_ref, p_ref, w_out_ref, g_ffn_ref, w_ff1_ref, w_ff2_ref,
                        g_ple_ref, w_gate_ref, w_proj_ref, g_final_ref, y_ref):
    y_ref[...] = _tail(x_ref[...], hm_ref[...], cact_ref[...], p_ref[...], w_out_ref, g_ffn_ref,
                       w_ff1_ref, w_ff2_ref, g_ple_ref, w_gate_ref, w_proj_ref, g_final_ref)


def _sample_tail_call(x, hm, cact, p, ws):
    R, D = x.shape
    tm = TILE_ROWS
    rspec = lambda w: pl.BlockSpec((tm, w), lambda i: (i, 0))
    return pl.pallas_call(
        _sample_tail_kernel,
        grid=(R // tm,),
        in_specs=[rspec(D), rspec(hm.shape[1]), rspec(cact.shape[1]), rspec(p.shape[1])]
        + _weight_specs(ws),
        out_specs=rspec(D),
        out_shape=jax.ShapeDtypeStruct((R, D), f32),
        compiler_params=pltpu.CompilerParams(
            dimension_semantics=("arbitrary",), vmem_limit_bytes=VMEM_LIMIT_BYTES),
        name="sample_tail",
    )(x, hm, cact, p, *ws)


def _layer_weights(i, g_mix, w_in, b_in, g_head, w_dw, b_dw, g_cn, b_cn, w_out, g_ffn, w_ff1,
                   w_ff2, g_ple, w_ple_gate, w_ple_proj):
    row = lambda a: a[i].reshape(1, -1).astype(f32)
    d4 = 4 * D_MLSTM
    n_gate = 2 * N_HEADS
    pad = GATE_LANES - n_gate
    w = w_in[i]
    w_in_p = jnp.concatenate(
        [w[:, :d4], jnp.pad(w[:, d4:d4 + n_gate], ((0, 0), (0, pad))), w[:, d4 + n_gate:]],
        axis=1).astype(bf16)
    b = b_in[i]
    b_in_p = jnp.concatenate(
        [b[:d4], jnp.pad(b[d4:d4 + n_gate], (0, pad)), b[d4 + n_gate:]]).reshape(1, -1).astype(f32)
    w_dw_p = jnp.pad(w_dw[i].astype(f32), ((0, HIST_ROWS - CONV_WIDTH), (0, 0)))
    return dict(
        g_mix=row(g_mix), w_in=w_in_p, b_in=b_in_p, g_head=row(g_head), w_dw=w_dw_p,
        b_dw=row(b_dw), g_cn=row(g_cn), b_cn=row(b_cn), w_out=w_out[i].astype(bf16),
        g_ffn=row(g_ffn), w_ff1=w_ff1[i].astype(bf16), w_ff2=w_ff2[i].astype(bf16),
        g_ple=row(g_ple), w_gate=w_ple_gate[i].astype(bf16), w_proj=w_ple_proj[i].astype(bf16))


def kernel(x_prompt, x_sample, state_mlstm_C, state_mlstm_n, state_mlstm_m, cache_conv, p_prompt,
           p_sample, g_mix, w_in, b_in, g_head, w_dw, b_dw, g_cn, b_cn, w_out, g_ffn, w_ff1, w_ff2,
           g_ple, w_ple_gate, w_ple_proj, g_final):
    depth = w_in.shape[0]
    assert depth == 1, "the final norm is fused into the layer kernels"
    bs, seq_len, d_model = x_sample.shape
    assert seq_len & (seq_len - 1) == 0 and seq_len <= CHUNK and SAMPLE_GROUP & (SAMPLE_GROUP - 1) == 0
    g_fin = g_final.reshape(1, -1).astype(f32)

    i = 0
    lw = _layer_weights(i, g_mix, w_in, b_in, g_head, w_dw, b_dw, g_cn, b_cn, w_out, g_ffn,
                        w_ff1, w_ff2, g_ple, w_ple_gate, w_ple_proj)
    tail_ws = (lw["w_out"], lw["g_ffn"], lw["w_ff1"], lw["w_ff2"], lw["g_ple"], lw["w_gate"],
               lw["w_proj"], g_fin)

    prompt_ws = (lw["g_mix"], lw["w_in"], lw["b_in"], lw["g_head"], lw["w_dw"], lw["b_dw"],
                 lw["g_cn"], lw["b_cn"]) + tail_ws
    y_p, c_p, n_p, m_p, conv_p = _prompt_call(x_prompt, p_prompt[i], prompt_ws)
    m_p = m_p[:, 0, F_LANE:F_LANE + N_HEADS]

    xs = x_sample.reshape(bs * seq_len, d_model)
    ps = p_sample[i].reshape(bs * seq_len, -1)
    qkvo, gates, u = _sample_in_call(xs, lw["g_mix"], lw["w_in"], lw["b_in"])
    m0 = jnp.pad(state_mlstm_m[i].astype(f32), ((0, 0), (F_LANE, GATE_LANES - F_LANE - N_HEADS)))
    mrep = jnp.repeat(m0, seq_len, axis=0)
    hm, cact, c_s, n_s, mt, conv_s = _sample_rec_call(
        seq_len, qkvo, gates, u, mrep, state_mlstm_C[i], state_mlstm_n[i].reshape(bs, -1),
        cache_conv[i].transpose(1, 0, 2), lw["g_head"], lw["w_dw"], lw["b_dw"], lw["g_cn"],
        lw["b_cn"])
    conv_s = conv_s.transpose(1, 0, 2)
    y_s = _sample_tail_call(xs, hm, cact, ps, tail_ws).reshape(bs, seq_len, d_model)
    m_s = mt[seq_len - 1::seq_len, F_LANE:F_LANE + N_HEADS]
    n_s = n_s.reshape(bs, N_HEADS, DK)

    stack = lambda a: a[None]
    return (y_p, y_s, stack(c_p), stack(n_p), stack(m_p), stack(conv_p),
            stack(c_s), stack(n_s), stack(m_s), stack(conv_s))
```

```python
import functools

import jax
import jax.numpy as jnp
from jax import lax
from jax.experimental import pallas as pl
from jax.experimental.pallas import tpu as pltpu

f32 = jnp.float32
bf16 = jnp.bfloat16

N_HEADS = 4
DK = 128
D_MLSTM = N_HEADS * DK
CONV_WIDTH = 31
CONV_BUF = CONV_WIDTH - 1
EPS = 1e-6
GATE_LANES = 128
F_LANE = N_HEADS
CHUNK = 128
TILE_ROWS = 256
HIST_ROWS = 32
SAMPLE_GROUP = 32
FFN_PIECES = 4
VMEM_LIMIT_BYTES = 56 * 1024 * 1024


def _bdot(a, b):
    return jnp.dot(a, b, preferred_element_type=f32)


def _dot_nt(a, b):
    return lax.dot_general(a, b, (((1,), (1,)), ((), ())), preferred_element_type=f32)


def _dot_tn(a, b):
    return lax.dot_general(a, b, (((0,), (0,)), ((), ())), preferred_element_type=f32)


def _rms(x, g):
    y = x * lax.rsqrt(jnp.mean(x * x, axis=-1, keepdims=True) + EPS)
    return y * g


def _ln(x):
    mu = jnp.mean(x, axis=-1, keepdims=True)
    xc = x - mu
    return xc * lax.rsqrt(jnp.mean(xc * xc, axis=-1, keepdims=True) + EPS)


def _exact_left(sel16, x):
    hi = x.astype(bf16)
    r = x - hi.astype(f32)
    mid = r.astype(bf16)
    lo = (r - mid.astype(f32)).astype(bf16)
    return _bdot(sel16, hi) + _bdot(sel16, mid) + _bdot(sel16, lo)


def _lane_pick(rows, lane_ids):
    out = jnp.zeros((rows[0].shape[0], GATE_LANES), f32)
    for h, r in enumerate(rows):
        out = jnp.where(lane_ids == F_LANE + h, r, out)
    return out


def _in_proj(x, g_mix_ref, w_in_ref, b_in_ref):
    xn = _rms(x, g_mix_ref[...]).astype(bf16)

    def proj(a, b):
        return _bdot(xn, w_in_ref[:, a:b]) + b_in_ref[:, a:b]

    d = D_MLSTM
    q = proj(0, d)
    k = proj(d, 2 * d) * (DK ** -0.5)
    v = proj(2 * d, 3 * d)
    o = proj(3 * d, 4 * d)
    gates = proj(4 * d, 4 * d + GATE_LANES)
    c0 = 4 * d + GATE_LANES
    ga = proj(c0, c0 + d)
    gg = proj(c0 + d, c0 + 2 * d)
    u = ga * jax.nn.sigmoid(gg)
    return q, k, v, o, gates, u


def _gate_cumsum(gates, lcum16):
    lf = jax.nn.log_sigmoid(gates)
    bc = _exact_left(lcum16, lf)
    return lf, bc, gates.T, bc.T


def _chunk_weights(gates, bc, gates_t, bc_t, h, mask, m_prev):
    icol = gates[:, h:h + 1]
    bcol = bc[:, F_LANE + h:F_LANE + h + 1]
    irow = gates_t[h:h + 1, :]
    brow = bc_t[F_LANE + h:F_LANE + h + 1, :]
    logw = jnp.where(mask, (bcol - brow) + irow, -jnp.inf)
    m_intra = jnp.max(logw, axis=-1, keepdims=True)
    log_inter = bcol + m_prev
    m_t = jnp.maximum(log_inter, m_intra)
    w_intra = jnp.exp(logw - m_t)
    w_inter = jnp.exp(log_inter - m_t)
    return icol, bcol, m_t, w_intra, w_inter


def _v_aug(v):
    lane = lax.broadcasted_iota(jnp.int32, v.shape, 1)
    return jnp.concatenate([v, jnp.where(lane == 0, 1.0, 0.0)], axis=1)


def _intra(q16, k16, vaug16, w_intra):
    s = _dot_nt(q16, k16)
    a = (w_intra * s).astype(bf16)
    return _bdot(a, vaug16)


def _finish(nd, w_inter, num_inter, den_inter, m_t):
    num = nd[:, :DK] + w_inter * num_inter
    den = nd[:, DK:DK + 1] + w_inter * den_inter
    return num / jnp.maximum(jnp.abs(den), jnp.exp(-m_t))


def _head_out(h, o, g_head):
    return (_ln(h) * g_head) * jax.nn.sigmoid(o)


def _conv_post(c, g_cn_ref, b_cn_ref):
    c = _ln(c) * g_cn_ref[...] + b_cn_ref[...]
    return c * jax.nn.sigmoid(c)


def _out_proj(x, hm, c, w_out_ref):
    d = D_MLSTM
    return x + (_bdot(hm.astype(bf16), w_out_ref[0:d, :]) + _bdot(c.astype(bf16), w_out_ref[d:, :]))


def _ffn_part(xn16, w_ff1_ref, w_ff2_ref, j0, j1):
    f = jnp.maximum(_bdot(xn16, w_ff1_ref[:, j0:j1]), 0.0)
    return _bdot((f * f).astype(bf16), w_ff2_ref[j0:j1, :])


def _ple_final(x, p, g_ple_ref, w_gate_ref, w_proj_ref, g_final_ref):
    gate = jax.nn.sigmoid(_bdot(_rms(x, g_ple_ref[...]).astype(bf16), w_gate_ref[...]))
    x = x + gate * _bdot(p.astype(bf16), w_proj_ref[...])
    return _rms(x, g_final_ref[...])


def _tail(x, hm, c, p, w_out_ref, g_ffn_ref, w_ff1_ref, w_ff2_ref, g_ple_ref,
          w_gate_ref, w_proj_ref, g_final_ref):
    x = _out_proj(x, hm, c, w_out_ref)
    xn = _rms(x, g_ffn_ref[...]).astype(bf16)
    d_ff = w_ff1_ref.shape[1]
    step = 1024
    acc = None
    for j in range(0, d_ff, step):
        part = _ffn_part(xn, w_ff1_ref, w_ff2_ref, j, j + step)
        acc = part if acc is None else acc + part
    return _ple_final(x + acc, p, g_ple_ref, w_gate_ref, w_proj_ref, g_final_ref)


def _cumsum_rows(x):
    sub = 8
    pos = lax.broadcasted_iota(jnp.int32, x.shape, 0) & (sub - 1)
    sh = 1
    while sh < sub:
        x = x + jnp.where(pos >= sh, pltpu.roll(x, sh, axis=0), 0.0)
        sh *= 2
    tiles = []
    carry = None
    for r0 in range(0, x.shape[0], sub):
        tile = x[r0:r0 + sub]
        if carry is not None:
            tile = tile + carry
        carry = tile[sub - 1:sub]
        tiles.append(tile)
    return jnp.concatenate(tiles, axis=0)


def _prompt_kernel(nt, xf_ref, xb_ref, p_ref, g_mix_ref, w_in_ref, b_in_ref, g_head_ref, w_dw_ref,
                   b_dw_ref, g_cn_ref, b_cn_ref, w_out_ref, g_ffn_ref, w_ff1_ref, w_ff2_ref,
                   g_ple_ref, w_gate_ref, w_proj_ref, g_final_ref,
                   y_ref, c_out_ref, n_out_ref, m_out_ref, conv_out_ref,
                   ct_s, m_s, ext_s, hm_s, cact_s, acc_s, xn_s, q_s, k_s, v_s, o_s, g_s, gc_s,
                   hh_s, cc_s):
    s = pl.program_id(0)
    n_tiles = pl.num_programs(0) - 1
    t = lax.rem(lax.rem(s, n_tiles), nt)
    tm = xf_ref.shape[0]
    off = HIST_ROWS - CONV_BUF

    @pl.when(s == 0)
    def _():
        hm_s[...] = jnp.zeros_like(hm_s)
        cact_s[...] = jnp.zeros_like(cact_s)

    @pl.when(t == 0)
    def _():
        ct_s[...] = jnp.zeros_like(ct_s)
        m_s[...] = jnp.zeros_like(m_s)
        ext_s[0:HIST_ROWS, :] = jnp.zeros((HIST_ROWS, ext_s.shape[1]), f32)

    x1 = _out_proj(xb_ref[...], hm_s[...], cact_s[...], w_out_ref)
    acc_s[...] = x1
    xn_s[...] = _rms(x1, g_ffn_ref[...]).astype(bf16)
    q, k, v, o, gates, u = _in_proj(xf_ref[...], g_mix_ref, w_in_ref, b_in_ref)
    q_s[...] = q.astype(bf16)
    k_s[...] = k.astype(bf16)
    v_s[...] = v
    o_s[...] = o
    g_s[...] = gates
    ext_s[HIST_ROWS:HIST_ROWS + tm, :] = u

    rid = lax.broadcasted_iota(jnp.int32, (CHUNK, CHUNK), 0)
    cid = lax.broadcasted_iota(jnp.int32, (CHUNK, CHUNK), 1)
    causal = cid <= rid
    lane_row = lax.broadcasted_iota(jnp.int32, (1, GATE_LANES), 1)

    def ffn_piece(j0, j1):
        acc_s[...] += _ffn_part(xn_s[...], w_ff1_ref, w_ff2_ref, j0, j1)

    def conv_block(r0, rb):
        sub = 8
        acc = None
        for ph in range(sub):
            n_rows = rb if ph == 0 else rb + sub
            grp = None
            for j in range(ph, off + CONV_WIDTH, sub):
                if j < off:
                    continue
                term = w_dw_ref[j - off:j - off + 1, :] * ext_s[r0 + j - ph:r0 + j - ph + n_rows, :]
                grp = term if grp is None else grp + term
            part = grp if ph == 0 else grp[ph:ph + rb]
            acc = part if acc is None else acc + part
        cc_s[r0:r0 + rb, :] = acc + b_dw_ref[...]

    def mlstm_piece(ci, h):
        rows = slice(ci * CHUNK, (ci + 1) * CHUNK)
        cols = slice(h * DK, (h + 1) * DK)
        gates_c = g_s[rows, :]
        if h == 0:
            bc = _cumsum_rows(jax.nn.log_sigmoid(gates_c))
            gc_s[0] = bc
            gc_s[1] = gates_c.T
            gc_s[2] = bc.T
        m_row = m_s[...]
        m_prev = m_row[:, F_LANE + h:F_LANE + h + 1]
        icol, bcol, m_t, w_intra, w_inter = _chunk_weights(
            gates_c, gc_s[0], gc_s[1], gc_s[2], h, causal, m_prev)
        q16 = q_s[rows, cols]
        k16 = k_s[rows, cols]
        vaug = _v_aug(v_s[rows, cols])
        nd = _intra(q16, k16, vaug.astype(bf16), w_intra)
        ct = ct_s[h]
        qc = _bdot(q16, ct.astype(bf16))
        hh_s[rows, cols] = _finish(nd, w_inter, qc[:, :DK], qc[:, DK:DK + 1], m_t)
        m_new = m_t[CHUNK - 1:CHUNK]
        b_last = bcol[CHUNK - 1:CHUNK]
        gcol = jnp.exp(((b_last - bcol) + icol) - m_new)
        decay = jnp.exp(b_last + m_prev - m_new)
        ct_s[h] = decay * ct + _dot_tn(k16, (gcol * vaug).astype(bf16))
        m_s[...] = jnp.where(lane_row == F_LANE + h, m_new, m_row)

    n_piece = (tm // CHUNK) * N_HEADS
    d_ff = w_ff1_ref.shape[1]
    fstep, rb = d_ff // FFN_PIECES, tm // n_piece
    for j in range(n_piece):
        if j % (n_piece // FFN_PIECES) == 0:
            jf = j // (n_piece // FFN_PIECES)
            ffn_piece(jf * fstep, (jf + 1) * fstep)
        conv_block(j * rb, rb)
        mlstm_piece(j // N_HEADS, j % N_HEADS)

    y_ref[...] = _ple_final(acc_s[...], p_ref[...], g_ple_ref, w_gate_ref, w_proj_ref,
                            g_final_ref)
    hm = jnp.concatenate(
        [_head_out(hh_s[:, h * DK:(h + 1) * DK], o_s[:, h * DK:(h + 1) * DK],
                   g_head_ref[:, h * DK:(h + 1) * DK]) for h in range(N_HEADS)], axis=1)
    hm_s[...] = hm.astype(bf16)
    cact_s[...] = _conv_post(cc_s[...], g_cn_ref, b_cn_ref).astype(bf16)
    ext_s[0:HIST_ROWS, :] = ext_s[tm:tm + HIST_ROWS, :]

    @pl.when(jnp.logical_and(t == nt - 1, s < n_tiles))
    def _():
        for h in range(N_HEADS):
            ctt = ct_s[h].T
            c_out_ref[h] = ctt[:DK]
            n_out_ref[h:h + 1, :] = ctt[DK:DK + 1]
        m_out_ref[...] = jnp.broadcast_to(m_s[...], m_out_ref.shape)
        conv_out_ref[...] = ext_s[off:HIST_ROWS, :]


def _const_spec(shape):
    return pl.BlockSpec(shape, lambda *_: (0,) * len(shape), pipeline_mode=pl.Buffered(1))


def _weight_specs(ws):
    return [_const_spec(w.shape) for w in ws]


def _prompt_call(x, p, ws):
    B, T, D = x.shape
    tm = TILE_ROWS
    nt = T // tm
    n_tiles = B * nt
    d_conv = ws[4].shape[1]
    front = lambda s: (lax.rem(s, n_tiles) // nt, lax.rem(lax.rem(s, n_tiles), nt), 0)
    back = lambda s: (jnp.maximum(s - 1, 0) // nt, lax.rem(jnp.maximum(s - 1, 0), nt), 0)
    state = lambda s: (jnp.minimum(s, n_tiles - 1) // nt, 0, 0)
    sd = jax.ShapeDtypeStruct
    rows_buf = lambda w, dt: pltpu.VMEM((tm, w), dt)
    return pl.pallas_call(
        functools.partial(_prompt_kernel, nt),
        grid=(n_tiles + 1,),
        in_specs=[pl.BlockSpec((None, tm, D), front), pl.BlockSpec((None, tm, D), back),
                  pl.BlockSpec((None, tm, p.shape[2]), back)] + _weight_specs(ws),
        out_specs=(
            pl.BlockSpec((None, tm, D), back),
            pl.BlockSpec((None, N_HEADS, DK, DK), lambda s: state(s) + (0,)),
            pl.BlockSpec((None, N_HEADS, DK), state),
            pl.BlockSpec((None, 8, GATE_LANES), state),
            pl.BlockSpec((None, CONV_BUF, d_conv), state),
        ),
        out_shape=(
            sd((B, T, D), f32),
            sd((B, N_HEADS, DK, DK), f32),
            sd((B, N_HEADS, DK), f32),
            sd((B, 8, GATE_LANES), f32),
            sd((B, CONV_BUF, d_conv), f32),
        ),
        scratch_shapes=[
            pltpu.VMEM((N_HEADS, DK, 2 * DK), f32),
            pltpu.VMEM((1, GATE_LANES), f32),
            pltpu.VMEM((HIST_ROWS + tm, d_conv), f32),
            rows_buf(D_MLSTM, bf16), rows_buf(d_conv, bf16),
            rows_buf(D, f32), rows_buf(D, bf16),
            rows_buf(D_MLSTM, bf16), rows_buf(D_MLSTM, bf16),
            rows_buf(D_MLSTM, f32), rows_buf(D_MLSTM, f32),
            rows_buf(GATE_LANES, f32),
            pltpu.VMEM((3, CHUNK, GATE_LANES), f32),
            rows_buf(D_MLSTM, f32), rows_buf(d_conv, f32),
        ],
        compiler_params=pltpu.CompilerParams(
            dimension_semantics=("arbitrary",), vmem_limit_bytes=VMEM_LIMIT_BYTES),
        name="prompt_layer",
    )(x, x, p, *ws)


def _sample_in_kernel(x_ref, g_mix_ref, w_in_ref, b_in_ref, qkvo_ref, gates_ref, u_ref):
    q, k, v, o, gates, u = _in_proj(x_ref[...], g_mix_ref, w_in_ref, b_in_ref)
    qkvo_ref[...] = jnp.concatenate([q, k, v, o], axis=1)
    gates_ref[...] = gates
    u_ref[...] = u


def _sample_in_call(x, g_mix, w_in, b_in):
    R, D = x.shape
    tm = TILE_ROWS
    d_conv = (w_in.shape[1] - 4 * D_MLSTM - GATE_LANES) // 2
    sd = jax.ShapeDtypeStruct
    return pl.pallas_call(
        _sample_in_kernel,
        grid=(R // tm,),
        in_specs=[pl.BlockSpec((tm, D), lambda i: (i, 0))] + _weight_specs((g_mix, w_in, b_in)),
        out_specs=(pl.BlockSpec((tm, 4 * D_MLSTM), lambda i: (i, 0)),
                   pl.BlockSpec((tm, GATE_LANES), lambda i: (i, 0)),
                   pl.BlockSpec((tm, d_conv), lambda i: (i, 0))),
        out_shape=(sd((R, 4 * D_MLSTM), f32), sd((R, GATE_LANES), f32), sd((R, d_conv), f32)),
        compiler_params=pltpu.CompilerParams(
            dimension_semantics=("arbitrary",), vmem_limit_bytes=VMEM_LIMIT_BYTES),
        name="sample_in_proj",
    )(x, g_mix, w_in, b_in)


def _sample_rec_kernel(seq_len, qkvo_ref, gates_ref, u_ref, mrep_ref, c_ref, n_ref, hist_ref,
                       g_head_ref, w_dw_ref, b_dw_ref, g_cn_ref, b_cn_ref,
                       hm_ref, cact_ref, c_out_ref, n_out_ref, mt_ref, hist_out_ref,
                       numi_s, dec_s, gk_s):
    rows = qkvo_ref.shape[0]
    nb = rows // seq_len
    shift = seq_len.bit_length() - 1
    d = D_MLSTM

    rid = lax.broadcasted_iota(jnp.int32, (rows, rows), 0)
    cid = lax.broadcasted_iota(jnp.int32, (rows, rows), 1)
    same = (rid >> shift) == (cid >> shift)
    mask = jnp.logical_and(same, cid <= rid)
    lcum16 = jnp.where(mask, 1.0, 0.0).astype(bf16)
    bsum16 = jnp.where(same, 1.0, 0.0).astype(bf16)
    plast16 = jnp.where(cid == ((rid >> shift) << shift) + (seq_len - 1), 1.0, 0.0).astype(bf16)
    lane_g = lax.broadcasted_iota(jnp.int32, (rows, GATE_LANES), 1)

    gates = gates_ref[...]
    lf, bc, gates_t, bc_t = _gate_cumsum(gates, lcum16)
    blast = _exact_left(bsum16, lf)
    mrep = mrep_ref[...]

    q = qkvo_ref[:, 0:d]
    k = qkvo_ref[:, d:2 * d]
    v = qkvo_ref[:, 2 * d:3 * d]
    o = qkvo_ref[:, 3 * d:4 * d]
    q16 = q.astype(bf16)
    k16 = k.astype(bf16)

    per_head = []
    for h in range(N_HEADS):
        cols = slice(h * DK, (h + 1) * DK)
        m_prev = mrep[:, F_LANE + h:F_LANE + h + 1]
        icol, bcol, m_t, w_intra, w_inter = _chunk_weights(gates, bc, gates_t, bc_t, h, mask, m_prev)
        vaug = _v_aug(v[:, cols])
        nd = _intra(q16[:, cols], k16[:, cols], vaug.astype(bf16), w_intra)
        per_head.append((icol, bcol, m_t, w_inter, nd))
    mt_all = _lane_pick([ph[2] for ph in per_head], lane_g)
    mt_ref[...] = mt_all
    mnew = _exact_left(plast16, mt_all)
    dec_s[...] = jnp.exp(blast + mrep - mnew)
    gvt16 = []
    for h in range(N_HEADS):
        cols = slice(h * DK, (h + 1) * DK)
        icol, bcol, _, _, _ = per_head[h]
        lanes = slice(F_LANE + h, F_LANE + h + 1)
        gcol = jnp.exp(((blast[:, lanes] - bcol) + icol) - mnew[:, lanes])
        gvt16.append((gcol * v[:, cols]).T.astype(bf16))
        gk_s[:, cols] = gcol * k[:, cols]

    rsel = lax.broadcasted_iota(jnp.int32, (rows, nb), 0)
    bsel = lax.broadcasted_iota(jnp.int32, (rows, nb), 1)
    expand16 = jnp.where((rsel >> shift) == bsel, 1.0, 0.0).astype(bf16)
    nrep = _exact_left(expand16, n_ref[...])

    grp = lax.broadcasted_iota(jnp.int32, (rows, DK), 0) >> shift
    numi_s[...] = jnp.zeros_like(numi_s)

    def per_seq(b, carry):
        mb = grp == b
        dec_b = dec_s[pl.ds(b * seq_len, 1), :]
        n_b = n_ref[pl.ds(b, 1), :]
        n_new = []
        for h in range(N_HEADS):
            cols = slice(h * DK, (h + 1) * DK)
            cb = c_ref[b, h]
            ni = _dot_nt(q16[:, cols], cb.astype(bf16))
            numi_s[:, cols] = jnp.where(mb, ni, numi_s[:, cols])
            kb = jnp.where(mb, k[:, cols], 0.0).astype(bf16)
            dec = dec_b[:, F_LANE + h:F_LANE + h + 1]
            c_out_ref[b, h] = dec * cb + _bdot(gvt16[h], kb)
            gk = jnp.sum(jnp.where(mb, gk_s[:, cols], 0.0), axis=0, keepdims=True)
            n_new.append(dec * n_b[:, cols] + gk)
        n_out_ref[pl.ds(b, 1), :] = jnp.concatenate(n_new, axis=1)
        return carry

    lax.fori_loop(0, nb, per_seq, 0)

    hm = []
    for h in range(N_HEADS):
        cols = slice(h * DK, (h + 1) * DK)
        _, _, m_t, w_inter, nd = per_head[h]
        den_i = jnp.sum(q[:, cols] * nrep[:, cols], axis=-1, keepdims=True)
        hh = _finish(nd, w_inter, numi_s[:, cols], den_i, m_t)
        hm.append(_head_out(hh, o[:, cols], g_head_ref[:, cols]))
    hm_ref[...] = jnp.concatenate(hm, axis=1)

    rt = lax.broadcasted_iota(jnp.int32, (rows, rows), 0)
    rbm = lax.broadcasted_iota(jnp.int32, (rows, rows), 1)
    nb_shift = nb.bit_length() - 1
    to_tm16 = jnp.where(rbm == ((rt & (nb - 1)) << shift) + (rt >> nb_shift), 1.0, 0.0).astype(bf16)
    to_bm16 = jnp.where(rt == ((rbm & (nb - 1)) << shift) + (rbm >> nb_shift), 1.0, 0.0).astype(bf16)
    u_tm = _exact_left(to_tm16, u_ref[...])
    u_steps = [u_tm[s * nb:(s + 1) * nb] for s in range(seq_len)]
    c_steps = []
    for t in range(seq_len):
        acc = None
        for j in range(t, CONV_BUF):
            term = w_dw_ref[j - t:j - t + 1, :] * hist_ref[j]
            acc = term if acc is None else acc + term
        for s in range(t + 1):
            acc = acc + w_dw_ref[CONV_BUF + s - t:CONV_BUF + s - t + 1, :] * u_steps[s]
        c_steps.append(acc + b_dw_ref[...])
    cact_tm = _conv_post(jnp.concatenate(c_steps, axis=0), g_cn_ref, b_cn_ref)
    cact_ref[...] = _bdot(to_bm16, cact_tm.astype(bf16))
    hist_out_ref[0:CONV_BUF - seq_len] = hist_ref[seq_len:CONV_BUF]
    for s in range(seq_len):
        hist_out_ref[CONV_BUF - seq_len + s] = u_steps[s]


def _sample_rec_call(seq_len, qkvo, gates, u, mrep, c_state, n_state, hist,
                     g_head, w_dw, b_dw, g_cn, b_cn):
    R = qkvo.shape[0]
    nb_total = c_state.shape[0]
    gb = SAMPLE_GROUP
    rows = gb * seq_len
    d_conv = u.shape[1]
    sd = jax.ShapeDtypeStruct
    rspec = lambda w: pl.BlockSpec((rows, w), lambda i: (i, 0))
    return pl.pallas_call(
        functools.partial(_sample_rec_kernel, seq_len),
        grid=(nb_total // gb,),
        in_specs=[rspec(qkvo.shape[1]), rspec(GATE_LANES), rspec(d_conv), rspec(GATE_LANES),
                  pl.BlockSpec((gb, N_HEADS, DK, DK), lambda i: (i, 0, 0, 0)),
                  pl.BlockSpec((gb, D_MLSTM), lambda i: (i, 0)),
                  pl.BlockSpec((CONV_BUF, gb, d_conv), lambda i: (0, i, 0))]
        + _weight_specs((g_head, w_dw, b_dw, g_cn, b_cn)),
        out_specs=(rspec(D_MLSTM), rspec(d_conv),
                   pl.BlockSpec((gb, N_HEADS, DK, DK), lambda i: (i, 0, 0, 0)),
                   pl.BlockSpec((gb, D_MLSTM), lambda i: (i, 0)),
                   rspec(GATE_LANES),
                   pl.BlockSpec((CONV_BUF, gb, d_conv), lambda i: (0, i, 0))),
        out_shape=(sd((R, D_MLSTM), f32), sd((R, d_conv), f32),
                   sd(c_state.shape, f32), sd(n_state.shape, f32),
                   sd((R, GATE_LANES), f32), sd(hist.shape, f32)),
        scratch_shapes=[pltpu.VMEM((rows, D_MLSTM), f32), pltpu.VMEM((rows, GATE_LANES), f32),
                        pltpu.VMEM((rows, D_MLSTM), f32)],
        compiler_params=pltpu.CompilerParams(
            dimension_semantics=("arbitrary",), vmem_limit_bytes=VMEM_LIMIT_BYTES),
        name="sample_recurrent",
    )(qkvo, gates, u, mrep, c_state, n_state, hist, g_head, w_dw, b_dw, g_cn, b_cn)


def _sample_tail_kernel(x_ref, hm_ref, cact_ref, p_ref, w_out_ref, g_ffn_ref, w_ff1_ref, w_ff2_ref,
                        g_ple_ref, w_gate_ref, w_proj_ref, g_final_ref, y_ref):
    y_ref[...] = _tail(x_ref[...], hm_ref[...], cact_ref[...], p_ref[...], w_out_ref, g_ffn_ref,
                       w_ff1_ref, w_ff2_ref, g_ple_ref, w_gate_ref, w_proj_ref, g_final_ref)


def _sample_tail_call(x, hm, cact, p, ws):
    R, D = x.shape
    tm = TILE_ROWS
    rspec = lambda w: pl.BlockSpec((tm, w), lambda i: (i, 0))
    return pl.pallas_call(
        _sample_tail_kernel,
        grid=(R // tm,),
        in_specs=[rspec(D), rspec(hm.shape[1]), rspec(cact.shape[1]), rspec(p.shape[1])]
        + _weight_specs(ws),
        out_specs=rspec(D),
        out_shape=jax.ShapeDtypeStruct((R, D), f32),
        compiler_params=pltpu.CompilerParams(
            dimension_semantics=("arbitrary",), vmem_limit_bytes=VMEM_LIMIT_BYTES),
        name="sample_tail",
    )(x, hm, cact, p, *ws)


def _layer_weights(i, g_mix, w_in, b_in, g_head, w_dw, b_dw, g_cn, b_cn, w_out, g_ffn, w_ff1,
                   w_ff2, g_ple, w_ple_gate, w_ple_proj):
    row = lambda a: a[i].reshape(1, -1).astype(f32)
    d4 = 4 * D_MLSTM
    n_gate = 2 * N_HEADS
    pad = GATE_LANES - n_gate
    w = w_in[i]
    w_in_p = jnp.concatenate(
        [w[:, :d4], jnp.pad(w[:, d4:d4 + n_gate], ((0, 0), (0, pad))), w[:, d4 + n_gate:]],
        axis=1).astype(bf16)
    b = b_in[i]
    b_in_p = jnp.concatenate(
        [b[:d4], jnp.pad(b[d4:d4 + n_gate], (0, pad)), b[d4 + n_gate:]]).reshape(1, -1).astype(f32)
    w_dw_p = jnp.pad(w_dw[i].astype(f32), ((0, HIST_ROWS - CONV_WIDTH), (0, 0)))
    return dict(
        g_mix=row(g_mix), w_in=w_in_p, b_in=b_in_p, g_head=row(g_head), w_dw=w_dw_p,
        b_dw=row(b_dw), g_cn=row(g_cn), b_cn=row(b_cn), w_out=w_out[i].astype(bf16),
        g_ffn=row(g_ffn), w_ff1=w_ff1[i].astype(bf16), w_ff2=w_ff2[i].astype(bf16),
        g_ple=row(g_ple), w_gate=w_ple_gate[i].astype(bf16), w_proj=w_ple_proj[i].astype(bf16))


def kernel(x_prompt, x_sample, state_mlstm_C, state_mlstm_n, state_mlstm_m, cache_conv, p_prompt,
           p_sample, g_mix, w_in, b_in, g_head, w_dw, b_dw, g_cn, b_cn, w_out, g_ffn, w_ff1, w_ff2,
           g_ple, w_ple_gate, w_ple_proj, g_final):
    depth = w_in.shape[0]
    assert depth == 1, "the final norm is fused into the layer kernels"
    bs, seq_len, d_model = x_sample.shape
    assert seq_len & (seq_len - 1) == 0 and seq_len <= CHUNK and SAMPLE_GROUP & (SAMPLE_GROUP - 1) == 0
    g_fin = g_final.reshape(1, -1).astype(f32)

    i = 0
    lw = _layer_weights(i, g_mix, w_in, b_in, g_head, w_dw, b_dw, g_cn, b_cn, w_out, g_ffn,
                        w_ff1, w_ff2, g_ple, w_ple_gate, w_ple_proj)
    tail_ws = (lw["w_out"], lw["g_ffn"], lw["w_ff1"], lw["w_ff2"], lw["g_ple"], lw["w_gate"],
               lw["w_proj"], g_fin)

    prompt_ws = (lw["g_mix"], lw["w_in"], lw["b_in"], lw["g_head"], lw["w_dw"], lw["b_dw"],
                 lw["g_cn"], lw["b_cn"]) + tail_ws
    y_p, c_p, n_p, m_p, conv_p = _prompt_call(x_prompt, p_prompt[i], prompt_ws)
    m_p = m_p[:, 0, F_LANE:F_LANE + N_HEADS]

    xs = x_sample.reshape(bs * seq_len, d_model)
    ps = p_sample[i].reshape(bs * seq_len, -1)
    qkvo, gates, u = _sample_in_call(xs, lw["g_mix"], lw["w_in"], lw["b_in"])
    m0 = jnp.pad(state_mlstm_m[i].astype(f32), ((0, 0), (F_LANE, GATE_LANES - F_LANE - N_HEADS)))
    mrep = jnp.repeat(m0, seq_len, axis=0)
    hm, cact, c_s, n_s, mt, conv_s = _sample_rec_call(
        seq_len, qkvo, gates, u, mrep, state_mlstm_C[i], state_mlstm_n[i].reshape(bs, -1),
        cache_conv[i].transpose(1, 0, 2), lw["g_head"], lw["w_dw"], lw["b_dw"], lw["g_cn"],
        lw["b_cn"])
    conv_s = conv_s.transpose(1, 0, 2)
    y_s = _sample_tail_call(xs, hm, cact, ps, tail_ws).reshape(bs, seq_len, d_model)
    m_s = mt[seq_len - 1::seq_len, F_LANE:F_LANE + N_HEADS]
    n_s = n_s.reshape(bs, N_HEADS, DK)

    stack = lambda a: a[None]
    return (y_p, y_s, stack(c_p), stack(n_p), stack(m_p), stack(conv_p),
            stack(c_s), stack(n_s), stack(m_s), stack(conv_s))
```

```python
import functools

import jax
import jax.numpy as jnp
from jax import lax
from jax.experimental import pallas as pl
from jax.experimental.pallas import tpu as pltpu

f32 = jnp.float32
bf16 = jnp.bfloat16

N_HEADS = 4
DK = 128
D_MLSTM = N_HEADS * DK
CONV_WIDTH = 31
CONV_BUF = CONV_WIDTH - 1
EPS = 1e-6
GATE_LANES = 128
F_LANE = N_HEADS
CHUNK = 128
TILE_ROWS = 256
HIST_ROWS = 32
SAMPLE_GROUP = 32
FFN_PIECES = 4
SEQ_UNROLL = 8
VMEM_LIMIT_BYTES = 56 * 1024 * 1024


def _bdot(a, b):
    return jnp.dot(a, b, preferred_element_type=f32)


def _dot_nt(a, b):
    return lax.dot_general(a, b, (((1,), (1,)), ((), ())), preferred_element_type=f32)


def _dot_tn(a, b):
    return lax.dot_general(a, b, (((0,), (0,)), ((), ())), preferred_element_type=f32)


def _rms(x, g):
    y = x * lax.rsqrt(jnp.mean(x * x, axis=-1, keepdims=True) + EPS)
    return y * g


def _ln(x):
    mu = jnp.mean(x, axis=-1, keepdims=True)
    xc = x - mu
    return xc * lax.rsqrt(jnp.mean(xc * xc, axis=-1, keepdims=True) + EPS)


def _exact_left(sel16, x):
    hi = x.astype(bf16)
    r = x - hi.astype(f32)
    mid = r.astype(bf16)
    lo = (r - mid.astype(f32)).astype(bf16)
    return _bdot(sel16, hi) + _bdot(sel16, mid) + _bdot(sel16, lo)


def _lane_pick(rows, lane_ids):
    out = jnp.zeros((rows[0].shape[0], GATE_LANES), f32)
    for h, r in enumerate(rows):
        out = jnp.where(lane_ids == F_LANE + h, r, out)
    return out


def _in_proj(x, g_mix_ref, w_in_ref, b_in_ref):
    xn = _rms(x, g_mix_ref[...]).astype(bf16)

    def proj(a, b):
        return _bdot(xn, w_in_ref[:, a:b]) + b_in_ref[:, a:b]

    d = D_MLSTM
    q = proj(0, d)
    k = proj(d, 2 * d) * (DK ** -0.5)
    v = proj(2 * d, 3 * d)
    o = proj(3 * d, 4 * d)
    gates = proj(4 * d, 4 * d + GATE_LANES)
    c0 = 4 * d + GATE_LANES
    ga = proj(c0, c0 + d)
    gg = proj(c0 + d, c0 + 2 * d)
    u = ga * jax.nn.sigmoid(gg)
    return q, k, v, o, gates, u


def _gate_cumsum(gates, lcum16):
    lf = jax.nn.log_sigmoid(gates)
    bc = _exact_left(lcum16, lf)
    return lf, bc, gates.T, bc.T


def _chunk_weights(gates, bc, gates_t, bc_t, h, mask, m_prev):
    icol = gates[:, h:h + 1]
    bcol = bc[:, F_LANE + h:F_LANE + h + 1]
    irow = gates_t[h:h + 1, :]
    brow = bc_t[F_LANE + h:F_LANE + h + 1, :]
    logw = jnp.where(mask, (bcol - brow) + irow, -jnp.inf)
    m_intra = jnp.max(logw, axis=-1, keepdims=True)
    log_inter = bcol + m_prev
    m_t = jnp.maximum(log_inter, m_intra)
    w_intra = jnp.exp(logw - m_t)
    w_inter = jnp.exp(log_inter - m_t)
    return icol, bcol, m_t, w_intra, w_inter


def _v_aug(v):
    lane = lax.broadcasted_iota(jnp.int32, v.shape, 1)
    return jnp.concatenate([v, jnp.where(lane == 0, 1.0, 0.0)], axis=1)


def _intra(q16, k16, vaug16, w_intra):
    s = _dot_nt(q16, k16)
    a = (w_intra * s).astype(bf16)
    return _bdot(a, vaug16)


def _finish(nd, w_inter, num_inter, den_inter, m_t):
    num = nd[:, :DK] + w_inter * num_inter
    den = nd[:, DK:DK + 1] + w_inter * den_inter
    return num / jnp.maximum(jnp.abs(den), jnp.exp(-m_t))


def _head_out(h, o, g_head):
    return (_ln(h) * g_head) * jax.nn.sigmoid(o)


def _conv_post(c, g_cn_ref, b_cn_ref):
    c = _ln(c) * g_cn_ref[...] + b_cn_ref[...]
    return c * jax.nn.sigmoid(c)


def _out_proj(x, hm, c, w_out_ref):
    d = D_MLSTM
    return x + (_bdot(hm.astype(bf16), w_out_ref[0:d, :]) + _bdot(c.astype(bf16), w_out_ref[d:, :]))


def _ffn_part(xn16, w_ff1_ref, w_ff2_ref, j0, j1):
    f = jnp.maximum(_bdot(xn16, w_ff1_ref[:, j0:j1]), 0.0)
    return _bdot((f * f).astype(bf16), w_ff2_ref[j0:j1, :])


def _ple_final(x, p, g_ple_ref, w_gate_ref, w_proj_ref, g_final_ref):
    gate = jax.nn.sigmoid(_bdot(_rms(x, g_ple_ref[...]).astype(bf16), w_gate_ref[...]))
    x = x + gate * _bdot(p.astype(bf16), w_proj_ref[...])
    return _rms(x, g_final_ref[...])


def _tail(x, hm, c, p, w_out_ref, g_ffn_ref, w_ff1_ref, w_ff2_ref, g_ple_ref,
          w_gate_ref, w_proj_ref, g_final_ref):
    x = _out_proj(x, hm, c, w_out_ref)
    xn = _rms(x, g_ffn_ref[...]).astype(bf16)
    d_ff = w_ff1_ref.shape[1]
    step = 1024
    acc = None
    for j in range(0, d_ff, step):
        part = _ffn_part(xn, w_ff1_ref, w_ff2_ref, j, j + step)
        acc = part if acc is None else acc + part
    return _ple_final(x + acc, p, g_ple_ref, w_gate_ref, w_proj_ref, g_final_ref)


def _cumsum_rows(x):
    sub = 8
    pos = lax.broadcasted_iota(jnp.int32, x.shape, 0) & (sub - 1)
    sh = 1
    while sh < sub:
        x = x + jnp.where(pos >= sh, pltpu.roll(x, sh, axis=0), 0.0)
        sh *= 2
    tiles = []
    carry = None
    for r0 in range(0, x.shape[0], sub):
        tile = x[r0:r0 + sub]
        if carry is not None:
            tile = tile + carry
        carry = tile[sub - 1:sub]
        tiles.append(tile)
    return jnp.concatenate(tiles, axis=0)


def _prompt_kernel(nt, xf_ref, xb_ref, p_ref, g_mix_ref, w_in_ref, b_in_ref, g_head_ref, w_dw_ref,
                   b_dw_ref, g_cn_ref, b_cn_ref, w_out_ref, g_ffn_ref, w_ff1_ref, w_ff2_ref,
                   g_ple_ref, w_gate_ref, w_proj_ref, g_final_ref,
                   y_ref, c_out_ref, n_out_ref, m_out_ref, conv_out_ref,
                   ct_s, m_s, ext_s, hm_s, cact_s, acc_s, xn_s, q_s, k_s, v_s, o_s, g_s, gc_s,
                   hh_s, cc_s):
    s = pl.program_id(0)
    n_tiles = pl.num_programs(0) - 1
    t = lax.rem(lax.rem(s, n_tiles), nt)
    tm = xf_ref.shape[0]
    off = HIST_ROWS - CONV_BUF

    @pl.when(s == 0)
    def _():
        hm_s[...] = jnp.zeros_like(hm_s)
        cact_s[...] = jnp.zeros_like(cact_s)

    @pl.when(t == 0)
    def _():
        ct_s[...] = jnp.zeros_like(ct_s)
        m_s[...] = jnp.zeros_like(m_s)
        ext_s[0:HIST_ROWS, :] = jnp.zeros((HIST_ROWS, ext_s.shape[1]), f32)

    x1 = _out_proj(xb_ref[...], hm_s[...], cact_s[...], w_out_ref)
    acc_s[...] = x1
    xn_s[...] = _rms(x1, g_ffn_ref[...]).astype(bf16)
    q, k, v, o, gates, u = _in_proj(xf_ref[...], g_mix_ref, w_in_ref, b_in_ref)
    q_s[...] = q.astype(bf16)
    k_s[...] = k.astype(bf16)
    v_s[...] = v
    o_s[...] = o
    g_s[...] = gates
    ext_s[HIST_ROWS:HIST_ROWS + tm, :] = u

    rid = lax.broadcasted_iota(jnp.int32, (CHUNK, CHUNK), 0)
    cid = lax.broadcasted_iota(jnp.int32, (CHUNK, CHUNK), 1)
    causal = cid <= rid
    lane_row = lax.broadcasted_iota(jnp.int32, (1, GATE_LANES), 1)

    def ffn_piece(j0, j1):
        acc_s[...] += _ffn_part(xn_s[...], w_ff1_ref, w_ff2_ref, j0, j1)

    def conv_block(r0, rb):
        sub = 8
        acc = None
        for ph in range(sub):
            n_rows = rb if ph == 0 else rb + sub
            grp = None
            for j in range(ph, off + CONV_WIDTH, sub):
                if j < off:
                    continue
                term = w_dw_ref[j - off:j - off + 1, :] * ext_s[r0 + j - ph:r0 + j - ph + n_rows, :]
                grp = term if grp is None else grp + term
            part = grp if ph == 0 else grp[ph:ph + rb]
            acc = part if acc is None else acc + part
        cc_s[r0:r0 + rb, :] = acc + b_dw_ref[...]

    def mlstm_piece(ci, h):
        rows = slice(ci * CHUNK, (ci + 1) * CHUNK)
        cols = slice(h * DK, (h + 1) * DK)
        gates_c = g_s[rows, :]
        if h == 0:
            bc = _cumsum_rows(jax.nn.log_sigmoid(gates_c))
            gc_s[0] = bc
            gc_s[1] = gates_c.T
            gc_s[2] = bc.T
        m_row = m_s[...]
        m_prev = m_row[:, F_LANE + h:F_LANE + h + 1]
        icol, bcol, m_t, w_intra, w_inter = _chunk_weights(
            gates_c, gc_s[0], gc_s[1], gc_s[2], h, causal, m_prev)
        q16 = q_s[rows, cols]
        k16 = k_s[rows, cols]
        vaug = _v_aug(v_s[rows, cols])
        nd = _intra(q16, k16, vaug.astype(bf16), w_intra)
        ct = ct_s[h]
        qc = _bdot(q16, ct.astype(bf16))
        hh_s[rows, cols] = _finish(nd, w_inter, qc[:, :DK], qc[:, DK:DK + 1], m_t)
        m_new = m_t[CHUNK - 1:CHUNK]
        b_last = bcol[CHUNK - 1:CHUNK]
        gcol = jnp.exp(((b_last - bcol) + icol) - m_new)
        decay = jnp.exp(b_last + m_prev - m_new)
        ct_s[h] = decay * ct + _dot_tn(k16, (gcol * vaug).astype(bf16))
        m_s[...] = jnp.where(lane_row == F_LANE + h, m_new, m_row)

    n_piece = (tm // CHUNK) * N_HEADS
    d_ff = w_ff1_ref.shape[1]
    fstep, rb = d_ff // FFN_PIECES, tm // n_piece
    for j in range(n_piece):
        if j % (n_piece // FFN_PIECES) == 0:
            jf = j // (n_piece // FFN_PIECES)
            ffn_piece(jf * fstep, (jf + 1) * fstep)
        conv_block(j * rb, rb)
        mlstm_piece(j // N_HEADS, j % N_HEADS)

    y_ref[...] = _ple_final(acc_s[...], p_ref[...], g_ple_ref, w_gate_ref, w_proj_ref,
                            g_final_ref)
    hm = jnp.concatenate(
        [_head_out(hh_s[:, h * DK:(h + 1) * DK], o_s[:, h * DK:(h + 1) * DK],
                   g_head_ref[:, h * DK:(h + 1) * DK]) for h in range(N_HEADS)], axis=1)
    hm_s[...] = hm.astype(bf16)
    cact_s[...] = _conv_post(cc_s[...], g_cn_ref, b_cn_ref).astype(bf16)
    ext_s[0:HIST_ROWS, :] = ext_s[tm:tm + HIST_ROWS, :]

    @pl.when(jnp.logical_and(t == nt - 1, s < n_tiles))
    def _():
        for h in range(N_HEADS):
            ctt = ct_s[h].T
            c_out_ref[h] = ctt[:DK]
            n_out_ref[h:h + 1, :] = ctt[DK:DK + 1]
        m_out_ref[...] = jnp.broadcast_to(m_s[...], m_out_ref.shape)
        conv_out_ref[...] = ext_s[off:HIST_ROWS, :]


def _const_spec(shape):
    return pl.BlockSpec(shape, lambda *_: (0,) * len(shape), pipeline_mode=pl.Buffered(1))


def _weight_specs(ws):
    return [_const_spec(w.shape) for w in ws]


def _prompt_call(x, p, ws):
    B, T, D = x.shape
    tm = TILE_ROWS
    nt = T // tm
    n_tiles = B * nt
    d_conv = ws[4].shape[1]
    front = lambda s: (lax.rem(s, n_tiles) // nt, lax.rem(lax.rem(s, n_tiles), nt), 0)
    back = lambda s: (jnp.maximum(s - 1, 0) // nt, lax.rem(jnp.maximum(s - 1, 0), nt), 0)
    state = lambda s: (jnp.minimum(s, n_tiles - 1) // nt, 0, 0)
    sd = jax.ShapeDtypeStruct
    rows_buf = lambda w, dt: pltpu.VMEM((tm, w), dt)
    return pl.pallas_call(
        functools.partial(_prompt_kernel, nt),
        grid=(n_tiles + 1,),
        in_specs=[pl.BlockSpec((None, tm, D), front), pl.BlockSpec((None, tm, D), back),
                  pl.BlockSpec((None, tm, p.shape[2]), back)] + _weight_specs(ws),
        out_specs=(
            pl.BlockSpec((None, tm, D), back),
            pl.BlockSpec((None, N_HEADS, DK, DK), lambda s: state(s) + (0,)),
            pl.BlockSpec((None, N_HEADS, DK), state),
            pl.BlockSpec((None, 8, GATE_LANES), state),
            pl.BlockSpec((None, CONV_BUF, d_conv), state),
        ),
        out_shape=(
            sd((B, T, D), f32),
            sd((B, N_HEADS, DK, DK), f32),
            sd((B, N_HEADS, DK), f32),
            sd((B, 8, GATE_LANES), f32),
            sd((B, CONV_BUF, d_conv), f32),
        ),
        scratch_shapes=[
            pltpu.VMEM((N_HEADS, DK, 2 * DK), f32),
            pltpu.VMEM((1, GATE_LANES), f32),
            pltpu.VMEM((HIST_ROWS + tm, d_conv), f32),
            rows_buf(D_MLSTM, bf16), rows_buf(d_conv, bf16),
            rows_buf(D, f32), rows_buf(D, bf16),
            rows_buf(D_MLSTM, bf16), rows_buf(D_MLSTM, bf16),
            rows_buf(D_MLSTM, f32), rows_buf(D_MLSTM, f32),
            rows_buf(GATE_LANES, f32),
            pltpu.VMEM((3, CHUNK, GATE_LANES), f32),
            rows_buf(D_MLSTM, f32), rows_buf(d_conv, f32),
        ],
        compiler_params=pltpu.CompilerParams(
            dimension_semantics=("arbitrary",), vmem_limit_bytes=VMEM_LIMIT_BYTES),
        name="prompt_layer",
    )(x, x, p, *ws)


def _sample_in_kernel(x_ref, g_mix_ref, w_in_ref, b_in_ref, qkvo_ref, gates_ref, u_ref):
    q, k, v, o, gates, u = _in_proj(x_ref[...], g_mix_ref, w_in_ref, b_in_ref)
    qkvo_ref[...] = jnp.concatenate([q, k, v, o], axis=1)
    gates_ref[...] = gates
    u_ref[...] = u


def _sample_in_call(x, g_mix, w_in, b_in):
    R, D = x.shape
    tm = TILE_ROWS
    d_conv = (w_in.shape[1] - 4 * D_MLSTM - GATE_LANES) // 2
    sd = jax.ShapeDtypeStruct
    return pl.pallas_call(
        _sample_in_kernel,
        grid=(R // tm,),
        in_specs=[pl.BlockSpec((tm, D), lambda i: (i, 0))] + _weight_specs((g_mix, w_in, b_in)),
        out_specs=(pl.BlockSpec((tm, 4 * D_MLSTM), lambda i: (i, 0)),
                   pl.BlockSpec((tm, GATE_LANES), lambda i: (i, 0)),
                   pl.BlockSpec((tm, d_conv), lambda i: (i, 0))),
        out_shape=(sd((R, 4 * D_MLSTM), f32), sd((R, GATE_LANES), f32), sd((R, d_conv), f32)),
        compiler_params=pltpu.CompilerParams(
            dimension_semantics=("arbitrary",), vmem_limit_bytes=VMEM_LIMIT_BYTES),
        name="sample_in_proj",
    )(x, g_mix, w_in, b_in)


def _sample_rec_kernel(seq_len, qkvo_ref, gates_ref, u_ref, mrep_ref, c_ref, n_ref, hist_ref,
                       g_head_ref, w_dw_ref, b_dw_ref, g_cn_ref, b_cn_ref,
                       hm_ref, cact_ref, c_out_ref, n_out_ref, mt_ref, hist_out_ref,
                       numi_s, dec_s, gk_s):
    rows = qkvo_ref.shape[0]
    nb = rows // seq_len
    shift = seq_len.bit_length() - 1
    d = D_MLSTM

    rid = lax.broadcasted_iota(jnp.int32, (rows, rows), 0)
    cid = lax.broadcasted_iota(jnp.int32, (rows, rows), 1)
    same = (rid >> shift) == (cid >> shift)
    mask = jnp.logical_and(same, cid <= rid)
    lcum16 = jnp.where(mask, 1.0, 0.0).astype(bf16)
    bsum16 = jnp.where(same, 1.0, 0.0).astype(bf16)
    plast16 = jnp.where(cid == ((rid >> shift) << shift) + (seq_len - 1), 1.0, 0.0).astype(bf16)
    lane_g = lax.broadcasted_iota(jnp.int32, (rows, GATE_LANES), 1)

    gates = gates_ref[...]
    lf, bc, gates_t, bc_t = _gate_cumsum(gates, lcum16)
    blast = _exact_left(bsum16, lf)
    mrep = mrep_ref[...]

    q = qkvo_ref[:, 0:d]
    k = qkvo_ref[:, d:2 * d]
    v = qkvo_ref[:, 2 * d:3 * d]
    o = qkvo_ref[:, 3 * d:4 * d]
    q16 = q.astype(bf16)
    k16 = k.astype(bf16)

    per_head = []
    for h in range(N_HEADS):
        cols = slice(h * DK, (h + 1) * DK)
        m_prev = mrep[:, F_LANE + h:F_LANE + h + 1]
        icol, bcol, m_t, w_intra, w_inter = _chunk_weights(gates, bc, gates_t, bc_t, h, mask, m_prev)
        vaug = _v_aug(v[:, cols])
        nd = _intra(q16[:, cols], k16[:, cols], vaug.astype(bf16), w_intra)
        per_head.append((icol, bcol, m_t, w_inter, nd))
    mt_all = _lane_pick([ph[2] for ph in per_head], lane_g)
    mt_ref[...] = mt_all
    mnew = _exact_left(plast16, mt_all)
    dec_s[...] = jnp.exp(blast + mrep - mnew)
    gvt16 = []
    for h in range(N_HEADS):
        cols = slice(h * DK, (h + 1) * DK)
        icol, bcol, _, _, _ = per_head[h]
        lanes = slice(F_LANE + h, F_LANE + h + 1)
        gcol = jnp.exp(((blast[:, lanes] - bcol) + icol) - mnew[:, lanes])
        gvt16.append((gcol * v[:, cols]).T.astype(bf16))
        gk_s[:, cols] = gcol * k[:, cols]

    rsel = lax.broadcasted_iota(jnp.int32, (rows, nb), 0)
    bsel = lax.broadcasted_iota(jnp.int32, (rows, nb), 1)
    expand16 = jnp.where((rsel >> shift) == bsel, 1.0, 0.0).astype(bf16)
    nrep = _exact_left(expand16, n_ref[...])

    grp = lax.broadcasted_iota(jnp.int32, (rows, DK), 0) >> shift
    numi_s[...] = jnp.zeros_like(numi_s)

    sub = 8
    tile_shift = (sub // seq_len).bit_length() - 1
    in_tile = lax.broadcasted_iota(jnp.int32, (sub, d), 0) >> shift

    def per_seq(b, carry):
        mb = grp == b
        r0 = pl.multiple_of((b >> tile_shift) * sub, sub)
        mine = in_tile == (b & (sub // seq_len - 1))
        q8 = qkvo_ref[pl.ds(r0, sub), 0:d]
        dec_b = dec_s[pl.ds(b * seq_len, 1), :]
        n_b = n_ref[pl.ds(b, 1), :]
        gk8 = jnp.sum(jnp.where(mine, gk_s[pl.ds(r0, sub), :], 0.0), axis=0, keepdims=True)
        n_new = []
        num_i = []
        for h in range(N_HEADS):
            cols = slice(h * DK, (h + 1) * DK)
            cb = c_ref[b, h]
            num_i.append(_dot_nt(q8[:, cols].astype(bf16), cb.astype(bf16)))
            kb = jnp.where(mb, k[:, cols], 0.0).astype(bf16)
            dec = dec_b[:, F_LANE + h:F_LANE + h + 1]
            c_out_ref[b, h] = dec * cb + _bdot(gvt16[h], kb)
            n_new.append(dec * n_b[:, cols] + gk8[:, cols])
        numi_s[pl.ds(r0, sub), :] = jnp.where(mine, jnp.concatenate(num_i, axis=1),
                                              numi_s[pl.ds(r0, sub), :])
        n_out_ref[pl.ds(b, 1), :] = jnp.concatenate(n_new, axis=1)
        return carry

    lax.fori_loop(0, nb, per_seq, 0, unroll=SEQ_UNROLL)

    hm = []
    for h in range(N_HEADS):
        cols = slice(h * DK, (h + 1) * DK)
        _, _, m_t, w_inter, nd = per_head[h]
        den_i = jnp.sum(q[:, cols] * nrep[:, cols], axis=-1, keepdims=True)
        hh = _finish(nd, w_inter, numi_s[:, cols], den_i, m_t)
        hm.append(_head_out(hh, o[:, cols], g_head_ref[:, cols]))
    hm_ref[...] = jnp.concatenate(hm, axis=1)

    rt = lax.broadcasted_iota(jnp.int32, (rows, rows), 0)
    rbm = lax.broadcasted_iota(jnp.int32, (rows, rows), 1)
    nb_shift = nb.bit_length() - 1
    to_tm16 = jnp.where(rbm == ((rt & (nb - 1)) << shift) + (rt >> nb_shift), 1.0, 0.0).astype(bf16)
    to_bm16 = jnp.where(rt == ((rbm & (nb - 1)) << shift) + (rbm >> nb_shift), 1.0, 0.0).astype(bf16)
    u_tm = _exact_left(to_tm16, u_ref[...])
    u_steps = [u_tm[s * nb:(s + 1) * nb] for s in range(seq_len)]
    c_steps = []
    for t in range(seq_len):
        acc = None
        for j in range(t, CONV_BUF):
            term = w_dw_ref[j - t:j - t + 1, :] * hist_ref[j]
            acc = term if acc is None else acc + term
        for s in range(t + 1):
            acc = acc + w_dw_ref[CONV_BUF + s - t:CONV_BUF + s - t + 1, :] * u_steps[s]
        c_steps.append(acc + b_dw_ref[...])
    cact_tm = _conv_post(jnp.concatenate(c_steps, axis=0), g_cn_ref, b_cn_ref)
    cact_ref[...] = _bdot(to_bm16, cact_tm.astype(bf16))
    hist_out_ref[0:CONV_BUF - seq_len] = hist_ref[seq_len:CONV_BUF]
    for s in range(seq_len):
        hist_out_ref[CONV_BUF - seq_len + s] = u_steps[s]


def _sample_rec_call(seq_len, qkvo, gates, u, mrep, c_state, n_state, hist,
                     g_head, w_dw, b_dw, g_cn, b_cn):
    R = qkvo.shape[0]
    nb_total = c_state.shape[0]
    gb = SAMPLE_GROUP
    rows = gb * seq_len
    d_conv = u.shape[1]
    sd = jax.ShapeDtypeStruct
    rspec = lambda w: pl.BlockSpec((rows, w), lambda i: (i, 0))
    return pl.pallas_call(
        functools.partial(_sample_rec_kernel, seq_len),
        grid=(nb_total // gb,),
        in_specs=[rspec(qkvo.shape[1]), rspec(GATE_LANES), rspec(d_conv), rspec(GATE_LANES),
                  pl.BlockSpec((gb, N_HEADS, DK, DK), lambda i: (i, 0, 0, 0)),
                  pl.BlockSpec((gb, D_MLSTM), lambda i: (i, 0)),
                  pl.BlockSpec((CONV_BUF, gb, d_conv), lambda i: (0, i, 0))]
        + _weight_specs((g_head, w_dw, b_dw, g_cn, b_cn)),
        out_specs=(rspec(D_MLSTM), rspec(d_conv),
                   pl.BlockSpec((gb, N_HEADS, DK, DK), lambda i: (i, 0, 0, 0)),
                   pl.BlockSpec((gb, D_MLSTM), lambda i: (i, 0)),
                   rspec(GATE_LANES),
                   pl.BlockSpec((CONV_BUF, gb, d_conv), lambda i: (0, i, 0))),
        out_shape=(sd((R, D_MLSTM), f32), sd((R, d_conv), f32),
                   sd(c_state.shape, f32), sd(n_state.shape, f32),
                   sd((R, GATE_LANES), f32), sd(hist.shape, f32)),
        scratch_shapes=[pltpu.VMEM((rows, D_MLSTM), f32), pltpu.VMEM((rows, GATE_LANES), f32),
                        pltpu.VMEM((rows, D_MLSTM), f32)],
        compiler_params=pltpu.CompilerParams(
            dimension_semantics=("arbitrary",), vmem_limit_bytes=VMEM_LIMIT_BYTES),
        name="sample_recurrent",
    )(qkvo, gates, u, mrep, c_state, n_state, hist, g_head, w_dw, b_dw, g_cn, b_cn)


def _sample_tail_kernel(x_ref, hm_ref, cact_ref, p_ref, w_out_ref, g_ffn_ref, w_ff1_ref, w_ff2_ref,
                        g_ple_ref, w_gate_ref, w_proj_ref, g_final_ref, y_ref):
    y_ref[...] = _tail(x_ref[...], hm_ref[...], cact_ref[...], p_ref[...], w_out_ref, g_ffn_ref,
                       w_ff1_ref, w_ff2_ref, g_ple_ref, w_gate_ref, w_proj_ref, g_final_ref)


def _sample_tail_call(x, hm, cact, p, ws):
    R, D = x.shape
    tm = TILE_ROWS
    rspec = lambda w: pl.BlockSpec((tm, w), lambda i: (i, 0))
    return pl.pallas_call(
        _sample_tail_kernel,
        grid=(R // tm,),
        in_specs=[rspec(D), rspec(hm.shape[1]), rspec(cact.shape[1]), rspec(p.shape[1])]
        + _weight_specs(ws),
        out_specs=rspec(D),
        out_shape=jax.ShapeDtypeStruct((R, D), f32),
        compiler_params=pltpu.CompilerParams(
            dimension_semantics=("arbitrary",), vmem_limit_bytes=VMEM_LIMIT_BYTES),
        name="sample_tail",
    )(x, hm, cact, p, *ws)


def _w_in_layout_kernel(d4, n_gate, w_ref, o_ref):
    x = w_ref[...]
    rest = x.shape[1] - d4 - n_gate
    o_ref[:, 0:d4] = x[:, 0:d4].astype(bf16)
    lane = lax.broadcasted_iota(jnp.int32, (x.shape[0], GATE_LANES), 1)
    o_ref[:, d4:d4 + GATE_LANES] = jnp.where(lane < n_gate, x[:, d4:d4 + GATE_LANES], 0.0).astype(bf16)
    o_ref[:, d4 + GATE_LANES:d4 + GATE_LANES + rest] = x[:, d4 + n_gate:d4 + n_gate + rest].astype(bf16)


def _w_in_layout_call(w, d4, n_gate):
    rows, width = w.shape
    out_w = width - n_gate + GATE_LANES
    rb = TILE_ROWS
    return pl.pallas_call(
        functools.partial(_w_in_layout_kernel, d4, n_gate),
        grid=(rows // rb,),
        in_specs=[pl.BlockSpec((rb, width), lambda i: (i, 0))],
        out_specs=pl.BlockSpec((rb, out_w), lambda i: (i, 0)),
        out_shape=jax.ShapeDtypeStruct((rows, out_w), bf16),
        compiler_params=pltpu.CompilerParams(
            dimension_semantics=("arbitrary",), vmem_limit_bytes=VMEM_LIMIT_BYTES),
        name="w_in_layout",
    )(w)


def _layer_weights(i, g_mix, w_in, b_in, g_head, w_dw, b_dw, g_cn, b_cn, w_out, g_ffn, w_ff1,
                   w_ff2, g_ple, w_ple_gate, w_ple_proj):
    row = lambda a: a[i].reshape(1, -1).astype(f32)
    d4 = 4 * D_MLSTM
    n_gate = 2 * N_HEADS
    pad = GATE_LANES - n_gate
    w_in_p = _w_in_layout_call(w_in[i], d4, n_gate)
    b = b_in[i]
    b_in_p = jnp.concatenate(
        [b[:d4], jnp.pad(b[d4:d4 + n_gate], (0, pad)), b[d4 + n_gate:]]).reshape(1, -1).astype(f32)
    w_dw_p = jnp.pad(w_dw[i].astype(f32), ((0, HIST_ROWS - CONV_WIDTH), (0, 0)))
    return dict(
        g_mix=row(g_mix), w_in=w_in_p, b_in=b_in_p, g_head=row(g_head), w_dw=w_dw_p,
        b_dw=row(b_dw), g_cn=row(g_cn), b_cn=row(b_cn), w_out=w_out[i].astype(bf16),
        g_ffn=row(g_ffn), w_ff1=w_ff1[i].astype(bf16), w_ff2=w_ff2[i].astype(bf16),
        g_ple=row(g_ple), w_gate=w_ple_gate[i].astype(bf16), w_proj=w_ple_proj[i].astype(bf16))


def kernel(x_prompt, x_sample, state_mlstm_C, state_mlstm_n, state_mlstm_m, cache_conv, p_prompt,
           p_sample, g_mix, w_in, b_in, g_head, w_dw, b_dw, g_cn, b_cn, w_out, g_ffn, w_ff1, w_ff2,
           g_ple, w_ple_gate, w_ple_proj, g_final):
    depth = w_in.shape[0]
    assert depth == 1, "the final norm is fused into the layer kernels"
    bs, seq_len, d_model = x_sample.shape
    assert seq_len & (seq_len - 1) == 0 and seq_len <= 8 and SAMPLE_GROUP & (SAMPLE_GROUP - 1) == 0
    g_fin = g_final.reshape(1, -1).astype(f32)

    i = 0
    lw = _layer_weights(i, g_mix, w_in, b_in, g_head, w_dw, b_dw, g_cn, b_cn, w_out, g_ffn,
                        w_ff1, w_ff2, g_ple, w_ple_gate, w_ple_proj)
    tail_ws = (lw["w_out"], lw["g_ffn"], lw["w_ff1"], lw["w_ff2"], lw["g_ple"], lw["w_gate"],
               lw["w_proj"], g_fin)

    prompt_ws = (lw["g_mix"], lw["w_in"], lw["b_in"], lw["g_head"], lw["w_dw"], lw["b_dw"],
                 lw["g_cn"], lw["b_cn"]) + tail_ws
    y_p, c_p, n_p, m_p, conv_p = _prompt_call(x_prompt, p_prompt[i], prompt_ws)
    m_p = m_p[:, 0, F_LANE:F_LANE + N_HEADS]

    xs = x_sample.reshape(bs * seq_len, d_model)
    ps = p_sample[i].reshape(bs * seq_len, -1)
    qkvo, gates, u = _sample_in_call(xs, lw["g_mix"], lw["w_in"], lw["b_in"])
    m0 = jnp.pad(state_mlstm_m[i].astype(f32), ((0, 0), (F_LANE, GATE_LANES - F_LANE - N_HEADS)))
    mrep = jnp.repeat(m0, seq_len, axis=0)
    hm, cact, c_s, n_s, mt, conv_s = _sample_rec_call(
        seq_len, qkvo, gates, u, mrep, state_mlstm_C[i], state_mlstm_n[i].reshape(bs, -1),
        cache_conv[i].transpose(1, 0, 2), lw["g_head"], lw["w_dw"], lw["b_dw"], lw["g_cn"],
        lw["b_cn"])
    conv_s = conv_s.transpose(1, 0, 2)
    y_s = _sample_tail_call(xs, hm, cact, ps, tail_ws).reshape(bs, seq_len, d_model)
    m_s = mt[seq_len - 1::seq_len, F_LANE:F_LANE + N_HEADS]
    n_s = n_s.reshape(bs, N_HEADS, DK)

    stack = lambda a: a[None]
    return (y_p, y_s, stack(c_p), stack(n_p), stack(m_p), stack(conv_p),
            stack(c_s), stack(n_s), stack(m_s), stack(conv_s))
```

```python
import functools

import jax
import jax.numpy as jnp
from jax import lax
from jax.experimental import pallas as pl
from jax.experimental.pallas import tpu as pltpu

f32 = jnp.float32
bf16 = jnp.bfloat16

N_HEADS = 4
DK = 128
D_MLSTM = N_HEADS * DK
CONV_WIDTH = 31
CONV_BUF = CONV_WIDTH - 1
EPS = 1e-6
GATE_LANES = 128
F_LANE = N_HEADS
CHUNK = 128
TILE_ROWS = 256
HIST_ROWS = 32
SAMPLE_GROUP = 32
FFN_PIECES = 4
SEQ_UNROLL = 8
VMEM_LIMIT_BYTES = 56 * 1024 * 1024


def _bdot(a, b):
    return jnp.dot(a, b, preferred_element_type=f32)


def _dot_nt(a, b):
    return lax.dot_general(a, b, (((1,), (1,)), ((), ())), preferred_element_type=f32)


def _dot_tn(a, b):
    return lax.dot_general(a, b, (((0,), (0,)), ((), ())), preferred_element_type=f32)


def _rms(x, g):
    y = x * lax.rsqrt(jnp.mean(x * x, axis=-1, keepdims=True) + EPS)
    return y * g


def _ln(x):
    mu = jnp.mean(x, axis=-1, keepdims=True)
    xc = x - mu
    return xc * lax.rsqrt(jnp.mean(xc * xc, axis=-1, keepdims=True) + EPS)


def _exact_left(sel16, x):
    hi = x.astype(bf16)
    r = x - hi.astype(f32)
    mid = r.astype(bf16)
    lo = (r - mid.astype(f32)).astype(bf16)
    return _bdot(sel16, hi) + _bdot(sel16, mid) + _bdot(sel16, lo)


def _lane_pick(rows, lane_ids):
    out = jnp.zeros((rows[0].shape[0], GATE_LANES), f32)
    for h, r in enumerate(rows):
        out = jnp.where(lane_ids == F_LANE + h, r, out)
    return out


def _in_proj(x, g_mix_ref, w_in_ref, b_in_ref):
    xn = _rms(x, g_mix_ref[...]).astype(bf16)

    def proj(a, b):
        return _bdot(xn, w_in_ref[:, a:b]) + b_in_ref[:, a:b]

    d = D_MLSTM
    q = proj(0, d)
    k = proj(d, 2 * d) * (DK ** -0.5)
    v = proj(2 * d, 3 * d)
    o = proj(3 * d, 4 * d)
    gates = proj(4 * d, 4 * d + GATE_LANES)
    c0 = 4 * d + GATE_LANES
    ga = proj(c0, c0 + d)
    gg = proj(c0 + d, c0 + 2 * d)
    u = ga * jax.nn.sigmoid(gg)
    return q, k, v, o, gates, u


def _gate_cumsum(gates, lcum16):
    lf = jax.nn.log_sigmoid(gates)
    bc = _exact_left(lcum16, lf)
    return lf, bc, gates.T, bc.T


def _chunk_weights(gates, bc, gates_t, bc_t, h, mask, m_prev):
    icol = gates[:, h:h + 1]
    bcol = bc[:, F_LANE + h:F_LANE + h + 1]
    irow = gates_t[h:h + 1, :]
    brow = bc_t[F_LANE + h:F_LANE + h + 1, :]
    logw = jnp.where(mask, (bcol - brow) + irow, -jnp.inf)
    m_intra = jnp.max(logw, axis=-1, keepdims=True)
    log_inter = bcol + m_prev
    m_t = jnp.maximum(log_inter, m_intra)
    w_intra = jnp.exp(logw - m_t)
    w_inter = jnp.exp(log_inter - m_t)
    return icol, bcol, m_t, w_intra, w_inter


def _v_aug(v):
    lane = lax.broadcasted_iota(jnp.int32, v.shape, 1)
    return jnp.concatenate([v, jnp.where(lane == 0, 1.0, 0.0)], axis=1)


def _intra(q16, k16, vaug16, w_intra):
    s = _dot_nt(q16, k16)
    a = (w_intra * s).astype(bf16)
    return _bdot(a, vaug16)


def _finish(nd, w_inter, num_inter, den_inter, m_t):
    num = nd[:, :DK] + w_inter * num_inter
    den = nd[:, DK:DK + 1] + w_inter * den_inter
    return num / jnp.maximum(jnp.abs(den), jnp.exp(-m_t))


def _head_out(h, o, g_head):
    return (_ln(h) * g_head) * jax.nn.sigmoid(o)


def _conv_post(c, g_cn_ref, b_cn_ref):
    c = _ln(c) * g_cn_ref[...] + b_cn_ref[...]
    return c * jax.nn.sigmoid(c)


def _out_proj(x, hm, c, w_out_ref):
    d = D_MLSTM
    return x + (_bdot(hm.astype(bf16), w_out_ref[0:d, :]) + _bdot(c.astype(bf16), w_out_ref[d:, :]))


def _ffn_part(xn16, w_ff1_ref, w_ff2_ref, j0, j1):
    f = jnp.maximum(_bdot(xn16, w_ff1_ref[:, j0:j1]), 0.0)
    return _bdot((f * f).astype(bf16), w_ff2_ref[j0:j1, :])


def _ple_final(x, p, g_ple_ref, w_gate_ref, w_proj_ref, g_final_ref):
    gate = jax.nn.sigmoid(_bdot(_rms(x, g_ple_ref[...]).astype(bf16), w_gate_ref[...]))
    x = x + gate * _bdot(p.astype(bf16), w_proj_ref[...])
    return _rms(x, g_final_ref[...])


def _tail(x, hm, c, p, w_out_ref, g_ffn_ref, w_ff1_ref, w_ff2_ref, g_ple_ref,
          w_gate_ref, w_proj_ref, g_final_ref):
    x = _out_proj(x, hm, c, w_out_ref)
    xn = _rms(x, g_ffn_ref[...]).astype(bf16)
    d_ff = w_ff1_ref.shape[1]
    step = 1024
    acc = None
    for j in range(0, d_ff, step):
        part = _ffn_part(xn, w_ff1_ref, w_ff2_ref, j, j + step)
        acc = part if acc is None else acc + part
    return _ple_final(x + acc, p, g_ple_ref, w_gate_ref, w_proj_ref, g_final_ref)


def _cumsum_rows(x):
    sub = 8
    pos = lax.broadcasted_iota(jnp.int32, x.shape, 0) & (sub - 1)
    sh = 1
    while sh < sub:
        x = x + jnp.where(pos >= sh, pltpu.roll(x, sh, axis=0), 0.0)
        sh *= 2
    tiles = []
    carry = None
    for r0 in range(0, x.shape[0], sub):
        tile = x[r0:r0 + sub]
        if carry is not None:
            tile = tile + carry
        carry = tile[sub - 1:sub]
        tiles.append(tile)
    return jnp.concatenate(tiles, axis=0)


def _prompt_kernel(nt, xf_ref, xb_ref, p_ref, g_mix_ref, w_in_ref, b_in_ref, g_head_ref, w_dw_ref,
                   b_dw_ref, g_cn_ref, b_cn_ref, w_out_ref, g_ffn_ref, w_ff1_ref, w_ff2_ref,
                   g_ple_ref, w_gate_ref, w_proj_ref, g_final_ref,
                   y_ref, c_out_ref, n_out_ref, m_out_ref, conv_out_ref,
                   ct_s, m_s, ext_s, hm_s, cact_s, acc_s, xn_s, q_s, k_s, v_s, o_s, g_s, gc_s,
                   hh_s, cc_s):
    s = pl.program_id(0)
    n_tiles = pl.num_programs(0) - 1
    t = lax.rem(lax.rem(s, n_tiles), nt)
    tm = xf_ref.shape[0]
    off = HIST_ROWS - CONV_BUF

    @pl.when(s == 0)
    def _():
        hm_s[...] = jnp.zeros_like(hm_s)
        cact_s[...] = jnp.zeros_like(cact_s)

    @pl.when(t == 0)
    def _():
        ct_s[...] = jnp.zeros_like(ct_s)
        m_s[...] = jnp.zeros_like(m_s)
        ext_s[0:HIST_ROWS, :] = jnp.zeros((HIST_ROWS, ext_s.shape[1]), f32)

    x1 = _out_proj(xb_ref[...], hm_s[...], cact_s[...], w_out_ref)
    acc_s[...] = x1
    xn_s[...] = _rms(x1, g_ffn_ref[...]).astype(bf16)
    q, k, v, o, gates, u = _in_proj(xf_ref[...], g_mix_ref, w_in_ref, b_in_ref)
    q_s[...] = q.astype(bf16)
    k_s[...] = k.astype(bf16)
    v_s[...] = v
    o_s[...] = o
    g_s[...] = gates
    ext_s[HIST_ROWS:HIST_ROWS + tm, :] = u

    rid = lax.broadcasted_iota(jnp.int32, (CHUNK, CHUNK), 0)
    cid = lax.broadcasted_iota(jnp.int32, (CHUNK, CHUNK), 1)
    causal = cid <= rid
    lane_row = lax.broadcasted_iota(jnp.int32, (1, GATE_LANES), 1)

    def ffn_piece(j0, j1):
        acc_s[...] += _ffn_part(xn_s[...], w_ff1_ref, w_ff2_ref, j0, j1)

    def conv_block(r0, rb):
        sub = 8
        acc = None
        for ph in range(sub):
            n_rows = rb if ph == 0 else rb + sub
            grp = None
            for j in range(ph, off + CONV_WIDTH, sub):
                if j < off:
                    continue
                term = w_dw_ref[j - off:j - off + 1, :] * ext_s[r0 + j - ph:r0 + j - ph + n_rows, :]
                grp = term if grp is None else grp + term
            part = grp if ph == 0 else grp[ph:ph + rb]
            acc = part if acc is None else acc + part
        cc_s[r0:r0 + rb, :] = acc + b_dw_ref[...]

    def mlstm_piece(ci, h):
        rows = slice(ci * CHUNK, (ci + 1) * CHUNK)
        cols = slice(h * DK, (h + 1) * DK)
        gates_c = g_s[rows, :]
        if h == 0:
            bc = _cumsum_rows(jax.nn.log_sigmoid(gates_c))
            gc_s[0] = bc
            gc_s[1] = gates_c.T
            gc_s[2] = bc.T
        m_row = m_s[...]
        m_prev = m_row[:, F_LANE + h:F_LANE + h + 1]
        icol, bcol, m_t, w_intra, w_inter = _chunk_weights(
            gates_c, gc_s[0], gc_s[1], gc_s[2], h, causal, m_prev)
        q16 = q_s[rows, cols]
        k16 = k_s[rows, cols]
        vaug = _v_aug(v_s[rows, cols])
        nd = _intra(q16, k16, vaug.astype(bf16), w_intra)
        ct = ct_s[h]
        qc = _bdot(q16, ct.astype(bf16))
        hh_s[rows, cols] = _finish(nd, w_inter, qc[:, :DK], qc[:, DK:DK + 1], m_t)
        m_new = m_t[CHUNK - 1:CHUNK]
        b_last = bcol[CHUNK - 1:CHUNK]
        gcol = jnp.exp(((b_last - bcol) + icol) - m_new)
        decay = jnp.exp(b_last + m_prev - m_new)
        ct_s[h] = decay * ct + _dot_tn(k16, (gcol * vaug).astype(bf16))
        m_s[...] = jnp.where(lane_row == F_LANE + h, m_new, m_row)

    n_piece = (tm // CHUNK) * N_HEADS
    d_ff = w_ff1_ref.shape[1]
    fstep, rb = d_ff // FFN_PIECES, tm // n_piece
    for j in range(n_piece):
        if j % (n_piece // FFN_PIECES) == 0:
            jf = j // (n_piece // FFN_PIECES)
            ffn_piece(jf * fstep, (jf + 1) * fstep)
        conv_block(j * rb, rb)
        mlstm_piece(j // N_HEADS, j % N_HEADS)

    y_ref[...] = _ple_final(acc_s[...], p_ref[...], g_ple_ref, w_gate_ref, w_proj_ref,
                            g_final_ref)
    hm = jnp.concatenate(
        [_head_out(hh_s[:, h * DK:(h + 1) * DK], o_s[:, h * DK:(h + 1) * DK],
                   g_head_ref[:, h * DK:(h + 1) * DK]) for h in range(N_HEADS)], axis=1)
    hm_s[...] = hm.astype(bf16)
    cact_s[...] = _conv_post(cc_s[...], g_cn_ref, b_cn_ref).astype(bf16)
    ext_s[0:HIST_ROWS, :] = ext_s[tm:tm + HIST_ROWS, :]

    @pl.when(jnp.logical_and(t == nt - 1, s < n_tiles))
    def _():
        for h in range(N_HEADS):
            ctt = ct_s[h].T
            c_out_ref[h] = ctt[:DK]
            n_out_ref[h:h + 1, :] = ctt[DK:DK + 1]
        m_out_ref[...] = jnp.broadcast_to(m_s[...], m_out_ref.shape)
        conv_out_ref[...] = ext_s[off:HIST_ROWS, :]


def _const_spec(shape):
    return pl.BlockSpec(shape, lambda *_: (0,) * len(shape), pipeline_mode=pl.Buffered(1))


def _weight_specs(ws):
    return [_const_spec(w.shape) for w in ws]


def _prompt_call(x, p, ws):
    B, T, D = x.shape
    tm = TILE_ROWS
    nt = T // tm
    n_tiles = B * nt
    d_conv = ws[4].shape[1]
    front = lambda s: (lax.rem(s, n_tiles) // nt, lax.rem(lax.rem(s, n_tiles), nt), 0)
    back = lambda s: (jnp.maximum(s - 1, 0) // nt, lax.rem(jnp.maximum(s - 1, 0), nt), 0)
    state = lambda s: (jnp.minimum(s, n_tiles - 1) // nt, 0, 0)
    sd = jax.ShapeDtypeStruct
    rows_buf = lambda w, dt: pltpu.VMEM((tm, w), dt)
    return pl.pallas_call(
        functools.partial(_prompt_kernel, nt),
        grid=(n_tiles + 1,),
        in_specs=[pl.BlockSpec((None, tm, D), front), pl.BlockSpec((None, tm, D), back),
                  pl.BlockSpec((None, tm, p.shape[2]), back)] + _weight_specs(ws),
        out_specs=(
            pl.BlockSpec((None, tm, D), back),
            pl.BlockSpec((None, N_HEADS, DK, DK), lambda s: state(s) + (0,)),
            pl.BlockSpec((None, N_HEADS, DK), state),
            pl.BlockSpec((None, 8, GATE_LANES), state),
            pl.BlockSpec((None, CONV_BUF, d_conv), state),
        ),
        out_shape=(
            sd((B, T, D), f32),
            sd((B, N_HEADS, DK, DK), f32),
            sd((B, N_HEADS, DK), f32),
            sd((B, 8, GATE_LANES), f32),
            sd((B, CONV_BUF, d_conv), f32),
        ),
        scratch_shapes=[
            pltpu.VMEM((N_HEADS, DK, 2 * DK), f32),
            pltpu.VMEM((1, GATE_LANES), f32),
            pltpu.VMEM((HIST_ROWS + tm, d_conv), f32),
            rows_buf(D_MLSTM, bf16), rows_buf(d_conv, bf16),
            rows_buf(D, f32), rows_buf(D, bf16),
            rows_buf(D_MLSTM, bf16), rows_buf(D_MLSTM, bf16),
            rows_buf(D_MLSTM, f32), rows_buf(D_MLSTM, f32),
            rows_buf(GATE_LANES, f32),
            pltpu.VMEM((3, CHUNK, GATE_LANES), f32),
            rows_buf(D_MLSTM, f32), rows_buf(d_conv, f32),
        ],
        compiler_params=pltpu.CompilerParams(
            dimension_semantics=("arbitrary",), vmem_limit_bytes=VMEM_LIMIT_BYTES),
        name="prompt_layer",
    )(x, x, p, *ws)


def _sample_in_kernel(x_ref, g_mix_ref, w_in_ref, b_in_ref, qkvo_ref, gates_ref, u_ref):
    q, k, v, o, gates, u = _in_proj(x_ref[...], g_mix_ref, w_in_ref, b_in_ref)
    qkvo_ref[...] = jnp.concatenate([q, k, v, o], axis=1)
    gates_ref[...] = gates
    u_ref[...] = u


def _sample_in_call(x, g_mix, w_in, b_in):
    R, D = x.shape
    tm = TILE_ROWS
    d_conv = (w_in.shape[1] - 4 * D_MLSTM - GATE_LANES) // 2
    sd = jax.ShapeDtypeStruct
    return pl.pallas_call(
        _sample_in_kernel,
        grid=(R // tm,),
        in_specs=[pl.BlockSpec((tm, D), lambda i: (i, 0))] + _weight_specs((g_mix, w_in, b_in)),
        out_specs=(pl.BlockSpec((tm, 4 * D_MLSTM), lambda i: (i, 0)),
                   pl.BlockSpec((tm, GATE_LANES), lambda i: (i, 0)),
                   pl.BlockSpec((tm, d_conv), lambda i: (i, 0))),
        out_shape=(sd((R, 4 * D_MLSTM), f32), sd((R, GATE_LANES), f32), sd((R, d_conv), f32)),
        compiler_params=pltpu.CompilerParams(
            dimension_semantics=("arbitrary",), vmem_limit_bytes=VMEM_LIMIT_BYTES),
        name="sample_in_proj",
    )(x, g_mix, w_in, b_in)


def _sample_rec_kernel(seq_len, qkvo_ref, gates_ref, u_ref, mrep_ref, c_ref, n_ref, hist_ref,
                       g_head_ref, w_dw_ref, b_dw_ref, g_cn_ref, b_cn_ref,
                       hm_ref, cact_ref, c_out_ref, n_out_ref, mt_ref, hist_out_ref,
                       numi_s, dec_s, gk_s):
    rows = qkvo_ref.shape[0]
    nb = rows // seq_len
    shift = seq_len.bit_length() - 1
    d = D_MLSTM

    rid = lax.broadcasted_iota(jnp.int32, (rows, rows), 0)
    cid = lax.broadcasted_iota(jnp.int32, (rows, rows), 1)
    same = (rid >> shift) == (cid >> shift)
    mask = jnp.logical_and(same, cid <= rid)
    lcum16 = jnp.where(mask, 1.0, 0.0).astype(bf16)
    bsum16 = jnp.where(same, 1.0, 0.0).astype(bf16)
    plast16 = jnp.where(cid == ((rid >> shift) << shift) + (seq_len - 1), 1.0, 0.0).astype(bf16)
    lane_g = lax.broadcasted_iota(jnp.int32, (rows, GATE_LANES), 1)

    gates = gates_ref[...]
    lf, bc, gates_t, bc_t = _gate_cumsum(gates, lcum16)
    blast = _exact_left(bsum16, lf)
    mrep = mrep_ref[...]

    q = qkvo_ref[:, 0:d]
    k = qkvo_ref[:, d:2 * d]
    v = qkvo_ref[:, 2 * d:3 * d]
    o = qkvo_ref[:, 3 * d:4 * d]
    q16 = q.astype(bf16)
    k16 = k.astype(bf16)

    per_head = []
    for h in range(N_HEADS):
        cols = slice(h * DK, (h + 1) * DK)
        m_prev = mrep[:, F_LANE + h:F_LANE + h + 1]
        icol, bcol, m_t, w_intra, w_inter = _chunk_weights(gates, bc, gates_t, bc_t, h, mask, m_prev)
        vaug = _v_aug(v[:, cols])
        nd = _intra(q16[:, cols], k16[:, cols], vaug.astype(bf16), w_intra)
        per_head.append((icol, bcol, m_t, w_inter, nd))
    mt_all = _lane_pick([ph[2] for ph in per_head], lane_g)
    mt_ref[...] = mt_all
    mnew = _exact_left(plast16, mt_all)
    dec_s[...] = jnp.exp(blast + mrep - mnew)
    gvt16 = []
    for h in range(N_HEADS):
        cols = slice(h * DK, (h + 1) * DK)
        icol, bcol, _, _, _ = per_head[h]
        lanes = slice(F_LANE + h, F_LANE + h + 1)
        gcol = jnp.exp(((blast[:, lanes] - bcol) + icol) - mnew[:, lanes])
        gvt16.append((gcol * v[:, cols]).T.astype(bf16))
        gk_s[:, cols] = gcol * k[:, cols]

    rsel = lax.broadcasted_iota(jnp.int32, (rows, nb), 0)
    bsel = lax.broadcasted_iota(jnp.int32, (rows, nb), 1)
    expand16 = jnp.where((rsel >> shift) == bsel, 1.0, 0.0).astype(bf16)
    nrep = _exact_left(expand16, n_ref[...])

    grp = lax.broadcasted_iota(jnp.int32, (rows, DK), 0) >> shift
    numi_s[...] = jnp.zeros_like(numi_s)

    sub = 8
    tile_shift = (sub // seq_len).bit_length() - 1
    in_tile = lax.broadcasted_iota(jnp.int32, (sub, d), 0) >> shift

    def per_seq(b, carry):
        mb = grp == b
        r0 = pl.multiple_of((b >> tile_shift) * sub, sub)
        mine = in_tile == (b & (sub // seq_len - 1))
        q8 = qkvo_ref[pl.ds(r0, sub), 0:d]
        dec_b = dec_s[pl.ds(b * seq_len, 1), :]
        n_b = n_ref[pl.ds(b, 1), :]
        gk8 = jnp.sum(jnp.where(mine, gk_s[pl.ds(r0, sub), :], 0.0), axis=0, keepdims=True)
        n_new = []
        num_i = []
        for h in range(N_HEADS):
            cols = slice(h * DK, (h + 1) * DK)
            cb = c_ref[b, h]
            num_i.append(_dot_nt(q8[:, cols].astype(bf16), cb.astype(bf16)))
            kb = jnp.where(mb, k[:, cols], 0.0).astype(bf16)
            dec = dec_b[:, F_LANE + h:F_LANE + h + 1]
            c_out_ref[b, h] = dec * cb + _bdot(gvt16[h], kb)
            n_new.append(dec * n_b[:, cols] + gk8[:, cols])
        numi_s[pl.ds(r0, sub), :] = jnp.where(mine, jnp.concatenate(num_i, axis=1),
                                              numi_s[pl.ds(r0, sub), :])
        n_out_ref[pl.ds(b, 1), :] = jnp.concatenate(n_new, axis=1)
        return carry

    lax.fori_loop(0, nb, per_seq, 0, unroll=SEQ_UNROLL)

    hm = []
    for h in range(N_HEADS):
        cols = slice(h * DK, (h + 1) * DK)
        _, _, m_t, w_inter, nd = per_head[h]
        den_i = jnp.sum(q[:, cols] * nrep[:, cols], axis=-1, keepdims=True)
        hh = _finish(nd, w_inter, numi_s[:, cols], den_i, m_t)
        hm.append(_head_out(hh, o[:, cols], g_head_ref[:, cols]))
    hm_ref[...] = jnp.concatenate(hm, axis=1)

    rt = lax.broadcasted_iota(jnp.int32, (rows, rows), 0)
    rbm = lax.broadcasted_iota(jnp.int32, (rows, rows), 1)
    nb_shift = nb.bit_length() - 1
    to_tm16 = jnp.where(rbm == ((rt & (nb - 1)) << shift) + (rt >> nb_shift), 1.0, 0.0).astype(bf16)
    to_bm16 = jnp.where(rt == ((rbm & (nb - 1)) << shift) + (rbm >> nb_shift), 1.0, 0.0).astype(bf16)
    u_tm = _exact_left(to_tm16, u_ref[...])
    u_steps = [u_tm[s * nb:(s + 1) * nb] for s in range(seq_len)]
    c_steps = []
    for t in range(seq_len):
        acc = None
        for j in range(t, CONV_BUF):
            term = w_dw_ref[j - t:j - t + 1, :] * hist_ref[j]
            acc = term if acc is None else acc + term
        for s in range(t + 1):
            acc = acc + w_dw_ref[CONV_BUF + s - t:CONV_BUF + s - t + 1, :] * u_steps[s]
        c_steps.append(acc + b_dw_ref[...])
    cact_tm = _conv_post(jnp.concatenate(c_steps, axis=0), g_cn_ref, b_cn_ref)
    cact_ref[...] = _bdot(to_bm16, cact_tm.astype(bf16))
    hist_out_ref[0:CONV_BUF - seq_len] = hist_ref[seq_len:CONV_BUF]
    for s in range(seq_len):
        hist_out_ref[CONV_BUF - seq_len + s] = u_steps[s]


def _sample_rec_call(seq_len, qkvo, gates, u, mrep, c_state, n_state, hist,
                     g_head, w_dw, b_dw, g_cn, b_cn):
    R = qkvo.shape[0]
    nb_total = c_state.shape[0]
    gb = SAMPLE_GROUP
    rows = gb * seq_len
    d_conv = u.shape[1]
    sd = jax.ShapeDtypeStruct
    rspec = lambda w: pl.BlockSpec((rows, w), lambda i: (i, 0))
    return pl.pallas_call(
        functools.partial(_sample_rec_kernel, seq_len),
        grid=(nb_total // gb,),
        in_specs=[rspec(qkvo.shape[1]), rspec(GATE_LANES), rspec(d_conv), rspec(GATE_LANES),
                  pl.BlockSpec((gb, N_HEADS, DK, DK), lambda i: (i, 0, 0, 0)),
                  pl.BlockSpec((gb, D_MLSTM), lambda i: (i, 0)),
                  pl.BlockSpec((CONV_BUF, gb, d_conv), lambda i: (0, i, 0))]
        + _weight_specs((g_head, w_dw, b_dw, g_cn, b_cn)),
        out_specs=(rspec(D_MLSTM), rspec(d_conv),
                   pl.BlockSpec((gb, N_HEADS, DK, DK), lambda i: (i, 0, 0, 0)),
                   pl.BlockSpec((gb, D_MLSTM), lambda i: (i, 0)),
                   rspec(GATE_LANES),
                   pl.BlockSpec((CONV_BUF, gb, d_conv), lambda i: (0, i, 0))),
        out_shape=(sd((R, D_MLSTM), f32), sd((R, d_conv), f32),
                   sd(c_state.shape, f32), sd(n_state.shape, f32),
                   sd((R, GATE_LANES), f32), sd(hist.shape, f32)),
        scratch_shapes=[pltpu.VMEM((rows, D_MLSTM), f32), pltpu.VMEM((rows, GATE_LANES), f32),
                        pltpu.VMEM((rows, D_MLSTM), f32)],
        compiler_params=pltpu.CompilerParams(
            dimension_semantics=("arbitrary",), vmem_limit_bytes=VMEM_LIMIT_BYTES),
        name="sample_recurrent",
    )(qkvo, gates, u, mrep, c_state, n_state, hist, g_head, w_dw, b_dw, g_cn, b_cn)


def _sample_tail_kernel(x_ref, hm_ref, cact_ref, p_ref, w_out_ref, g_ffn_ref, w_ff1_ref, w_ff2_ref,
                        g_ple_ref, w_gate_ref, w_proj_ref, g_final_ref, y_ref):
    y_ref[...] = _tail(x_ref[...], hm_ref[...], cact_ref[...], p_ref[...], w_out_ref, g_ffn_ref,
                       w_ff1_ref, w_ff2_ref, g_ple_ref, w_gate_ref, w_proj_ref, g_final_ref)


def _sample_tail_call(x, hm, cact, p, ws):
    R, D = x.shape
    tm = TILE_ROWS
    rspec = lambda w: pl.BlockSpec((tm, w), lambda i: (i, 0))
    return pl.pallas_call(
        _sample_tail_kernel,
        grid=(R // tm,),
        in_specs=[rspec(D), rspec(hm.shape[1]), rspec(cact.shape[1]), rspec(p.shape[1])]
        + _weight_specs(ws),
        out_specs=rspec(D),
        out_shape=jax.ShapeDtypeStruct((R, D), f32),
        compiler_params=pltpu.CompilerParams(
            dimension_semantics=("arbitrary",), vmem_limit_bytes=VMEM_LIMIT_BYTES),
        name="sample_tail",
    )(x, hm, cact, p, *ws)


def _w_in_layout_kernel(d4, n_gate, wt_ref, o_ref):
    blk = TILE_ROWS
    rest = wt_ref.shape[0] - d4 - n_gate
    for c in range(0, d4, blk):
        o_ref[:, c:c + blk] = wt_ref[c:c + blk, :].T.astype(bf16)
    lane = lax.broadcasted_iota(jnp.int32, (wt_ref.shape[1], GATE_LANES), 1)
    gates = wt_ref[d4:d4 + GATE_LANES, :].T
    o_ref[:, d4:d4 + GATE_LANES] = jnp.where(lane < n_gate, gates, 0.0).astype(bf16)
    for c in range(0, rest, blk):
        o_ref[:, d4 + GATE_LANES + c:d4 + GATE_LANES + c + blk] = (
            wt_ref[d4 + n_gate + c:d4 + n_gate + c + blk, :].T.astype(bf16))


def _w_in_layout_call(wt, d4, n_gate):
    width, rows = wt.shape
    out_w = width - n_gate + GATE_LANES
    return pl.pallas_call(
        functools.partial(_w_in_layout_kernel, d4, n_gate),
        grid=(1,),
        in_specs=[_const_spec(wt.shape)],
        out_specs=_const_spec((rows, out_w)),
        out_shape=jax.ShapeDtypeStruct((rows, out_w), bf16),
        compiler_params=pltpu.CompilerParams(
            dimension_semantics=("arbitrary",), vmem_limit_bytes=VMEM_LIMIT_BYTES),
        name="w_in_layout",
    )(wt)


def _layer_weights(i, g_mix, w_in, b_in, g_head, w_dw, b_dw, g_cn, b_cn, w_out, g_ffn, w_ff1,
                   w_ff2, g_ple, w_ple_gate, w_ple_proj):
    row = lambda a: a[i].reshape(1, -1).astype(f32)
    d4 = 4 * D_MLSTM
    n_gate = 2 * N_HEADS
    pad = GATE_LANES - n_gate
    w_in_p = _w_in_layout_call(w_in[i].T, d4, n_gate)
    b = b_in[i]
    b_in_p = jnp.concatenate(
        [b[:d4], jnp.pad(b[d4:d4 + n_gate], (0, pad)), b[d4 + n_gate:]]).reshape(1, -1).astype(f32)
    w_dw_p = jnp.pad(w_dw[i].astype(f32), ((0, HIST_ROWS - CONV_WIDTH), (0, 0)))
    return dict(
        g_mix=row(g_mix), w_in=w_in_p, b_in=b_in_p, g_head=row(g_head), w_dw=w_dw_p,
        b_dw=row(b_dw), g_cn=row(g_cn), b_cn=row(b_cn), w_out=w_out[i].astype(bf16),
        g_ffn=row(g_ffn), w_ff1=w_ff1[i].astype(bf16), w_ff2=w_ff2[i].astype(bf16),
        g_ple=row(g_ple), w_gate=w_ple_gate[i].astype(bf16), w_proj=w_ple_proj[i].astype(bf16))


def kernel(x_prompt, x_sample, state_mlstm_C, state_mlstm_n, state_mlstm_m, cache_conv, p_prompt,
           p_sample, g_mix, w_in, b_in, g_head, w_dw, b_dw, g_cn, b_cn, w_out, g_ffn, w_ff1, w_ff2,
           g_ple, w_ple_gate, w_ple_proj, g_final):
    depth = w_in.shape[0]
    assert depth == 1, "the final norm is fused into the layer kernels"
    bs, seq_len, d_model = x_sample.shape
    assert seq_len & (seq_len - 1) == 0 and seq_len <= 8 and SAMPLE_GROUP & (SAMPLE_GROUP - 1) == 0
    g_fin = g_final.reshape(1, -1).astype(f32)

    i = 0
    lw = _layer_weights(i, g_mix, w_in, b_in, g_head, w_dw, b_dw, g_cn, b_cn, w_out, g_ffn,
                        w_ff1, w_ff2, g_ple, w_ple_gate, w_ple_proj)
    tail_ws = (lw["w_out"], lw["g_ffn"], lw["w_ff1"], lw["w_ff2"], lw["g_ple"], lw["w_gate"],
               lw["w_proj"], g_fin)

    prompt_ws = (lw["g_mix"], lw["w_in"], lw["b_in"], lw["g_head"], lw["w_dw"], lw["b_dw"],
                 lw["g_cn"], lw["b_cn"]) + tail_ws
    y_p, c_p, n_p, m_p, conv_p = _prompt_call(x_prompt, p_prompt[i], prompt_ws)
    m_p = m_p[:, 0, F_LANE:F_LANE + N_HEADS]

    xs = x_sample.reshape(bs * seq_len, d_model)
    ps = p_sample[i].reshape(bs * seq_len, -1)
    qkvo, gates, u = _sample_in_call(xs, lw["g_mix"], lw["w_in"], lw["b_in"])
    m0 = jnp.pad(state_mlstm_m[i].astype(f32), ((0, 0), (F_LANE, GATE_LANES - F_LANE - N_HEADS)))
    mrep = jnp.repeat(m0, seq_len, axis=0)
    hm, cact, c_s, n_s, mt, conv_s = _sample_rec_call(
        seq_len, qkvo, gates, u, mrep, state_mlstm_C[i], state_mlstm_n[i].reshape(bs, -1),
        cache_conv[i].transpose(1, 0, 2), lw["g_head"], lw["w_dw"], lw["b_dw"], lw["g_cn"],
        lw["b_cn"])
    conv_s = conv_s.transpose(1, 0, 2)
    y_s = _sample_tail_call(xs, hm, cact, ps, tail_ws).reshape(bs, seq_len, d_model)
    m_s = mt[seq_len - 1::seq_len, F_LANE:F_LANE + N_HEADS]
    n_s = n_s.reshape(bs, N_HEADS, DK)

    stack = lambda a: a[None]
    return (y_p, y_s, stack(c_p), stack(n_p), stack(m_p), stack(conv_p),
            stack(c_s), stack(n_s), stack(m_s), stack(conv_s))
```

```python
import functools

import jax
import jax.numpy as jnp
from jax import lax
from jax.experimental import pallas as pl
from jax.experimental.pallas import tpu as pltpu

f32 = jnp.float32
bf16 = jnp.bfloat16

N_HEADS = 4
DK = 128
D_MLSTM = N_HEADS * DK
CONV_WIDTH = 31
CONV_BUF = CONV_WIDTH - 1
EPS = 1e-6
GATE_LANES = 128
F_LANE = N_HEADS
CHUNK = 128
TILE_ROWS = 256
HIST_ROWS = 32
SAMPLE_GROUP = 32
FFN_PIECES = 4
SEQ_UNROLL = 8
VMEM_LIMIT_BYTES = 56 * 1024 * 1024


def _bdot(a, b):
    return jnp.dot(a, b, preferred_element_type=f32)


def _dot_nt(a, b):
    return lax.dot_general(a, b, (((1,), (1,)), ((), ())), preferred_element_type=f32)


def _dot_tn(a, b):
    return lax.dot_general(a, b, (((0,), (0,)), ((), ())), preferred_element_type=f32)


def _rms(x, g):
    y = x * lax.rsqrt(jnp.mean(x * x, axis=-1, keepdims=True) + EPS)
    return y * g


def _ln(x):
    mu = jnp.mean(x, axis=-1, keepdims=True)
    xc = x - mu
    return xc * lax.rsqrt(jnp.mean(xc * xc, axis=-1, keepdims=True) + EPS)


def _exact_left(sel16, x):
    hi = x.astype(bf16)
    r = x - hi.astype(f32)
    mid = r.astype(bf16)
    lo = (r - mid.astype(f32)).astype(bf16)
    return _bdot(sel16, hi) + _bdot(sel16, mid) + _bdot(sel16, lo)


def _lane_pick(rows, lane_ids):
    out = jnp.zeros((rows[0].shape[0], GATE_LANES), f32)
    for h, r in enumerate(rows):
        out = jnp.where(lane_ids == F_LANE + h, r, out)
    return out


def _in_proj(x, g_mix_ref, w_in_ref, b_in_ref):
    xn = _rms(x, g_mix_ref[...]).astype(bf16)

    def proj(a, b):
        return _bdot(xn, w_in_ref[:, a:b]) + b_in_ref[:, a:b]

    d = D_MLSTM
    q = proj(0, d)
    k = proj(d, 2 * d) * (DK ** -0.5)
    v = proj(2 * d, 3 * d)
    o = proj(3 * d, 4 * d)
    gates = proj(4 * d, 4 * d + GATE_LANES)
    c0 = 4 * d + GATE_LANES
    ga = proj(c0, c0 + d)
    gg = proj(c0 + d, c0 + 2 * d)
    u = ga * jax.nn.sigmoid(gg)
    return q, k, v, o, gates, u


def _gate_cumsum(gates, lcum16):
    lf = jax.nn.log_sigmoid(gates)
    bc = _exact_left(lcum16, lf)
    return lf, bc, gates.T, bc.T


def _chunk_weights(gates, bc, gates_t, bc_t, h, mask, m_prev):
    icol = gates[:, h:h + 1]
    bcol = bc[:, F_LANE + h:F_LANE + h + 1]
    irow = gates_t[h:h + 1, :]
    brow = bc_t[F_LANE + h:F_LANE + h + 1, :]
    logw = jnp.where(mask, (bcol - brow) + irow, -jnp.inf)
    m_intra = jnp.max(logw, axis=-1, keepdims=True)
    log_inter = bcol + m_prev
    m_t = jnp.maximum(log_inter, m_intra)
    w_intra = jnp.exp(logw - m_t)
    w_inter = jnp.exp(log_inter - m_t)
    return icol, bcol, m_t, w_intra, w_inter


def _v_aug(v):
    lane = lax.broadcasted_iota(jnp.int32, v.shape, 1)
    return jnp.concatenate([v, jnp.where(lane == 0, 1.0, 0.0)], axis=1)


def _intra(q16, k16, vaug16, w_intra):
    s = _dot_nt(q16, k16)
    a = (w_intra * s).astype(bf16)
    return _bdot(a, vaug16)


def _finish(nd, w_inter, num_inter, den_inter, m_t):
    num = nd[:, :DK] + w_inter * num_inter
    den = nd[:, DK:DK + 1] + w_inter * den_inter
    return num / jnp.maximum(jnp.abs(den), jnp.exp(-m_t))


def _head_out(h, o, g_head):
    return (_ln(h) * g_head) * jax.nn.sigmoid(o)


def _conv_post(c, g_cn_ref, b_cn_ref):
    c = _ln(c) * g_cn_ref[...] + b_cn_ref[...]
    return c * jax.nn.sigmoid(c)


def _out_proj(x, hm, c, w_out_ref):
    d = D_MLSTM
    return x + (_bdot(hm.astype(bf16), w_out_ref[0:d, :]) + _bdot(c.astype(bf16), w_out_ref[d:, :]))


def _ffn_part(xn16, w_ff1_ref, w_ff2_ref, j0, j1):
    f = jnp.maximum(_bdot(xn16, w_ff1_ref[:, j0:j1]), 0.0)
    return _bdot((f * f).astype(bf16), w_ff2_ref[j0:j1, :])


def _ple_final(x, p, g_ple_ref, w_gate_ref, w_proj_ref, g_final_ref):
    gate = jax.nn.sigmoid(_bdot(_rms(x, g_ple_ref[...]).astype(bf16), w_gate_ref[...]))
    x = x + gate * _bdot(p.astype(bf16), w_proj_ref[...])
    return _rms(x, g_final_ref[...])


def _tail(x, hm, c, p, w_out_ref, g_ffn_ref, w_ff1_ref, w_ff2_ref, g_ple_ref,
          w_gate_ref, w_proj_ref, g_final_ref):
    x = _out_proj(x, hm, c, w_out_ref)
    xn = _rms(x, g_ffn_ref[...]).astype(bf16)
    d_ff = w_ff1_ref.shape[1]
    step = 1024
    acc = None
    for j in range(0, d_ff, step):
        part = _ffn_part(xn, w_ff1_ref, w_ff2_ref, j, j + step)
        acc = part if acc is None else acc + part
    return _ple_final(x + acc, p, g_ple_ref, w_gate_ref, w_proj_ref, g_final_ref)


def _cumsum_rows(x):
    sub = 8
    pos = lax.broadcasted_iota(jnp.int32, x.shape, 0) & (sub - 1)
    sh = 1
    while sh < sub:
        x = x + jnp.where(pos >= sh, pltpu.roll(x, sh, axis=0), 0.0)
        sh *= 2
    tiles = []
    carry = None
    for r0 in range(0, x.shape[0], sub):
        tile = x[r0:r0 + sub]
        if carry is not None:
            tile = tile + carry
        carry = tile[sub - 1:sub]
        tiles.append(tile)
    return jnp.concatenate(tiles, axis=0)


def _prompt_kernel(nt, xf_ref, xb_ref, p_ref, g_mix_ref, w_in_ref, b_in_ref, g_head_ref, w_dw_ref,
                   b_dw_ref, g_cn_ref, b_cn_ref, w_out_ref, g_ffn_ref, w_ff1_ref, w_ff2_ref,
                   g_ple_ref, w_gate_ref, w_proj_ref, g_final_ref,
                   y_ref, c_out_ref, n_out_ref, m_out_ref, conv_out_ref,
                   ct_s, m_s, ext_s, hm_s, cact_s, acc_s, xn_s, q_s, k_s, v_s, o_s, g_s, gc_s,
                   hh_s, cc_s):
    s = pl.program_id(0)
    n_tiles = pl.num_programs(0) - 1
    t = lax.rem(lax.rem(s, n_tiles), nt)
    tm = xf_ref.shape[0]
    off = HIST_ROWS - CONV_BUF

    @pl.when(s == 0)
    def _():
        hm_s[...] = jnp.zeros_like(hm_s)
        cact_s[...] = jnp.zeros_like(cact_s)

    @pl.when(t == 0)
    def _():
        ct_s[...] = jnp.zeros_like(ct_s)
        m_s[...] = jnp.zeros_like(m_s)
        ext_s[0:HIST_ROWS, :] = jnp.zeros((HIST_ROWS, ext_s.shape[1]), f32)

    x1 = _out_proj(xb_ref[...], hm_s[...], cact_s[...], w_out_ref)
    acc_s[...] = x1
    xn_s[...] = _rms(x1, g_ffn_ref[...]).astype(bf16)
    q, k, v, o, gates, u = _in_proj(xf_ref[...], g_mix_ref, w_in_ref, b_in_ref)
    q_s[...] = q.astype(bf16)
    k_s[...] = k.astype(bf16)
    v_s[...] = v
    o_s[...] = o
    g_s[...] = gates
    ext_s[HIST_ROWS:HIST_ROWS + tm, :] = u

    rid = lax.broadcasted_iota(jnp.int32, (CHUNK, CHUNK), 0)
    cid = lax.broadcasted_iota(jnp.int32, (CHUNK, CHUNK), 1)
    causal = cid <= rid
    lane_row = lax.broadcasted_iota(jnp.int32, (1, GATE_LANES), 1)

    def ffn_piece(j0, j1):
        acc_s[...] += _ffn_part(xn_s[...], w_ff1_ref, w_ff2_ref, j0, j1)

    def conv_block(r0, rb):
        sub = 8
        acc = None
        for ph in range(sub):
            n_rows = rb if ph == 0 else rb + sub
            grp = None
            for j in range(ph, off + CONV_WIDTH, sub):
                if j < off:
                    continue
                term = w_dw_ref[j - off:j - off + 1, :] * ext_s[r0 + j - ph:r0 + j - ph + n_rows, :]
                grp = term if grp is None else grp + term
            part = grp if ph == 0 else grp[ph:ph + rb]
            acc = part if acc is None else acc + part
        cc_s[r0:r0 + rb, :] = acc + b_dw_ref[...]

    def mlstm_piece(ci, h):
        rows = slice(ci * CHUNK, (ci + 1) * CHUNK)
        cols = slice(h * DK, (h + 1) * DK)
        gates_c = g_s[rows, :]
        if h == 0:
            bc = _cumsum_rows(jax.nn.log_sigmoid(gates_c))
            gc_s[0] = bc
            gc_s[1] = gates_c.T
            gc_s[2] = bc.T
        m_row = m_s[...]
        m_prev = m_row[:, F_LANE + h:F_LANE + h + 1]
        icol, bcol, m_t, w_intra, w_inter = _chunk_weights(
            gates_c, gc_s[0], gc_s[1], gc_s[2], h, causal, m_prev)
        q16 = q_s[rows, cols]
        k16 = k_s[rows, cols]
        vaug = _v_aug(v_s[rows, cols])
        nd = _intra(q16, k16, vaug.astype(bf16), w_intra)
        ct = ct_s[h]
        qc = _bdot(q16, ct.astype(bf16))
        hh_s[rows, cols] = _finish(nd, w_inter, qc[:, :DK], qc[:, DK:DK + 1], m_t)
        m_new = m_t[CHUNK - 1:CHUNK]
        b_last = bcol[CHUNK - 1:CHUNK]
        gcol = jnp.exp(((b_last - bcol) + icol) - m_new)
        decay = jnp.exp(b_last + m_prev - m_new)
        ct_s[h] = decay * ct + _dot_tn(k16, (gcol * vaug).astype(bf16))
        m_s[...] = jnp.where(lane_row == F_LANE + h, m_new, m_row)

    n_piece = (tm // CHUNK) * N_HEADS
    d_ff = w_ff1_ref.shape[1]
    fstep, rb = d_ff // FFN_PIECES, tm // n_piece
    for j in range(n_piece):
        if j % (n_piece // FFN_PIECES) == 0:
            jf = j // (n_piece // FFN_PIECES)
            ffn_piece(jf * fstep, (jf + 1) * fstep)
        conv_block(j * rb, rb)
        mlstm_piece(j // N_HEADS, j % N_HEADS)

    y_ref[...] = _ple_final(acc_s[...], p_ref[...], g_ple_ref, w_gate_ref, w_proj_ref,
                            g_final_ref)
    hm = jnp.concatenate(
        [_head_out(hh_s[:, h * DK:(h + 1) * DK], o_s[:, h * DK:(h + 1) * DK],
                   g_head_ref[:, h * DK:(h + 1) * DK]) for h in range(N_HEADS)], axis=1)
    hm_s[...] = hm.astype(bf16)
    cact_s[...] = _conv_post(cc_s[...], g_cn_ref, b_cn_ref).astype(bf16)
    ext_s[0:HIST_ROWS, :] = ext_s[tm:tm + HIST_ROWS, :]

    @pl.when(jnp.logical_and(t == nt - 1, s < n_tiles))
    def _():
        for h in range(N_HEADS):
            ctt = ct_s[h].T
            c_out_ref[h] = ctt[:DK]
            n_out_ref[h:h + 1, :] = ctt[DK:DK + 1]
        m_out_ref[...] = jnp.broadcast_to(m_s[...], m_out_ref.shape)
        conv_out_ref[...] = ext_s[off:HIST_ROWS, :]


def _const_spec(shape):
    return pl.BlockSpec(shape, lambda *_: (0,) * len(shape), pipeline_mode=pl.Buffered(1))


def _weight_specs(ws):
    return [_const_spec(w.shape) for w in ws]


def _prompt_call(x, p, ws):
    B, T, D = x.shape
    tm = TILE_ROWS
    nt = T // tm
    n_tiles = B * nt
    d_conv = ws[4].shape[1]
    front = lambda s: (lax.rem(s, n_tiles) // nt, lax.rem(lax.rem(s, n_tiles), nt), 0)
    back = lambda s: (jnp.maximum(s - 1, 0) // nt, lax.rem(jnp.maximum(s - 1, 0), nt), 0)
    state = lambda s: (jnp.minimum(s, n_tiles - 1) // nt, 0, 0)
    sd = jax.ShapeDtypeStruct
    rows_buf = lambda w, dt: pltpu.VMEM((tm, w), dt)
    return pl.pallas_call(
        functools.partial(_prompt_kernel, nt),
        grid=(n_tiles + 1,),
        in_specs=[pl.BlockSpec((None, tm, D), front), pl.BlockSpec((None, tm, D), back),
                  pl.BlockSpec((None, tm, p.shape[2]), back)] + _weight_specs(ws),
        out_specs=(
            pl.BlockSpec((None, tm, D), back),
            pl.BlockSpec((None, N_HEADS, DK, DK), lambda s: state(s) + (0,)),
            pl.BlockSpec((None, N_HEADS, DK), state),
            pl.BlockSpec((None, 8, GATE_LANES), state),
            pl.BlockSpec((None, CONV_BUF, d_conv), state),
        ),
        out_shape=(
            sd((B, T, D), f32),
            sd((B, N_HEADS, DK, DK), f32),
            sd((B, N_HEADS, DK), f32),
            sd((B, 8, GATE_LANES), f32),
            sd((B, CONV_BUF, d_conv), f32),
        ),
        scratch_shapes=[
            pltpu.VMEM((N_HEADS, DK, 2 * DK), f32),
            pltpu.VMEM((1, GATE_LANES), f32),
            pltpu.VMEM((HIST_ROWS + tm, d_conv), f32),
            rows_buf(D_MLSTM, bf16), rows_buf(d_conv, bf16),
            rows_buf(D, f32), rows_buf(D, bf16),
            rows_buf(D_MLSTM, bf16), rows_buf(D_MLSTM, bf16),
            rows_buf(D_MLSTM, f32), rows_buf(D_MLSTM, f32),
            rows_buf(GATE_LANES, f32),
            pltpu.VMEM((3, CHUNK, GATE_LANES), f32),
            rows_buf(D_MLSTM, f32), rows_buf(d_conv, f32),
        ],
        compiler_params=pltpu.CompilerParams(
            dimension_semantics=("arbitrary",), vmem_limit_bytes=VMEM_LIMIT_BYTES),
        name="prompt_layer",
    )(x, x, p, *ws)


def _sample_in_kernel(x_ref, g_mix_ref, w_in_ref, b_in_ref, qkvo_ref, gates_ref, u_ref):
    q, k, v, o, gates, u = _in_proj(x_ref[...], g_mix_ref, w_in_ref, b_in_ref)
    qkvo_ref[...] = jnp.concatenate([q, k, v, o], axis=1)
    gates_ref[...] = gates
    u_ref[...] = u


def _sample_in_call(x, g_mix, w_in, b_in):
    R, D = x.shape
    tm = R
    d_conv = (w_in.shape[1] - 4 * D_MLSTM - GATE_LANES) // 2
    sd = jax.ShapeDtypeStruct
    return pl.pallas_call(
        _sample_in_kernel,
        grid=(R // tm,),
        in_specs=[pl.BlockSpec((tm, D), lambda i: (i, 0))] + _weight_specs((g_mix, w_in, b_in)),
        out_specs=(pl.BlockSpec((tm, 4 * D_MLSTM), lambda i: (i, 0)),
                   pl.BlockSpec((tm, GATE_LANES), lambda i: (i, 0)),
                   pl.BlockSpec((tm, d_conv), lambda i: (i, 0))),
        out_shape=(sd((R, 4 * D_MLSTM), f32), sd((R, GATE_LANES), f32), sd((R, d_conv), f32)),
        compiler_params=pltpu.CompilerParams(
            dimension_semantics=("arbitrary",), vmem_limit_bytes=VMEM_LIMIT_BYTES),
        name="sample_in_proj",
    )(x, g_mix, w_in, b_in)


def _sample_rec_kernel(seq_len, qkvo_ref, gates_ref, u_ref, mrep_ref, c_ref, n_ref, hist_ref,
                       g_head_ref, w_dw_ref, b_dw_ref, g_cn_ref, b_cn_ref,
                       hm_ref, cact_ref, c_out_ref, n_out_ref, mt_ref, hist_out_ref,
                       numi_s, dec_s, gk_s):
    rows = qkvo_ref.shape[0]
    nb = rows // seq_len
    shift = seq_len.bit_length() - 1
    d = D_MLSTM

    rid = lax.broadcasted_iota(jnp.int32, (rows, rows), 0)
    cid = lax.broadcasted_iota(jnp.int32, (rows, rows), 1)
    same = (rid >> shift) == (cid >> shift)
    mask = jnp.logical_and(same, cid <= rid)
    lcum16 = jnp.where(mask, 1.0, 0.0).astype(bf16)
    bsum16 = jnp.where(same, 1.0, 0.0).astype(bf16)
    plast16 = jnp.where(cid == ((rid >> shift) << shift) + (seq_len - 1), 1.0, 0.0).astype(bf16)
    lane_g = lax.broadcasted_iota(jnp.int32, (rows, GATE_LANES), 1)

    gates = gates_ref[...]
    lf, bc, gates_t, bc_t = _gate_cumsum(gates, lcum16)
    blast = _exact_left(bsum16, lf)
    mrep = mrep_ref[...]

    q = qkvo_ref[:, 0:d]
    k = qkvo_ref[:, d:2 * d]
    v = qkvo_ref[:, 2 * d:3 * d]
    o = qkvo_ref[:, 3 * d:4 * d]
    q16 = q.astype(bf16)
    k16 = k.astype(bf16)

    per_head = []
    for h in range(N_HEADS):
        cols = slice(h * DK, (h + 1) * DK)
        m_prev = mrep[:, F_LANE + h:F_LANE + h + 1]
        icol, bcol, m_t, w_intra, w_inter = _chunk_weights(gates, bc, gates_t, bc_t, h, mask, m_prev)
        vaug = _v_aug(v[:, cols])
        nd = _intra(q16[:, cols], k16[:, cols], vaug.astype(bf16), w_intra)
        per_head.append((icol, bcol, m_t, w_inter, nd))
    mt_all = _lane_pick([ph[2] for ph in per_head], lane_g)
    mt_ref[...] = mt_all
    mnew = _exact_left(plast16, mt_all)
    dec_s[...] = jnp.exp(blast + mrep - mnew)
    gvt16 = []
    for h in range(N_HEADS):
        cols = slice(h * DK, (h + 1) * DK)
        icol, bcol, _, _, _ = per_head[h]
        lanes = slice(F_LANE + h, F_LANE + h + 1)
        gcol = jnp.exp(((blast[:, lanes] - bcol) + icol) - mnew[:, lanes])
        gvt16.append((gcol * v[:, cols]).T.astype(bf16))
        gk_s[:, cols] = gcol * k[:, cols]

    rsel = lax.broadcasted_iota(jnp.int32, (rows, nb), 0)
    bsel = lax.broadcasted_iota(jnp.int32, (rows, nb), 1)
    expand16 = jnp.where((rsel >> shift) == bsel, 1.0, 0.0).astype(bf16)
    nrep = _exact_left(expand16, n_ref[...])

    grp = lax.broadcasted_iota(jnp.int32, (rows, DK), 0) >> shift
    numi_s[...] = jnp.zeros_like(numi_s)

    sub = 8
    tile_shift = (sub // seq_len).bit_length() - 1
    in_tile = lax.broadcasted_iota(jnp.int32, (sub, d), 0) >> shift

    def per_seq(b, carry):
        mb = grp == b
        r0 = pl.multiple_of((b >> tile_shift) * sub, sub)
        mine = in_tile == (b & (sub // seq_len - 1))
        q8 = qkvo_ref[pl.ds(r0, sub), 0:d]
        dec_b = dec_s[pl.ds(b * seq_len, 1), :]
        n_b = n_ref[pl.ds(b, 1), :]
        gk8 = jnp.sum(jnp.where(mine, gk_s[pl.ds(r0, sub), :], 0.0), axis=0, keepdims=True)
        n_new = []
        num_i = []
        for h in range(N_HEADS):
            cols = slice(h * DK, (h + 1) * DK)
            cb = c_ref[b, h]
            num_i.append(_dot_nt(q8[:, cols].astype(bf16), cb.astype(bf16)))
            kb = jnp.where(mb, k[:, cols], 0.0).astype(bf16)
            dec = dec_b[:, F_LANE + h:F_LANE + h + 1]
            c_out_ref[b, h] = dec * cb + _bdot(gvt16[h], kb)
            n_new.append(dec * n_b[:, cols] + gk8[:, cols])
        numi_s[pl.ds(r0, sub), :] = jnp.where(mine, jnp.concatenate(num_i, axis=1),
                                              numi_s[pl.ds(r0, sub), :])
        n_out_ref[pl.ds(b, 1), :] = jnp.concatenate(n_new, axis=1)
        return carry

    lax.fori_loop(0, nb, per_seq, 0, unroll=SEQ_UNROLL)

    hm = []
    for h in range(N_HEADS):
        cols = slice(h * DK, (h + 1) * DK)
        _, _, m_t, w_inter, nd = per_head[h]
        den_i = jnp.sum(q[:, cols] * nrep[:, cols], axis=-1, keepdims=True)
        hh = _finish(nd, w_inter, numi_s[:, cols], den_i, m_t)
        hm.append(_head_out(hh, o[:, cols], g_head_ref[:, cols]))
    hm_ref[...] = jnp.concatenate(hm, axis=1)

    rt = lax.broadcasted_iota(jnp.int32, (rows, rows), 0)
    rbm = lax.broadcasted_iota(jnp.int32, (rows, rows), 1)
    nb_shift = nb.bit_length() - 1
    to_tm16 = jnp.where(rbm == ((rt & (nb - 1)) << shift) + (rt >> nb_shift), 1.0, 0.0).astype(bf16)
    to_bm16 = jnp.where(rt == ((rbm & (nb - 1)) << shift) + (rbm >> nb_shift), 1.0, 0.0).astype(bf16)
    u_tm = _exact_left(to_tm16, u_ref[...])
    u_steps = [u_tm[s * nb:(s + 1) * nb] for s in range(seq_len)]
    c_steps = []
    for t in range(seq_len):
        acc = None
        for j in range(t, CONV_BUF):
            term = w_dw_ref[j - t:j - t + 1, :] * hist_ref[j]
            acc = term if acc is None else acc + term
        for s in range(t + 1):
            acc = acc + w_dw_ref[CONV_BUF + s - t:CONV_BUF + s - t + 1, :] * u_steps[s]
        c_steps.append(acc + b_dw_ref[...])
    cact_tm = _conv_post(jnp.concatenate(c_steps, axis=0), g_cn_ref, b_cn_ref)
    cact_ref[...] = _bdot(to_bm16, cact_tm.astype(bf16))
    hist_out_ref[0:CONV_BUF - seq_len] = hist_ref[seq_len:CONV_BUF]
    for s in range(seq_len):
        hist_out_ref[CONV_BUF - seq_len + s] = u_steps[s]


def _sample_rec_call(seq_len, qkvo, gates, u, mrep, c_state, n_state, hist,
                     g_head, w_dw, b_dw, g_cn, b_cn):
    R = qkvo.shape[0]
    nb_total = c_state.shape[0]
    gb = SAMPLE_GROUP
    rows = gb * seq_len
    d_conv = u.shape[1]
    sd = jax.ShapeDtypeStruct
    rspec = lambda w: pl.BlockSpec((rows, w), lambda i: (i, 0))
    return pl.pallas_call(
        functools.partial(_sample_rec_kernel, seq_len),
        grid=(nb_total // gb,),
        in_specs=[rspec(qkvo.shape[1]), rspec(GATE_LANES), rspec(d_conv), rspec(GATE_LANES),
                  pl.BlockSpec((gb, N_HEADS, DK, DK), lambda i: (i, 0, 0, 0)),
                  pl.BlockSpec((gb, D_MLSTM), lambda i: (i, 0)),
                  pl.BlockSpec((CONV_BUF, gb, d_conv), lambda i: (0, i, 0))]
        + _weight_specs((g_head, w_dw, b_dw, g_cn, b_cn)),
        out_specs=(rspec(D_MLSTM), rspec(d_conv),
                   pl.BlockSpec((gb, N_HEADS, DK, DK), lambda i: (i, 0, 0, 0)),
                   pl.BlockSpec((gb, D_MLSTM), lambda i: (i, 0)),
                   rspec(GATE_LANES),
                   pl.BlockSpec((CONV_BUF, gb, d_conv), lambda i: (0, i, 0))),
        out_shape=(sd((R, D_MLSTM), f32), sd((R, d_conv), f32),
                   sd(c_state.shape, f32), sd(n_state.shape, f32),
                   sd((R, GATE_LANES), f32), sd(hist.shape, f32)),
        scratch_shapes=[pltpu.VMEM((rows, D_MLSTM), f32), pltpu.VMEM((rows, GATE_LANES), f32),
                        pltpu.VMEM((rows, D_MLSTM), f32)],
        compiler_params=pltpu.CompilerParams(
            dimension_semantics=("arbitrary",), vmem_limit_bytes=VMEM_LIMIT_BYTES),
        name="sample_recurrent",
    )(qkvo, gates, u, mrep, c_state, n_state, hist, g_head, w_dw, b_dw, g_cn, b_cn)


def _sample_tail_kernel(x_ref, hm_ref, cact_ref, p_ref, w_out_ref, g_ffn_ref, w_ff1_ref, w_ff2_ref,
                        g_ple_ref, w_gate_ref, w_proj_ref, g_final_ref, y_ref):
    y_ref[...] = _tail(x_ref[...], hm_ref[...], cact_ref[...], p_ref[...], w_out_ref, g_ffn_ref,
                       w_ff1_ref, w_ff2_ref, g_ple_ref, w_gate_ref, w_proj_ref, g_final_ref)


def _sample_tail_call(x, hm, cact, p, ws):
    R, D = x.shape
    tm = R
    rspec = lambda w: pl.BlockSpec((tm, w), lambda i: (i, 0))
    return pl.pallas_call(
        _sample_tail_kernel,
        grid=(R // tm,),
        in_specs=[rspec(D), rspec(hm.shape[1]), rspec(cact.shape[1]), rspec(p.shape[1])]
        + _weight_specs(ws),
        out_specs=rspec(D),
        out_shape=jax.ShapeDtypeStruct((R, D), f32),
        compiler_params=pltpu.CompilerParams(
            dimension_semantics=("arbitrary",), vmem_limit_bytes=VMEM_LIMIT_BYTES),
        name="sample_tail",
    )(x, hm, cact, p, *ws)


def _w_in_layout_kernel(d4, n_gate, wt_ref, o_ref):
    blk = TILE_ROWS
    rest = wt_ref.shape[0] - d4 - n_gate
    for c in range(0, d4, blk):
        o_ref[:, c:c + blk] = wt_ref[c:c + blk, :].T.astype(bf16)
    lane = lax.broadcasted_iota(jnp.int32, (wt_ref.shape[1], GATE_LANES), 1)
    gates = wt_ref[d4:d4 + GATE_LANES, :].T
    o_ref[:, d4:d4 + GATE_LANES] = jnp.where(lane < n_gate, gates, 0.0).astype(bf16)
    for c in range(0, rest, blk):
        o_ref[:, d4 + GATE_LANES + c:d4 + GATE_LANES + c + blk] = (
            wt_ref[d4 + n_gate + c:d4 + n_gate + c + blk, :].T.astype(bf16))


def _w_in_layout_call(wt, d4, n_gate):
    width, rows = wt.shape
    out_w = width - n_gate + GATE_LANES
    return pl.pallas_call(
        functools.partial(_w_in_layout_kernel, d4, n_gate),
        grid=(1,),
        in_specs=[_const_spec(wt.shape)],
        out_specs=_const_spec((rows, out_w)),
        out_shape=jax.ShapeDtypeStruct((rows, out_w), bf16),
        compiler_params=pltpu.CompilerParams(
            dimension_semantics=("arbitrary",), vmem_limit_bytes=VMEM_LIMIT_BYTES),
        name="w_in_layout",
    )(wt)


def _cast_kernel(w_ref, o_ref):
    o_ref[...] = w_ref[...].astype(o_ref.dtype)


def _cast_call(w, dtype):
    rows, width = w.shape
    rb = TILE_ROWS
    return pl.pallas_call(
        _cast_kernel,
        grid=(rows // rb,),
        in_specs=[pl.BlockSpec((rb, width), lambda i: (i, 0))],
        out_specs=pl.BlockSpec((rb, width), lambda i: (i, 0)),
        out_shape=jax.ShapeDtypeStruct((rows, width), dtype),
        compiler_params=pltpu.CompilerParams(
            dimension_semantics=("arbitrary",), vmem_limit_bytes=VMEM_LIMIT_BYTES),
        name="weight_cast",
    )(w)


def _layer_weights(i, g_mix, w_in, b_in, g_head, w_dw, b_dw, g_cn, b_cn, w_out, g_ffn, w_ff1,
                   w_ff2, g_ple, w_ple_gate, w_ple_proj):
    row = lambda a: a[i].reshape(1, -1).astype(f32)
    d4 = 4 * D_MLSTM
    n_gate = 2 * N_HEADS
    pad = GATE_LANES - n_gate
    w_in_p = _w_in_layout_call(w_in[i].T, d4, n_gate)
    b = b_in[i]
    b_in_p = jnp.concatenate(
        [b[:d4], jnp.pad(b[d4:d4 + n_gate], (0, pad)), b[d4 + n_gate:]]).reshape(1, -1).astype(f32)
    w_dw_p = jnp.pad(w_dw[i].astype(f32), ((0, HIST_ROWS - CONV_WIDTH), (0, 0)))
    return dict(
        g_mix=row(g_mix), w_in=w_in_p, b_in=b_in_p, g_head=row(g_head), w_dw=w_dw_p,
        b_dw=row(b_dw), g_cn=row(g_cn), b_cn=row(b_cn), w_out=w_out[i].astype(bf16),
        g_ffn=row(g_ffn), w_ff1=_cast_call(w_ff1[i], bf16), w_ff2=w_ff2[i].astype(bf16),
        g_ple=row(g_ple), w_gate=w_ple_gate[i].astype(bf16), w_proj=w_ple_proj[i].astype(bf16))


def kernel(x_prompt, x_sample, state_mlstm_C, state_mlstm_n, state_mlstm_m, cache_conv, p_prompt,
           p_sample, g_mix, w_in, b_in, g_head, w_dw, b_dw, g_cn, b_cn, w_out, g_ffn, w_ff1, w_ff2,
           g_ple, w_ple_gate, w_ple_proj, g_final):
    depth = w_in.shape[0]
    assert depth == 1, "the final norm is fused into the layer kernels"
    bs, seq_len, d_model = x_sample.shape
    assert seq_len & (seq_len - 1) == 0 and seq_len <= 8 and SAMPLE_GROUP & (SAMPLE_GROUP - 1) == 0
    g_fin = g_final.reshape(1, -1).astype(f32)

    i = 0
    lw = _layer_weights(i, g_mix, w_in, b_in, g_head, w_dw, b_dw, g_cn, b_cn, w_out, g_ffn,
                        w_ff1, w_ff2, g_ple, w_ple_gate, w_ple_proj)
    tail_ws = (lw["w_out"], lw["g_ffn"], lw["w_ff1"], lw["w_ff2"], lw["g_ple"], lw["w_gate"],
               lw["w_proj"], g_fin)

    prompt_ws = (lw["g_mix"], lw["w_in"], lw["b_in"], lw["g_head"], lw["w_dw"], lw["b_dw"],
                 lw["g_cn"], lw["b_cn"]) + tail_ws
    y_p, c_p, n_p, m_p, conv_p = _prompt_call(x_prompt, p_prompt[i], prompt_ws)
    m_p = m_p[:, 0, F_LANE:F_LANE + N_HEADS]

    xs = x_sample.reshape(bs * seq_len, d_model)
    ps = p_sample[i].reshape(bs * seq_len, -1)
    qkvo, gates, u = _sample_in_call(xs, lw["g_mix"], lw["w_in"], lw["b_in"])
    m0 = jnp.pad(state_mlstm_m[i].astype(f32), ((0, 0), (F_LANE, GATE_LANES - F_LANE - N_HEADS)))
    mrep = jnp.repeat(m0, seq_len, axis=0)
    hm, cact, c_s, n_s, mt, conv_s = _sample_rec_call(
        seq_len, qkvo, gates, u, mrep, state_mlstm_C[i], state_mlstm_n[i].reshape(bs, -1),
        cache_conv[i].transpose(1, 0, 2), lw["g_head"], lw["w_dw"], lw["b_dw"], lw["g_cn"],
        lw["b_cn"])
    conv_s = conv_s.transpose(1, 0, 2)
    y_s = _sample_tail_call(xs, hm, cact, ps, tail_ws).reshape(bs, seq_len, d_model)
    m_s = mt[seq_len - 1::seq_len, F_LANE:F_LANE + N_HEADS]
    n_s = n_s.reshape(bs, N_HEADS, DK)

    stack = lambda a: a[None]
    return (y_p, y_s, stack(c_p), stack(n_p), stack(m_p), stack(conv_p),
            stack(c_s), stack(n_s), stack(m_s), stack(conv_s))
```

```python
import functools

import jax
import jax.numpy as jnp
from jax import lax
from jax.experimental import pallas as pl
from jax.experimental.pallas import tpu as pltpu

f32 = jnp.float32
bf16 = jnp.bfloat16

N_HEADS = 4
DK = 128
D_MLSTM = N_HEADS * DK
CONV_WIDTH = 31
CONV_BUF = CONV_WIDTH - 1
EPS = 1e-6
SUBLANES = 8
GATE_LANES = 128
F_LANE = N_HEADS
CHUNK = 128
TILE_ROWS = 256
HIST_ROWS = 32
SAMPLE_GROUP = 32
FFN_PIECES = 4
SEQ_UNROLL = 8
VMEM_LIMIT_BYTES = 56 * 1024 * 1024


def _bdot(a, b):
    return jnp.dot(a, b, preferred_element_type=f32)


def _dot_nt(a, b):
    return lax.dot_general(a, b, (((1,), (1,)), ((), ())), preferred_element_type=f32)


def _dot_tn(a, b):
    return lax.dot_general(a, b, (((0,), (0,)), ((), ())), preferred_element_type=f32)


def _rms(x, g):
    y = x * lax.rsqrt(jnp.mean(x * x, axis=-1, keepdims=True) + EPS)
    return y * g


def _ln(x):
    mu = jnp.mean(x, axis=-1, keepdims=True)
    xc = x - mu
    return xc * lax.rsqrt(jnp.mean(xc * xc, axis=-1, keepdims=True) + EPS)


def _exact_left(sel16, x):
    hi = x.astype(bf16)
    r = x - hi.astype(f32)
    mid = r.astype(bf16)
    lo = (r - mid.astype(f32)).astype(bf16)
    return _bdot(sel16, hi) + _bdot(sel16, mid) + _bdot(sel16, lo)


def _lane_pick(rows, lane_ids):
    out = jnp.zeros((rows[0].shape[0], GATE_LANES), f32)
    for h, r in enumerate(rows):
        out = jnp.where(lane_ids == F_LANE + h, r, out)
    return out


def _in_proj(x, g_mix_ref, w_in_ref, b_in_ref):
    xn = _rms(x, g_mix_ref[...]).astype(bf16)

    def proj(a, b):
        return _bdot(xn, w_in_ref[:, a:b]) + b_in_ref[:, a:b]

    d = D_MLSTM
    q = proj(0, d)
    k = proj(d, 2 * d) * (DK ** -0.5)
    v = proj(2 * d, 3 * d)
    o = proj(3 * d, 4 * d)
    gates = proj(4 * d, 4 * d + GATE_LANES)
    c0 = 4 * d + GATE_LANES
    ga = proj(c0, c0 + d)
    gg = proj(c0 + d, c0 + 2 * d)
    u = ga * jax.nn.sigmoid(gg)
    return q, k, v, o, gates, u


def _gate_cumsum(gates, lcum16):
    lf = jax.nn.log_sigmoid(gates)
    bc = _exact_left(lcum16, lf)
    return lf, bc, gates.T, bc.T


def _chunk_weights(gates, bc, gates_t, bc_t, h, mask, m_prev):
    icol = gates[:, h:h + 1]
    bcol = bc[:, F_LANE + h:F_LANE + h + 1]
    irow = gates_t[h:h + 1, :]
    brow = bc_t[F_LANE + h:F_LANE + h + 1, :]
    logw = jnp.where(mask, (bcol - brow) + irow, -jnp.inf)
    m_intra = jnp.max(logw, axis=-1, keepdims=True)
    log_inter = bcol + m_prev
    m_t = jnp.maximum(log_inter, m_intra)
    w_intra = jnp.exp(logw - m_t)
    w_inter = jnp.exp(log_inter - m_t)
    return icol, bcol, m_t, w_intra, w_inter


def _v_aug(v):
    lane = lax.broadcasted_iota(jnp.int32, v.shape, 1)
    return jnp.concatenate([v, jnp.where(lane == 0, 1.0, 0.0)], axis=1)


def _intra(q16, k16, vaug16, w_intra):
    s = _dot_nt(q16, k16)
    a = (w_intra * s).astype(bf16)
    return _bdot(a, vaug16)


def _finish(nd, w_inter, num_inter, den_inter, m_t):
    num = nd[:, :DK] + w_inter * num_inter
    den = nd[:, DK:DK + 1] + w_inter * den_inter
    return num / jnp.maximum(jnp.abs(den), jnp.exp(-m_t))


def _head_out(h, o, g_head):
    return (_ln(h) * g_head) * jax.nn.sigmoid(o)


def _conv_post(c, g_cn_ref, b_cn_ref):
    c = _ln(c) * g_cn_ref[...] + b_cn_ref[...]
    return c * jax.nn.sigmoid(c)


def _out_proj(x, hm, c, w_out_ref):
    d = D_MLSTM
    return x + (_bdot(hm.astype(bf16), w_out_ref[0:d, :]) + _bdot(c.astype(bf16), w_out_ref[d:, :]))


def _ffn_part(xn16, w_ff1_ref, w_ff2_ref, j0, j1):
    f = jnp.maximum(_bdot(xn16, w_ff1_ref[:, j0:j1]), 0.0)
    return _bdot((f * f).astype(bf16), w_ff2_ref[j0:j1, :])


def _ple_final(x, p, g_ple_ref, w_gate_ref, w_proj_ref, g_final_ref):
    gate = jax.nn.sigmoid(_bdot(_rms(x, g_ple_ref[...]).astype(bf16), w_gate_ref[...]))
    x = x + gate * _bdot(p.astype(bf16), w_proj_ref[...])
    return _rms(x, g_final_ref[...])


def _tail(x, hm, c, p, w_out_ref, g_ffn_ref, w_ff1_ref, w_ff2_ref, g_ple_ref,
          w_gate_ref, w_proj_ref, g_final_ref):
    x = _out_proj(x, hm, c, w_out_ref)
    xn = _rms(x, g_ffn_ref[...]).astype(bf16)
    d_ff = w_ff1_ref.shape[1]
    step = d_ff // FFN_PIECES
    acc = None
    for j in range(0, d_ff, step):
        part = _ffn_part(xn, w_ff1_ref, w_ff2_ref, j, j + step)
        acc = part if acc is None else acc + part
    return _ple_final(x + acc, p, g_ple_ref, w_gate_ref, w_proj_ref, g_final_ref)


def _cumsum_rows(x):
    sub = SUBLANES
    pos = lax.broadcasted_iota(jnp.int32, x.shape, 0) & (sub - 1)
    sh = 1
    while sh < sub:
        x = x + jnp.where(pos >= sh, pltpu.roll(x, sh, axis=0), 0.0)
        sh *= 2
    tiles = []
    carry = None
    for r0 in range(0, x.shape[0], sub):
        tile = x[r0:r0 + sub]
        if carry is not None:
            tile = tile + carry
        carry = tile[sub - 1:sub]
        tiles.append(tile)
    return jnp.concatenate(tiles, axis=0)


def _prompt_kernel(nt, xf_ref, xb_ref, p_ref, g_mix_ref, w_in_ref, b_in_ref, g_head_ref, w_dw_ref,
                   b_dw_ref, g_cn_ref, b_cn_ref, w_out_ref, g_ffn_ref, w_ff1_ref, w_ff2_ref,
                   g_ple_ref, w_gate_ref, w_proj_ref, g_final_ref,
                   y_ref, c_out_ref, n_out_ref, m_out_ref, conv_out_ref,
                   ct_s, m_s, ext_s, hm_s, cact_s, acc_s, xn_s, q_s, k_s, v_s, o_s, g_s, gc_s,
                   hh_s, cc_s):
    s = pl.program_id(0)
    n_tiles = pl.num_programs(0) - 1
    t = lax.rem(lax.rem(s, n_tiles), nt)
    tm = xf_ref.shape[0]
    off = HIST_ROWS - CONV_BUF

    @pl.when(s == 0)
    def _():
        hm_s[...] = jnp.zeros_like(hm_s)
        cact_s[...] = jnp.zeros_like(cact_s)

    @pl.when(t == 0)
    def _():
        ct_s[...] = jnp.zeros_like(ct_s)
        m_s[...] = jnp.zeros_like(m_s)
        ext_s[0:HIST_ROWS, :] = jnp.zeros((HIST_ROWS, ext_s.shape[1]), f32)

    x1 = _out_proj(xb_ref[...], hm_s[...], cact_s[...], w_out_ref)
    acc_s[...] = x1
    xn_s[...] = _rms(x1, g_ffn_ref[...]).astype(bf16)
    q, k, v, o, gates, u = _in_proj(xf_ref[...], g_mix_ref, w_in_ref, b_in_ref)
    q_s[...] = q.astype(bf16)
    k_s[...] = k.astype(bf16)
    v_s[...] = v
    o_s[...] = o
    g_s[...] = gates
    ext_s[HIST_ROWS:HIST_ROWS + tm, :] = u

    rid = lax.broadcasted_iota(jnp.int32, (CHUNK, CHUNK), 0)
    cid = lax.broadcasted_iota(jnp.int32, (CHUNK, CHUNK), 1)
    causal = cid <= rid
    lane_row = lax.broadcasted_iota(jnp.int32, (1, GATE_LANES), 1)

    def ffn_piece(j0, j1):
        acc_s[...] += _ffn_part(xn_s[...], w_ff1_ref, w_ff2_ref, j0, j1)

    def conv_block(r0, rb):
        sub = SUBLANES
        acc = None
        for ph in range(sub):
            n_rows = rb if ph == 0 else rb + sub
            grp = None
            for j in range(ph, off + CONV_WIDTH, sub):
                if j < off:
                    continue
                term = w_dw_ref[j - off:j - off + 1, :] * ext_s[r0 + j - ph:r0 + j - ph + n_rows, :]
                grp = term if grp is None else grp + term
            part = grp if ph == 0 else grp[ph:ph + rb]
            acc = part if acc is None else acc + part
        cc_s[r0:r0 + rb, :] = acc + b_dw_ref[...]

    def mlstm_piece(ci, h):
        rows = slice(ci * CHUNK, (ci + 1) * CHUNK)
        cols = slice(h * DK, (h + 1) * DK)
        gates_c = g_s[rows, :]
        if h == 0:
            bc = _cumsum_rows(jax.nn.log_sigmoid(gates_c))
            gc_s[0] = bc
            gc_s[1] = gates_c.T
            gc_s[2] = bc.T
        m_row = m_s[...]
        m_prev = m_row[:, F_LANE + h:F_LANE + h + 1]
        icol, bcol, m_t, w_intra, w_inter = _chunk_weights(
            gates_c, gc_s[0], gc_s[1], gc_s[2], h, causal, m_prev)
        q16 = q_s[rows, cols]
        k16 = k_s[rows, cols]
        vaug = _v_aug(v_s[rows, cols])
        nd = _intra(q16, k16, vaug.astype(bf16), w_intra)
        ct = ct_s[h]
        qc = _bdot(q16, ct.astype(bf16))
        hh_s[rows, cols] = _finish(nd, w_inter, qc[:, :DK], qc[:, DK:DK + 1], m_t)
        m_new = m_t[CHUNK - 1:CHUNK]
        b_last = bcol[CHUNK - 1:CHUNK]
        gcol = jnp.exp(((b_last - bcol) + icol) - m_new)
        decay = jnp.exp(b_last + m_prev - m_new)
        ct_s[h] = decay * ct + _dot_tn(k16, (gcol * vaug).astype(bf16))
        m_s[...] = jnp.where(lane_row == F_LANE + h, m_new, m_row)

    n_piece = (tm // CHUNK) * N_HEADS
    d_ff = w_ff1_ref.shape[1]
    fstep, rb = d_ff // FFN_PIECES, tm // n_piece
    for j in range(n_piece):
        if j % (n_piece // FFN_PIECES) == 0:
            jf = j // (n_piece // FFN_PIECES)
            ffn_piece(jf * fstep, (jf + 1) * fstep)
        conv_block(j * rb, rb)
        mlstm_piece(j // N_HEADS, j % N_HEADS)

    y_ref[...] = _ple_final(acc_s[...], p_ref[...], g_ple_ref, w_gate_ref, w_proj_ref,
                            g_final_ref)
    hm = jnp.concatenate(
        [_head_out(hh_s[:, h * DK:(h + 1) * DK], o_s[:, h * DK:(h + 1) * DK],
                   g_head_ref[:, h * DK:(h + 1) * DK]) for h in range(N_HEADS)], axis=1)
    hm_s[...] = hm.astype(bf16)
    cact_s[...] = _conv_post(cc_s[...], g_cn_ref, b_cn_ref).astype(bf16)
    ext_s[0:HIST_ROWS, :] = ext_s[tm:tm + HIST_ROWS, :]

    @pl.when(jnp.logical_and(t == nt - 1, s < n_tiles))
    def _():
        for h in range(N_HEADS):
            ctt = ct_s[h].T
            c_out_ref[h] = ctt[:DK]
            n_out_ref[h:h + 1, :] = ctt[DK:DK + 1]
        m_out_ref[...] = jnp.broadcast_to(m_s[...], m_out_ref.shape)
        conv_out_ref[...] = ext_s[off:HIST_ROWS, :]


def _const_spec(shape):
    return pl.BlockSpec(shape, lambda *_: (0,) * len(shape), pipeline_mode=pl.Buffered(1))


def _weight_specs(ws):
    return [_const_spec(w.shape) for w in ws]


def _prompt_call(x, p, ws):
    B, T, D = x.shape
    tm = TILE_ROWS
    nt = T // tm
    n_tiles = B * nt
    d_conv = ws[4].shape[1]
    front = lambda s: (lax.rem(s, n_tiles) // nt, lax.rem(lax.rem(s, n_tiles), nt), 0)
    back = lambda s: (jnp.maximum(s - 1, 0) // nt, lax.rem(jnp.maximum(s - 1, 0), nt), 0)
    state = lambda s: (jnp.minimum(s, n_tiles - 1) // nt, 0, 0)
    sd = jax.ShapeDtypeStruct
    rows_buf = lambda w, dt: pltpu.VMEM((tm, w), dt)
    return pl.pallas_call(
        functools.partial(_prompt_kernel, nt),
        grid=(n_tiles + 1,),
        in_specs=[pl.BlockSpec((None, tm, D), front), pl.BlockSpec((None, tm, D), back),
                  pl.BlockSpec((None, tm, p.shape[2]), back)] + _weight_specs(ws),
        out_specs=(
            pl.BlockSpec((None, tm, D), back),
            pl.BlockSpec((None, N_HEADS, DK, DK), lambda s: state(s) + (0,)),
            pl.BlockSpec((None, N_HEADS, DK), state),
            pl.BlockSpec((None, SUBLANES, GATE_LANES), state),
            pl.BlockSpec((None, CONV_BUF, d_conv), state),
        ),
        out_shape=(
            sd((B, T, D), f32),
            sd((B, N_HEADS, DK, DK), f32),
            sd((B, N_HEADS, DK), f32),
            sd((B, SUBLANES, GATE_LANES), f32),
            sd((B, CONV_BUF, d_conv), f32),
        ),
        scratch_shapes=[
            pltpu.VMEM((N_HEADS, DK, 2 * DK), f32),
            pltpu.VMEM((1, GATE_LANES), f32),
            pltpu.VMEM((HIST_ROWS + tm, d_conv), f32),
            rows_buf(D_MLSTM, bf16), rows_buf(d_conv, bf16),
            rows_buf(D, f32), rows_buf(D, bf16),
            rows_buf(D_MLSTM, bf16), rows_buf(D_MLSTM, bf16),
            rows_buf(D_MLSTM, f32), rows_buf(D_MLSTM, f32),
            rows_buf(GATE_LANES, f32),
            pltpu.VMEM((3, CHUNK, GATE_LANES), f32),
            rows_buf(D_MLSTM, f32), rows_buf(d_conv, f32),
        ],
        compiler_params=pltpu.CompilerParams(
            dimension_semantics=("arbitrary",), vmem_limit_bytes=VMEM_LIMIT_BYTES),
        name="prompt_layer",
    )(x, x, p, *ws)


def _sample_in_kernel(x_ref, g_mix_ref, w_in_ref, b_in_ref, qkvo_ref, gates_ref, u_ref):
    nb, seq_len, d_model = x_ref.shape
    x = x_ref[...].reshape(nb * seq_len, d_model)
    q, k, v, o, gates, u = _in_proj(x, g_mix_ref, w_in_ref, b_in_ref)
    qkvo_ref[...] = jnp.concatenate([q, k, v, o], axis=1)
    gates_ref[...] = gates
    u_ref[...] = u


def _sample_in_call(x, g_mix, w_in, b_in):
    bs, seq_len, D = x.shape
    R = bs * seq_len
    tm = TILE_ROWS
    d_conv = (w_in.shape[1] - 4 * D_MLSTM - GATE_LANES) // 2
    sd = jax.ShapeDtypeStruct
    return pl.pallas_call(
        _sample_in_kernel,
        grid=(R // tm,),
        in_specs=[pl.BlockSpec((tm // seq_len, seq_len, D), lambda i: (i, 0, 0))]
        + _weight_specs((g_mix, w_in, b_in)),
        out_specs=(pl.BlockSpec((tm, 4 * D_MLSTM), lambda i: (i, 0)),
                   pl.BlockSpec((tm, GATE_LANES), lambda i: (i, 0)),
                   pl.BlockSpec((tm, d_conv), lambda i: (i, 0))),
        out_shape=(sd((R, 4 * D_MLSTM), f32), sd((R, GATE_LANES), f32), sd((R, d_conv), f32)),
        compiler_params=pltpu.CompilerParams(
            dimension_semantics=("arbitrary",), vmem_limit_bytes=VMEM_LIMIT_BYTES),
        name="sample_in_proj",
    )(x, g_mix, w_in, b_in)


def _sample_rec_kernel(seq_len, qkvo_ref, gates_ref, u_ref, mrep_ref, c_ref, n_ref, hist_ref,
                       g_head_ref, w_dw_ref, b_dw_ref, g_cn_ref, b_cn_ref,
                       hm_ref, cact_ref, c_out_ref, n_out_ref, mt_ref, hist_out_ref,
                       numi_s, dec_s, gk_s):
    rows = qkvo_ref.shape[0]
    nb = rows // seq_len
    shift = seq_len.bit_length() - 1
    d = D_MLSTM

    rid = lax.broadcasted_iota(jnp.int32, (rows, rows), 0)
    cid = lax.broadcasted_iota(jnp.int32, (rows, rows), 1)
    same = (rid >> shift) == (cid >> shift)
    mask = jnp.logical_and(same, cid <= rid)
    lcum16 = jnp.where(mask, 1.0, 0.0).astype(bf16)
    bsum16 = jnp.where(same, 1.0, 0.0).astype(bf16)
    plast16 = jnp.where(cid == ((rid >> shift) << shift) + (seq_len - 1), 1.0, 0.0).astype(bf16)
    lane_g = lax.broadcasted_iota(jnp.int32, (rows, GATE_LANES), 1)

    gates = gates_ref[...]
    lf, bc, gates_t, bc_t = _gate_cumsum(gates, lcum16)
    blast = _exact_left(bsum16, lf)
    mrep = mrep_ref[...]

    q = qkvo_ref[:, 0:d]
    k = qkvo_ref[:, d:2 * d]
    v = qkvo_ref[:, 2 * d:3 * d]
    o = qkvo_ref[:, 3 * d:4 * d]
    q16 = q.astype(bf16)
    k16 = k.astype(bf16)

    per_head = []
    for h in range(N_HEADS):
        cols = slice(h * DK, (h + 1) * DK)
        m_prev = mrep[:, F_LANE + h:F_LANE + h + 1]
        icol, bcol, m_t, w_intra, w_inter = _chunk_weights(gates, bc, gates_t, bc_t, h, mask, m_prev)
        vaug = _v_aug(v[:, cols])
        nd = _intra(q16[:, cols], k16[:, cols], vaug.astype(bf16), w_intra)
        per_head.append((icol, bcol, m_t, w_inter, nd))
    mt_all = _lane_pick([ph[2] for ph in per_head], lane_g)
    mt_ref[...] = mt_all
    mnew = _exact_left(plast16, mt_all)
    dec_s[...] = jnp.exp(blast + mrep - mnew)
    gvt16 = []
    for h in range(N_HEADS):
        cols = slice(h * DK, (h + 1) * DK)
        icol, bcol, _, _, _ = per_head[h]
        lanes = slice(F_LANE + h, F_LANE + h + 1)
        gcol = jnp.exp(((blast[:, lanes] - bcol) + icol) - mnew[:, lanes])
        gvt16.append((gcol * v[:, cols]).T.astype(bf16))
        gk_s[:, cols] = gcol * k[:, cols]

    rsel = lax.broadcasted_iota(jnp.int32, (rows, nb), 0)
    bsel = lax.broadcasted_iota(jnp.int32, (rows, nb), 1)
    expand16 = jnp.where((rsel >> shift) == bsel, 1.0, 0.0).astype(bf16)
    nrep = _exact_left(expand16, n_ref[...])

    grp = lax.broadcasted_iota(jnp.int32, (rows, DK), 0) >> shift
    numi_s[...] = jnp.zeros_like(numi_s)

    sub = SUBLANES
    tile_shift = (sub // seq_len).bit_length() - 1
    in_tile = lax.broadcasted_iota(jnp.int32, (sub, d), 0) >> shift

    def per_seq(b, carry):
        mb = grp == b
        r0 = pl.multiple_of((b >> tile_shift) * sub, sub)
        mine = in_tile == (b & (sub // seq_len - 1))
        q8 = qkvo_ref[pl.ds(r0, sub), 0:d]
        dec_b = dec_s[pl.ds(b * seq_len, 1), :]
        n_b = n_ref[pl.ds(b, 1), :]
        gk8 = jnp.sum(jnp.where(mine, gk_s[pl.ds(r0, sub), :], 0.0), axis=0, keepdims=True)
        n_new = []
        num_i = []
        for h in range(N_HEADS):
            cols = slice(h * DK, (h + 1) * DK)
            cb = c_ref[b, h]
            num_i.append(_dot_nt(q8[:, cols].astype(bf16), cb.astype(bf16)))
            kb = jnp.where(mb, k[:, cols], 0.0).astype(bf16)
            dec = dec_b[:, F_LANE + h:F_LANE + h + 1]
            c_out_ref[b, h] = dec * cb + _bdot(gvt16[h], kb)
            n_new.append(dec * n_b[:, cols] + gk8[:, cols])
        numi_s[pl.ds(r0, sub), :] = jnp.where(mine, jnp.concatenate(num_i, axis=1),
                                              numi_s[pl.ds(r0, sub), :])
        n_out_ref[pl.ds(b, 1), :] = jnp.concatenate(n_new, axis=1)
        return carry

    lax.fori_loop(0, nb, per_seq, 0, unroll=SEQ_UNROLL)

    hm = []
    for h in range(N_HEADS):
        cols = slice(h * DK, (h + 1) * DK)
        _, _, m_t, w_inter, nd = per_head[h]
        den_i = jnp.sum(q[:, cols] * nrep[:, cols], axis=-1, keepdims=True)
        hh = _finish(nd, w_inter, numi_s[:, cols], den_i, m_t)
        hm.append(_head_out(hh, o[:, cols], g_head_ref[:, cols]))
    hm_ref[...] = jnp.concatenate(hm, axis=1)

    rt = lax.broadcasted_iota(jnp.int32, (rows, rows), 0)
    rbm = lax.broadcasted_iota(jnp.int32, (rows, rows), 1)
    nb_shift = nb.bit_length() - 1
    to_tm16 = jnp.where(rbm == ((rt & (nb - 1)) << shift) + (rt >> nb_shift), 1.0, 0.0).astype(bf16)
    to_bm16 = jnp.where(rt == ((rbm & (nb - 1)) << shift) + (rbm >> nb_shift), 1.0, 0.0).astype(bf16)
    u_tm = _exact_left(to_tm16, u_ref[...])
    u_steps = [u_tm[s * nb:(s + 1) * nb] for s in range(seq_len)]
    c_steps = []
    for t in range(seq_len):
        acc = None
        for j in range(t, CONV_BUF):
            term = w_dw_ref[j - t:j - t + 1, :] * hist_ref[j]
            acc = term if acc is None else acc + term
        for s in range(t + 1):
            acc = acc + w_dw_ref[CONV_BUF + s - t:CONV_BUF + s - t + 1, :] * u_steps[s]
        c_steps.append(acc + b_dw_ref[...])
    cact_tm = _conv_post(jnp.concatenate(c_steps, axis=0), g_cn_ref, b_cn_ref)
    cact_ref[...] = _bdot(to_bm16, cact_tm.astype(bf16))
    hist_out_ref[0:CONV_BUF - seq_len] = hist_ref[seq_len:CONV_BUF]
    for s in range(seq_len):
        hist_out_ref[CONV_BUF - seq_len + s] = u_steps[s]


def _sample_rec_call(seq_len, qkvo, gates, u, mrep, c_state, n_state, hist,
                     g_head, w_dw, b_dw, g_cn, b_cn):
    R = qkvo.shape[0]
    nb_total = c_state.shape[0]
    gb = SAMPLE_GROUP
    rows = gb * seq_len
    d_conv = u.shape[1]
    sd = jax.ShapeDtypeStruct
    rspec = lambda w: pl.BlockSpec((rows, w), lambda i: (i, 0))
    return pl.pallas_call(
        functools.partial(_sample_rec_kernel, seq_len),
        grid=(nb_total // gb,),
        in_specs=[rspec(qkvo.shape[1]), rspec(GATE_LANES), rspec(d_conv), rspec(GATE_LANES),
                  pl.BlockSpec((gb, N_HEADS, DK, DK), lambda i: (i, 0, 0, 0)),
                  pl.BlockSpec((gb, D_MLSTM), lambda i: (i, 0)),
                  pl.BlockSpec((CONV_BUF, gb, d_conv), lambda i: (0, i, 0))]
        + _weight_specs((g_head, w_dw, b_dw, g_cn, b_cn)),
        out_specs=(rspec(D_MLSTM), rspec(d_conv),
                   pl.BlockSpec((gb, N_HEADS, DK, DK), lambda i: (i, 0, 0, 0)),
                   pl.BlockSpec((gb, D_MLSTM), lambda i: (i, 0)),
                   rspec(GATE_LANES),
                   pl.BlockSpec((CONV_BUF, gb, d_conv), lambda i: (0, i, 0))),
        out_shape=(sd((R, D_MLSTM), f32), sd((R, d_conv), f32),
                   sd(c_state.shape, f32), sd(n_state.shape, f32),
                   sd((R, GATE_LANES), f32), sd(hist.shape, f32)),
        scratch_shapes=[pltpu.VMEM((rows, D_MLSTM), f32), pltpu.VMEM((rows, GATE_LANES), f32),
                        pltpu.VMEM((rows, D_MLSTM), f32)],
        compiler_params=pltpu.CompilerParams(
            dimension_semantics=("arbitrary",), vmem_limit_bytes=VMEM_LIMIT_BYTES),
        name="sample_recurrent",
    )(qkvo, gates, u, mrep, c_state, n_state, hist, g_head, w_dw, b_dw, g_cn, b_cn)


def _sample_tail_kernel(x_ref, hm_ref, cact_ref, p_ref, w_out_ref, g_ffn_ref, w_ff1_ref, w_ff2_ref,
                        g_ple_ref, w_gate_ref, w_proj_ref, g_final_ref, y_ref):
    nb, seq_len, d_model = x_ref.shape
    rows = nb * seq_len
    y = _tail(x_ref[...].reshape(rows, d_model), hm_ref[...], cact_ref[...],
              p_ref[...].reshape(rows, p_ref.shape[2]), w_out_ref, g_ffn_ref, w_ff1_ref, w_ff2_ref,
              g_ple_ref, w_gate_ref, w_proj_ref, g_final_ref)
    y_ref[...] = y.reshape(nb, seq_len, d_model)


def _sample_tail_call(x, hm, cact, p, ws):
    bs, seq_len, D = x.shape
    R = bs * seq_len
    tm = TILE_ROWS
    rspec = lambda w: pl.BlockSpec((tm, w), lambda i: (i, 0))
    sspec = lambda w: pl.BlockSpec((tm // seq_len, seq_len, w), lambda i: (i, 0, 0))
    return pl.pallas_call(
        _sample_tail_kernel,
        grid=(R // tm,),
        in_specs=[sspec(D), rspec(hm.shape[1]), rspec(cact.shape[1]), sspec(p.shape[2])]
        + _weight_specs(ws),
        out_specs=sspec(D),
        out_shape=jax.ShapeDtypeStruct((bs, seq_len, D), f32),
        compiler_params=pltpu.CompilerParams(
            dimension_semantics=("arbitrary",), vmem_limit_bytes=VMEM_LIMIT_BYTES),
        name="sample_tail",
    )(x, hm, cact, p, *ws)


def _w_in_layout_kernel(d4, n_gate, wt_ref, o_ref):
    blk = TILE_ROWS
    rest = wt_ref.shape[0] - d4 - n_gate
    for c in range(0, d4, blk):
        o_ref[:, c:c + blk] = wt_ref[c:c + blk, :].T.astype(bf16)
    lane = lax.broadcasted_iota(jnp.int32, (wt_ref.shape[1], GATE_LANES), 1)
    gates = wt_ref[d4:d4 + GATE_LANES, :].T
    o_ref[:, d4:d4 + GATE_LANES] = jnp.where(lane < n_gate, gates, 0.0).astype(bf16)
    for c in range(0, rest, blk):
        o_ref[:, d4 + GATE_LANES + c:d4 + GATE_LANES + c + blk] = (
            wt_ref[d4 + n_gate + c:d4 + n_gate + c + blk, :].T.astype(bf16))


def _w_in_layout_call(wt, d4, n_gate):
    width, rows = wt.shape
    out_w = width - n_gate + GATE_LANES
    return pl.pallas_call(
        functools.partial(_w_in_layout_kernel, d4, n_gate),
        grid=(1,),
        in_specs=[_const_spec(wt.shape)],
        out_specs=_const_spec((rows, out_w)),
        out_shape=jax.ShapeDtypeStruct((rows, out_w), bf16),
        compiler_params=pltpu.CompilerParams(
            dimension_semantics=("arbitrary",), vmem_limit_bytes=VMEM_LIMIT_BYTES),
        name="w_in_layout",
    )(wt)


def _layer_weights(i, g_mix, w_in, b_in, g_head, w_dw, b_dw, g_cn, b_cn, w_out, g_ffn, w_ff1,
                   w_ff2, g_ple, w_ple_gate, w_ple_proj):
    row = lambda a: a[i].reshape(1, -1).astype(f32)
    d4 = 4 * D_MLSTM
    n_gate = 2 * N_HEADS
    pad = GATE_LANES - n_gate
    w_in_p = _w_in_layout_call(w_in[i].T, d4, n_gate)
    b = b_in[i]
    b_in_p = jnp.concatenate(
        [b[:d4], jnp.pad(b[d4:d4 + n_gate], (0, pad)), b[d4 + n_gate:]]).reshape(1, -1).astype(f32)
    w_dw_p = jnp.pad(w_dw[i].astype(f32), ((0, HIST_ROWS - CONV_WIDTH), (0, 0)))
    return dict(
        g_mix=row(g_mix), w_in=w_in_p, b_in=b_in_p, g_head=row(g_head), w_dw=w_dw_p,
        b_dw=row(b_dw), g_cn=row(g_cn), b_cn=row(b_cn), w_out=w_out[i].astype(bf16),
        g_ffn=row(g_ffn), w_ff1=w_ff1[i].astype(bf16), w_ff2=w_ff2[i].astype(bf16),
        g_ple=row(g_ple), w_gate=w_ple_gate[i].astype(bf16), w_proj=w_ple_proj[i].astype(bf16))


def kernel(x_prompt, x_sample, state_mlstm_C, state_mlstm_n, state_mlstm_m, cache_conv, p_prompt,
           p_sample, g_mix, w_in, b_in, g_head, w_dw, b_dw, g_cn, b_cn, w_out, g_ffn, w_ff1, w_ff2,
           g_ple, w_ple_gate, w_ple_proj, g_final):
    depth = w_in.shape[0]
    assert depth == 1, "the final norm is fused into the layer kernels"
    bs, seq_len, d_model = x_sample.shape
    assert seq_len & (seq_len - 1) == 0 and seq_len <= SUBLANES
    assert SAMPLE_GROUP & (SAMPLE_GROUP - 1) == 0 and SAMPLE_GROUP * seq_len == CHUNK
    g_fin = g_final.reshape(1, -1).astype(f32)

    i = 0
    lw = _layer_weights(i, g_mix, w_in, b_in, g_head, w_dw, b_dw, g_cn, b_cn, w_out, g_ffn,
                        w_ff1, w_ff2, g_ple, w_ple_gate, w_ple_proj)
    tail_ws = (lw["w_out"], lw["g_ffn"], lw["w_ff1"], lw["w_ff2"], lw["g_ple"], lw["w_gate"],
               lw["w_proj"], g_fin)

    prompt_ws = (lw["g_mix"], lw["w_in"], lw["b_in"], lw["g_head"], lw["w_dw"], lw["b_dw"],
                 lw["g_cn"], lw["b_cn"]) + tail_ws
    y_p, c_p, n_p, m_p, conv_p = _prompt_call(x_prompt, p_prompt[i], prompt_ws)
    m_p = m_p[:, 0, F_LANE:F_LANE + N_HEADS]

    qkvo, gates, u = _sample_in_call(x_sample, lw["g_mix"], lw["w_in"], lw["b_in"])
    m0 = jnp.pad(state_mlstm_m[i].astype(f32), ((0, 0), (F_LANE, GATE_LANES - F_LANE - N_HEADS)))
    mrep = jnp.repeat(m0, seq_len, axis=0)
    hm, cact, c_s, n_s, mt, conv_s = _sample_rec_call(
        seq_len, qkvo, gates, u, mrep, state_mlstm_C[i], state_mlstm_n[i].reshape(bs, -1),
        cache_conv[i].transpose(1, 0, 2), lw["g_head"], lw["w_dw"], lw["b_dw"], lw["g_cn"],
        lw["b_cn"])
    conv_s = conv_s.transpose(1, 0, 2)
    y_s = _sample_tail_call(x_sample, hm, cact, p_sample[i], tail_ws)
    m_s = mt[seq_len - 1::seq_len, F_LANE:F_LANE + N_HEADS]
    n_s = n_s.reshape(bs, N_HEADS, DK)

    stack = lambda a: a[None]
    return (y_p, y_s, stack(c_p), stack(n_p), stack(m_p), stack(conv_p),
            stack(c_s), stack(n_s), stack(m_s), stack(conv_s))
```

```python
import functools

import jax
import jax.numpy as jnp
from jax import lax
from jax.experimental import pallas as pl
from jax.experimental.pallas import tpu as pltpu

f32 = jnp.float32
bf16 = jnp.bfloat16

N_HEADS = 4
DK = 128
D_MLSTM = N_HEADS * DK
CONV_WIDTH = 31
CONV_BUF = CONV_WIDTH - 1
EPS = 1e-6
SUBLANES = 8
GATE_LANES = 128
F_LANE = N_HEADS
CHUNK = 128
TILE_ROWS = 256
HIST_ROWS = 32
SAMPLE_GROUP = 32
FFN_PIECES = 4
SEQ_UNROLL = 8
VMEM_LIMIT_BYTES = 56 * 1024 * 1024


def _bdot(a, b):
    return jnp.dot(a, b, preferred_element_type=f32)


def _dot_nt(a, b):
    return lax.dot_general(a, b, (((1,), (1,)), ((), ())), preferred_element_type=f32)


def _dot_tn(a, b):
    return lax.dot_general(a, b, (((0,), (0,)), ((), ())), preferred_element_type=f32)


def _rms(x, g):
    y = x * lax.rsqrt(jnp.mean(x * x, axis=-1, keepdims=True) + EPS)
    return y * g


def _ln(x):
    mu = jnp.mean(x, axis=-1, keepdims=True)
    xc = x - mu
    return xc * lax.rsqrt(jnp.mean(xc * xc, axis=-1, keepdims=True) + EPS)


def _exact_left(sel16, x):
    hi = x.astype(bf16)
    r = x - hi.astype(f32)
    mid = r.astype(bf16)
    lo = (r - mid.astype(f32)).astype(bf16)
    return _bdot(sel16, hi) + _bdot(sel16, mid) + _bdot(sel16, lo)


def _lane_pick(rows, lane_ids):
    out = jnp.zeros((rows[0].shape[0], GATE_LANES), f32)
    for h, r in enumerate(rows):
        out = jnp.where(lane_ids == F_LANE + h, r, out)
    return out


def _in_proj(x, g_mix_ref, w_in_ref, b_in_ref):
    xn = _rms(x, g_mix_ref[...]).astype(bf16)

    def proj(a, b):
        return _bdot(xn, w_in_ref[:, a:b]) + b_in_ref[:, a:b]

    d = D_MLSTM
    q = proj(0, d)
    k = proj(d, 2 * d) * (DK ** -0.5)
    v = proj(2 * d, 3 * d)
    o = proj(3 * d, 4 * d)
    gates = proj(4 * d, 4 * d + GATE_LANES)
    c0 = 4 * d + GATE_LANES
    ga = proj(c0, c0 + d)
    gg = proj(c0 + d, c0 + 2 * d)
    u = ga * jax.nn.sigmoid(gg)
    return q, k, v, o, gates, u


def _gate_cumsum(gates, lcum16):
    lf = jax.nn.log_sigmoid(gates)
    bc = _exact_left(lcum16, lf)
    return lf, bc, gates.T, bc.T


def _chunk_weights(gates, bc, gates_t, bc_t, h, mask, m_prev):
    icol = gates[:, h:h + 1]
    bcol = bc[:, F_LANE + h:F_LANE + h + 1]
    irow = gates_t[h:h + 1, :]
    brow = bc_t[F_LANE + h:F_LANE + h + 1, :]
    logw = jnp.where(mask, (bcol - brow) + irow, -jnp.inf)
    m_intra = jnp.max(logw, axis=-1, keepdims=True)
    log_inter = bcol + m_prev
    m_t = jnp.maximum(log_inter, m_intra)
    w_intra = jnp.exp(logw - m_t)
    w_inter = jnp.exp(log_inter - m_t)
    return icol, bcol, m_t, w_intra, w_inter


def _v_aug(v):
    lane = lax.broadcasted_iota(jnp.int32, v.shape, 1)
    return jnp.concatenate([v, jnp.where(lane == 0, 1.0, 0.0)], axis=1)


def _intra(q16, k16, vaug16, w_intra):
    s = _dot_nt(q16, k16)
    a = (w_intra * s).astype(bf16)
    return _bdot(a, vaug16)


def _finish(nd, w_inter, num_inter, den_inter, m_t):
    num = nd[:, :DK] + w_inter * num_inter
    den = nd[:, DK:DK + 1] + w_inter * den_inter
    return num / jnp.maximum(jnp.abs(den), jnp.exp(-m_t))


def _head_out(h, o, g_head):
    return (_ln(h) * g_head) * jax.nn.sigmoid(o)


def _conv_post(c, g_cn_ref, b_cn_ref):
    c = _ln(c) * g_cn_ref[...] + b_cn_ref[...]
    return c * jax.nn.sigmoid(c)


def _out_proj(x, hm, c, w_out_ref):
    d = D_MLSTM
    return x + (_bdot(hm.astype(bf16), w_out_ref[0:d, :]) + _bdot(c.astype(bf16), w_out_ref[d:, :]))


def _ffn_part(xn16, w_ff1_ref, w_ff2_ref, j0, j1):
    f = jnp.maximum(_bdot(xn16, w_ff1_ref[:, j0:j1]), 0.0)
    return _bdot((f * f).astype(bf16), w_ff2_ref[j0:j1, :])


def _ple_final(x, p, g_ple_ref, w_gate_ref, w_proj_ref, g_final_ref):
    gate = jax.nn.sigmoid(_bdot(_rms(x, g_ple_ref[...]).astype(bf16), w_gate_ref[...]))
    x = x + gate * _bdot(p.astype(bf16), w_proj_ref[...])
    return _rms(x, g_final_ref[...])


def _cumsum_rows(x):
    sub = SUBLANES
    pos = lax.broadcasted_iota(jnp.int32, x.shape, 0) & (sub - 1)
    sh = 1
    while sh < sub:
        x = x + jnp.where(pos >= sh, pltpu.roll(x, sh, axis=0), 0.0)
        sh *= 2
    tiles = []
    carry = None
    for r0 in range(0, x.shape[0], sub):
        tile = x[r0:r0 + sub]
        if carry is not None:
            tile = tile + carry
        carry = tile[sub - 1:sub]
        tiles.append(tile)
    return jnp.concatenate(tiles, axis=0)


def _prompt_kernel(nt, xf_ref, xb_ref, p_ref, g_mix_ref, w_in_ref, b_in_ref, g_head_ref, w_dw_ref,
                   b_dw_ref, g_cn_ref, b_cn_ref, w_out_ref, g_ffn_ref, w_ff1_ref, w_ff2_ref,
                   g_ple_ref, w_gate_ref, w_proj_ref, g_final_ref,
                   y_ref, c_out_ref, n_out_ref, m_out_ref, conv_out_ref,
                   ct_s, m_s, ext_s, hm_s, cact_s, acc_s, xn_s, q_s, k_s, v_s, o_s, g_s, gc_s,
                   hh_s, cc_s):
    s = pl.program_id(0)
    n_tiles = pl.num_programs(0) - 1
    t = lax.rem(lax.rem(s, n_tiles), nt)
    tm = xf_ref.shape[0]
    off = HIST_ROWS - CONV_BUF

    @pl.when(s == 0)
    def _():
        hm_s[...] = jnp.zeros_like(hm_s)
        cact_s[...] = jnp.zeros_like(cact_s)

    @pl.when(t == 0)
    def _():
        ct_s[...] = jnp.zeros_like(ct_s)
        m_s[...] = jnp.zeros_like(m_s)
        ext_s[0:HIST_ROWS, :] = jnp.zeros((HIST_ROWS, ext_s.shape[1]), f32)

    x1 = _out_proj(xb_ref[...], hm_s[...], cact_s[...], w_out_ref)
    acc_s[...] = x1
    xn_s[...] = _rms(x1, g_ffn_ref[...]).astype(bf16)
    q, k, v, o, gates, u = _in_proj(xf_ref[...], g_mix_ref, w_in_ref, b_in_ref)
    q_s[...] = q.astype(bf16)
    k_s[...] = k.astype(bf16)
    v_s[...] = v
    o_s[...] = o
    g_s[...] = gates
    ext_s[HIST_ROWS:HIST_ROWS + tm, :] = u

    rid = lax.broadcasted_iota(jnp.int32, (CHUNK, CHUNK), 0)
    cid = lax.broadcasted_iota(jnp.int32, (CHUNK, CHUNK), 1)
    causal = cid <= rid
    lane_row = lax.broadcasted_iota(jnp.int32, (1, GATE_LANES), 1)

    def ffn_piece(j0, j1):
        acc_s[...] += _ffn_part(xn_s[...], w_ff1_ref, w_ff2_ref, j0, j1)

    def conv_block(r0, rb):
        sub = SUBLANES
        acc = None
        for ph in range(sub):
            n_rows = rb if ph == 0 else rb + sub
            grp = None
            for j in range(ph, off + CONV_WIDTH, sub):
                if j < off:
                    continue
                term = w_dw_ref[j - off:j - off + 1, :] * ext_s[r0 + j - ph:r0 + j - ph + n_rows, :]
                grp = term if grp is None else grp + term
            part = grp if ph == 0 else grp[ph:ph + rb]
            acc = part if acc is None else acc + part
        cc_s[r0:r0 + rb, :] = acc + b_dw_ref[...]

    def mlstm_piece(ci, h):
        rows = slice(ci * CHUNK, (ci + 1) * CHUNK)
        cols = slice(h * DK, (h + 1) * DK)
        gates_c = g_s[rows, :]
        if h == 0:
            bc = _cumsum_rows(jax.nn.log_sigmoid(gates_c))
            gc_s[0] = bc
            gc_s[1] = gates_c.T
            gc_s[2] = bc.T
        m_row = m_s[...]
        m_prev = m_row[:, F_LANE + h:F_LANE + h + 1]
        icol, bcol, m_t, w_intra, w_inter = _chunk_weights(
            gates_c, gc_s[0], gc_s[1], gc_s[2], h, causal, m_prev)
        q16 = q_s[rows, cols]
        k16 = k_s[rows, cols]
        vaug = _v_aug(v_s[rows, cols])
        nd = _intra(q16, k16, vaug.astype(bf16), w_intra)
        ct = ct_s[h]
        qc = _bdot(q16, ct.astype(bf16))
        hh_s[rows, cols] = _finish(nd, w_inter, qc[:, :DK], qc[:, DK:DK + 1], m_t)
        m_new = m_t[CHUNK - 1:CHUNK]
        b_last = bcol[CHUNK - 1:CHUNK]
        gcol = jnp.exp(((b_last - bcol) + icol) - m_new)
        decay = jnp.exp(b_last + m_prev - m_new)
        ct_s[h] = decay * ct + _dot_tn(k16, (gcol * vaug).astype(bf16))
        m_s[...] = jnp.where(lane_row == F_LANE + h, m_new, m_row)

    n_piece = (tm // CHUNK) * N_HEADS
    d_ff = w_ff1_ref.shape[1]
    fstep, rb = d_ff // FFN_PIECES, tm // n_piece
    for j in range(n_piece):
        if j % (n_piece // FFN_PIECES) == 0:
            jf = j // (n_piece // FFN_PIECES)
            ffn_piece(jf * fstep, (jf + 1) * fstep)
        conv_block(j * rb, rb)
        mlstm_piece(j // N_HEADS, j % N_HEADS)

    y_ref[...] = _ple_final(acc_s[...], p_ref[...], g_ple_ref, w_gate_ref, w_proj_ref,
                            g_final_ref)
    hm = jnp.concatenate(
        [_head_out(hh_s[:, h * DK:(h + 1) * DK], o_s[:, h * DK:(h + 1) * DK],
                   g_head_ref[:, h * DK:(h + 1) * DK]) for h in range(N_HEADS)], axis=1)
    hm_s[...] = hm.astype(bf16)
    cact_s[...] = _conv_post(cc_s[...], g_cn_ref, b_cn_ref).astype(bf16)
    ext_s[0:HIST_ROWS, :] = ext_s[tm:tm + HIST_ROWS, :]

    @pl.when(jnp.logical_and(t == nt - 1, s < n_tiles))
    def _():
        for h in range(N_HEADS):
            ctt = ct_s[h].T
            c_out_ref[h] = ctt[:DK]
            n_out_ref[h:h + 1, :] = ctt[DK:DK + 1]
        m_out_ref[...] = jnp.broadcast_to(m_s[...], m_out_ref.shape)
        conv_out_ref[...] = ext_s[off:HIST_ROWS, :]


def _const_spec(shape):
    return pl.BlockSpec(shape, lambda *_: (0,) * len(shape), pipeline_mode=pl.Buffered(1))


def _weight_specs(ws):
    return [_const_spec(w.shape) for w in ws]


def _prompt_call(x, p, ws):
    B, T, D = x.shape
    tm = TILE_ROWS
    nt = T // tm
    n_tiles = B * nt
    d_conv = ws[4].shape[1]
    front = lambda s: (lax.rem(s, n_tiles) // nt, lax.rem(lax.rem(s, n_tiles), nt), 0)
    back = lambda s: (jnp.maximum(s - 1, 0) // nt, lax.rem(jnp.maximum(s - 1, 0), nt), 0)
    state = lambda s: (jnp.minimum(s, n_tiles - 1) // nt, 0, 0)
    sd = jax.ShapeDtypeStruct
    rows_buf = lambda w, dt: pltpu.VMEM((tm, w), dt)
    return pl.pallas_call(
        functools.partial(_prompt_kernel, nt),
        grid=(n_tiles + 1,),
        in_specs=[pl.BlockSpec((None, tm, D), front), pl.BlockSpec((None, tm, D), back),
                  pl.BlockSpec((None, tm, p.shape[2]), back)] + _weight_specs(ws),
        out_specs=(
            pl.BlockSpec((None, tm, D), back),
            pl.BlockSpec((None, N_HEADS, DK, DK), lambda s: state(s) + (0,)),
            pl.BlockSpec((None, N_HEADS, DK), state),
            pl.BlockSpec((None, SUBLANES, GATE_LANES), state),
            pl.BlockSpec((None, CONV_BUF, d_conv), state),
        ),
        out_shape=(
            sd((B, T, D), f32),
            sd((B, N_HEADS, DK, DK), f32),
            sd((B, N_HEADS, DK), f32),
            sd((B, SUBLANES, GATE_LANES), f32),
            sd((B, CONV_BUF, d_conv), f32),
        ),
        scratch_shapes=[
            pltpu.VMEM((N_HEADS, DK, 2 * DK), f32),
            pltpu.VMEM((1, GATE_LANES), f32),
            pltpu.VMEM((HIST_ROWS + tm, d_conv), f32),
            rows_buf(D_MLSTM, bf16), rows_buf(d_conv, bf16),
            rows_buf(D, f32), rows_buf(D, bf16),
            rows_buf(D_MLSTM, bf16), rows_buf(D_MLSTM, bf16),
            rows_buf(D_MLSTM, f32), rows_buf(D_MLSTM, f32),
            rows_buf(GATE_LANES, f32),
            pltpu.VMEM((3, CHUNK, GATE_LANES), f32),
            rows_buf(D_MLSTM, f32), rows_buf(d_conv, f32),
        ],
        compiler_params=pltpu.CompilerParams(
            dimension_semantics=("arbitrary",), vmem_limit_bytes=VMEM_LIMIT_BYTES),
        name="prompt_layer",
    )(x, x, p, *ws)


def _sample_in_kernel(x_ref, g_mix_ref, w_in_ref, b_in_ref, qkvo_ref, gates_ref, u_ref):
    nb, seq_len, d_model = x_ref.shape
    x = x_ref[...].reshape(nb * seq_len, d_model)
    q, k, v, o, gates, u = _in_proj(x, g_mix_ref, w_in_ref, b_in_ref)
    qkvo_ref[...] = jnp.concatenate([q, k, v, o], axis=1)
    gates_ref[...] = gates
    u_ref[...] = u


def _sample_in_call(x, g_mix, w_in, b_in):
    bs, seq_len, D = x.shape
    R = bs * seq_len
    tm = TILE_ROWS
    d_conv = (w_in.shape[1] - 4 * D_MLSTM - GATE_LANES) // 2
    sd = jax.ShapeDtypeStruct
    return pl.pallas_call(
        _sample_in_kernel,
        grid=(R // tm,),
        in_specs=[pl.BlockSpec((tm // seq_len, seq_len, D), lambda i: (i, 0, 0))]
        + _weight_specs((g_mix, w_in, b_in)),
        out_specs=(pl.BlockSpec((tm, 4 * D_MLSTM), lambda i: (i, 0)),
                   pl.BlockSpec((tm, GATE_LANES), lambda i: (i, 0)),
                   pl.BlockSpec((tm, d_conv), lambda i: (i, 0))),
        out_shape=(sd((R, 4 * D_MLSTM), f32), sd((R, GATE_LANES), f32), sd((R, d_conv), f32)),
        compiler_params=pltpu.CompilerParams(
            dimension_semantics=("arbitrary",), vmem_limit_bytes=VMEM_LIMIT_BYTES),
        name="sample_in_proj",
    )(x, g_mix, w_in, b_in)


def _sample_rec_kernel(seq_len, qkvo_ref, gates_ref, u_ref, mrep_ref, c_ref, n_ref, hist_ref,
                       g_head_ref, w_dw_ref, b_dw_ref, g_cn_ref, b_cn_ref,
                       hm_ref, cact_ref, c_out_ref, n_out_ref, mt_ref, hist_out_ref,
                       numi_s, dec_s, gk_s):
    rows = qkvo_ref.shape[0]
    nb = rows // seq_len
    shift = seq_len.bit_length() - 1
    d = D_MLSTM

    rid = lax.broadcasted_iota(jnp.int32, (rows, rows), 0)
    cid = lax.broadcasted_iota(jnp.int32, (rows, rows), 1)
    same = (rid >> shift) == (cid >> shift)
    mask = jnp.logical_and(same, cid <= rid)
    lcum16 = jnp.where(mask, 1.0, 0.0).astype(bf16)
    bsum16 = jnp.where(same, 1.0, 0.0).astype(bf16)
    plast16 = jnp.where(cid == ((rid >> shift) << shift) + (seq_len - 1), 1.0, 0.0).astype(bf16)
    lane_g = lax.broadcasted_iota(jnp.int32, (rows, GATE_LANES), 1)

    gates = gates_ref[...]
    lf, bc, gates_t, bc_t = _gate_cumsum(gates, lcum16)
    blast = _exact_left(bsum16, lf)
    mrep = mrep_ref[...]

    q = qkvo_ref[:, 0:d]
    k = qkvo_ref[:, d:2 * d]
    v = qkvo_ref[:, 2 * d:3 * d]
    o = qkvo_ref[:, 3 * d:4 * d]
    q16 = q.astype(bf16)
    k16 = k.astype(bf16)

    per_head = []
    for h in range(N_HEADS):
        cols = slice(h * DK, (h + 1) * DK)
        m_prev = mrep[:, F_LANE + h:F_LANE + h + 1]
        icol, bcol, m_t, w_intra, w_inter = _chunk_weights(gates, bc, gates_t, bc_t, h, mask, m_prev)
        vaug = _v_aug(v[:, cols])
        nd = _intra(q16[:, cols], k16[:, cols], vaug.astype(bf16), w_intra)
        per_head.append((icol, bcol, m_t, w_inter, nd))
    mt_all = _lane_pick([ph[2] for ph in per_head], lane_g)
    mt_ref[...] = mt_all
    mnew = _exact_left(plast16, mt_all)
    dec_s[...] = jnp.exp(blast + mrep - mnew)
    gvt16 = []
    for h in range(N_HEADS):
        cols = slice(h * DK, (h + 1) * DK)
        icol, bcol, _, _, _ = per_head[h]
        lanes = slice(F_LANE + h, F_LANE + h + 1)
        gcol = jnp.exp(((blast[:, lanes] - bcol) + icol) - mnew[:, lanes])
        gvt16.append((gcol * v[:, cols]).T.astype(bf16))
        gk_s[:, cols] = gcol * k[:, cols]

    rsel = lax.broadcasted_iota(jnp.int32, (rows, nb), 0)
    bsel = lax.broadcasted_iota(jnp.int32, (rows, nb), 1)
    expand16 = jnp.where((rsel >> shift) == bsel, 1.0, 0.0).astype(bf16)
    nrep = _exact_left(expand16, n_ref[...])

    grp = lax.broadcasted_iota(jnp.int32, (rows, DK), 0) >> shift
    numi_s[...] = jnp.zeros_like(numi_s)

    sub = SUBLANES
    tile_shift = (sub // seq_len).bit_length() - 1
    in_tile = lax.broadcasted_iota(jnp.int32, (sub, d), 0) >> shift

    def per_seq(b, carry):
        mb = grp == b
        r0 = pl.multiple_of((b >> tile_shift) * sub, sub)
        mine = in_tile == (b & (sub // seq_len - 1))
        q8 = qkvo_ref[pl.ds(r0, sub), 0:d]
        dec_b = dec_s[pl.ds(b * seq_len, 1), :]
        n_b = n_ref[pl.ds(b, 1), :]
        gk8 = jnp.sum(jnp.where(mine, gk_s[pl.ds(r0, sub), :], 0.0), axis=0, keepdims=True)
        n_new = []
        num_i = []
        for h in range(N_HEADS):
            cols = slice(h * DK, (h + 1) * DK)
            cb = c_ref[b, h]
            num_i.append(_dot_nt(q8[:, cols].astype(bf16), cb.astype(bf16)))
            kb = jnp.where(mb, k[:, cols], 0.0).astype(bf16)
            dec = dec_b[:, F_LANE + h:F_LANE + h + 1]
            c_out_ref[b, h] = dec * cb + _bdot(gvt16[h], kb)
            n_new.append(dec * n_b[:, cols] + gk8[:, cols])
        numi_s[pl.ds(r0, sub), :] = jnp.where(mine, jnp.concatenate(num_i, axis=1),
                                              numi_s[pl.ds(r0, sub), :])
        n_out_ref[pl.ds(b, 1), :] = jnp.concatenate(n_new, axis=1)
        return carry

    lax.fori_loop(0, nb, per_seq, 0, unroll=SEQ_UNROLL)

    hm = []
    for h in range(N_HEADS):
        cols = slice(h * DK, (h + 1) * DK)
        _, _, m_t, w_inter, nd = per_head[h]
        den_i = jnp.sum(q[:, cols] * nrep[:, cols], axis=-1, keepdims=True)
        hh = _finish(nd, w_inter, numi_s[:, cols], den_i, m_t)
        hm.append(_head_out(hh, o[:, cols], g_head_ref[:, cols]))
    hm_ref[...] = jnp.concatenate(hm, axis=1)

    rt = lax.broadcasted_iota(jnp.int32, (rows, rows), 0)
    rbm = lax.broadcasted_iota(jnp.int32, (rows, rows), 1)
    nb_shift = nb.bit_length() - 1
    to_tm16 = jnp.where(rbm == ((rt & (nb - 1)) << shift) + (rt >> nb_shift), 1.0, 0.0).astype(bf16)
    to_bm16 = jnp.where(rt == ((rbm & (nb - 1)) << shift) + (rbm >> nb_shift), 1.0, 0.0).astype(bf16)
    u_tm = _exact_left(to_tm16, u_ref[...])
    u_steps = [u_tm[s * nb:(s + 1) * nb] for s in range(seq_len)]
    c_steps = []
    for t in range(seq_len):
        acc = None
        for j in range(t, CONV_BUF):
            term = w_dw_ref[j - t:j - t + 1, :] * hist_ref[j]
            acc = term if acc is None else acc + term
        for s in range(t + 1):
            acc = acc + w_dw_ref[CONV_BUF + s - t:CONV_BUF + s - t + 1, :] * u_steps[s]
        c_steps.append(acc + b_dw_ref[...])
    cact_tm = _conv_post(jnp.concatenate(c_steps, axis=0), g_cn_ref, b_cn_ref)
    cact_ref[...] = _bdot(to_bm16, cact_tm.astype(bf16))
    hist_out_ref[0:CONV_BUF - seq_len] = hist_ref[seq_len:CONV_BUF]
    for s in range(seq_len):
        hist_out_ref[CONV_BUF - seq_len + s] = u_steps[s]


def _sample_rec_call(seq_len, qkvo, gates, u, mrep, c_state, n_state, hist,
                     g_head, w_dw, b_dw, g_cn, b_cn):
    R = qkvo.shape[0]
    nb_total = c_state.shape[0]
    gb = SAMPLE_GROUP
    rows = gb * seq_len
    d_conv = u.shape[1]
    sd = jax.ShapeDtypeStruct
    rspec = lambda w: pl.BlockSpec((rows, w), lambda i: (i, 0))
    return pl.pallas_call(
        functools.partial(_sample_rec_kernel, seq_len),
        grid=(nb_total // gb,),
        in_specs=[rspec(qkvo.shape[1]), rspec(GATE_LANES), rspec(d_conv), rspec(GATE_LANES),
                  pl.BlockSpec((gb, N_HEADS, DK, DK), lambda i: (i, 0, 0, 0)),
                  pl.BlockSpec((gb, D_MLSTM), lambda i: (i, 0)),
                  pl.BlockSpec((CONV_BUF, gb, d_conv), lambda i: (0, i, 0))]
        + _weight_specs((g_head, w_dw, b_dw, g_cn, b_cn)),
        out_specs=(rspec(D_MLSTM), rspec(d_conv),
                   pl.BlockSpec((gb, N_HEADS, DK, DK), lambda i: (i, 0, 0, 0)),
                   pl.BlockSpec((gb, D_MLSTM), lambda i: (i, 0)),
                   rspec(GATE_LANES),
                   pl.BlockSpec((CONV_BUF, gb, d_conv), lambda i: (0, i, 0))),
        out_shape=(sd((R, D_MLSTM), f32), sd((R, d_conv), f32),
                   sd(c_state.shape, f32), sd(n_state.shape, f32),
                   sd((R, GATE_LANES), f32), sd(hist.shape, f32)),
        scratch_shapes=[pltpu.VMEM((rows, D_MLSTM), f32), pltpu.VMEM((rows, GATE_LANES), f32),
                        pltpu.VMEM((rows, D_MLSTM), f32)],
        compiler_params=pltpu.CompilerParams(
            dimension_semantics=("arbitrary",), vmem_limit_bytes=VMEM_LIMIT_BYTES),
        name="sample_recurrent",
    )(qkvo, gates, u, mrep, c_state, n_state, hist, g_head, w_dw, b_dw, g_cn, b_cn)


def _sample_tail_kernel(x_ref, hm_ref, cact_ref, p_ref, w_out_ref, g_ffn_ref, w_ff1_ref, w_ff2_ref,
                        g_ple_ref, w_gate_ref, w_proj_ref, g_final_ref, y_ref, acc_s, xn_s):
    j = pl.program_id(0)
    nb, seq_len, d_model = x_ref.shape
    rows = nb * seq_len

    @pl.when(j == 0)
    def _():
        x1 = _out_proj(x_ref[...].reshape(rows, d_model), hm_ref[...], cact_ref[...], w_out_ref)
        acc_s[...] = x1
        xn_s[...] = _rms(x1, g_ffn_ref[...]).astype(bf16)

    step = w_ff1_ref.shape[1]
    acc_s[...] += _ffn_part(xn_s[...], w_ff1_ref, w_ff2_ref, 0, step)

    @pl.when(j == pl.num_programs(0) - 1)
    def _():
        y = _ple_final(acc_s[...], p_ref[...].reshape(rows, p_ref.shape[2]), g_ple_ref, w_gate_ref,
                       w_proj_ref, g_final_ref)
        y_ref[...] = y.reshape(nb, seq_len, d_model)


def _sample_tail_call(x, hm, cact, p, ws):
    bs, seq_len, D = x.shape
    R = bs * seq_len
    w_out, g_ffn, w_ff1, w_ff2, g_ple, w_gate, w_proj, g_fin = ws
    d_ff = w_ff1.shape[1]
    step = d_ff // FFN_PIECES
    return pl.pallas_call(
        _sample_tail_kernel,
        grid=(FFN_PIECES,),
        in_specs=[_const_spec(x.shape), _const_spec(hm.shape), _const_spec(cact.shape),
                  _const_spec(p.shape), _const_spec(w_out.shape), _const_spec(g_ffn.shape),
                  pl.BlockSpec((w_ff1.shape[0], step), lambda j: (0, j)),
                  pl.BlockSpec((step, w_ff2.shape[1]), lambda j: (j, 0)),
                  _const_spec(g_ple.shape), _const_spec(w_gate.shape), _const_spec(w_proj.shape),
                  _const_spec(g_fin.shape)],
        out_specs=_const_spec(x.shape),
        out_shape=jax.ShapeDtypeStruct(x.shape, f32),
        scratch_shapes=[pltpu.VMEM((R, D), f32), pltpu.VMEM((R, D), bf16)],
        compiler_params=pltpu.CompilerParams(
            dimension_semantics=("arbitrary",), vmem_limit_bytes=VMEM_LIMIT_BYTES),
        name="sample_tail",
    )(x, hm, cact, p, *ws)


def _w_in_layout_kernel(d4, n_gate, wt_ref, o_ref):
    blk = TILE_ROWS
    rest = wt_ref.shape[0] - d4 - n_gate
    for c in range(0, d4, blk):
        o_ref[:, c:c + blk] = wt_ref[c:c + blk, :].T.astype(bf16)
    lane = lax.broadcasted_iota(jnp.int32, (wt_ref.shape[1], GATE_LANES), 1)
    gates = wt_ref[d4:d4 + GATE_LANES, :].T
    o_ref[:, d4:d4 + GATE_LANES] = jnp.where(lane < n_gate, gates, 0.0).astype(bf16)
    for c in range(0, rest, blk):
        o_ref[:, d4 + GATE_LANES + c:d4 + GATE_LANES + c + blk] = (
            wt_ref[d4 + n_gate + c:d4 + n_gate + c + blk, :].T.astype(bf16))


def _w_in_layout_call(wt, d4, n_gate):
    width, rows = wt.shape
    out_w = width - n_gate + GATE_LANES
    return pl.pallas_call(
        functools.partial(_w_in_layout_kernel, d4, n_gate),
        grid=(1,),
        in_specs=[_const_spec(wt.shape)],
        out_specs=_const_spec((rows, out_w)),
        out_shape=jax.ShapeDtypeStruct((rows, out_w), bf16),
        compiler_params=pltpu.CompilerParams(
            dimension_semantics=("arbitrary",), vmem_limit_bytes=VMEM_LIMIT_BYTES),
        name="w_in_layout",
    )(wt)


def _layer_weights(i, g_mix, w_in, b_in, g_head, w_dw, b_dw, g_cn, b_cn, w_out, g_ffn, w_ff1,
                   w_ff2, g_ple, w_ple_gate, w_ple_proj):
    row = lambda a: a[i].reshape(1, -1).astype(f32)
    d4 = 4 * D_MLSTM
    n_gate = 2 * N_HEADS
    pad = GATE_LANES - n_gate
    w_in_p = _w_in_layout_call(w_in[i].T, d4, n_gate)
    b = b_in[i]
    b_in_p = jnp.concatenate(
        [b[:d4], jnp.pad(b[d4:d4 + n_gate], (0, pad)), b[d4 + n_gate:]]).reshape(1, -1).astype(f32)
    w_dw_p = jnp.pad(w_dw[i].astype(f32), ((0, HIST_ROWS - CONV_WIDTH), (0, 0)))
    return dict(
        g_mix=row(g_mix), w_in=w_in_p, b_in=b_in_p, g_head=row(g_head), w_dw=w_dw_p,
        b_dw=row(b_dw), g_cn=row(g_cn), b_cn=row(b_cn), w_out=w_out[i].astype(bf16),
        g_ffn=row(g_ffn), w_ff1=w_ff1[i].astype(bf16), w_ff2=w_ff2[i].astype(bf16),
        g_ple=row(g_ple), w_gate=w_ple_gate[i].astype(bf16), w_proj=w_ple_proj[i].astype(bf16))


def kernel(x_prompt, x_sample, state_mlstm_C, state_mlstm_n, state_mlstm_m, cache_conv, p_prompt,
           p_sample, g_mix, w_in, b_in, g_head, w_dw, b_dw, g_cn, b_cn, w_out, g_ffn, w_ff1, w_ff2,
           g_ple, w_ple_gate, w_ple_proj, g_final):
    depth = w_in.shape[0]
    assert depth == 1, "the final norm is fused into the layer kernels"
    bs, seq_len, d_model = x_sample.shape
    assert seq_len & (seq_len - 1) == 0 and seq_len <= SUBLANES
    assert SAMPLE_GROUP & (SAMPLE_GROUP - 1) == 0 and SAMPLE_GROUP * seq_len == CHUNK
    g_fin = g_final.reshape(1, -1).astype(f32)

    i = 0
    lw = _layer_weights(i, g_mix, w_in, b_in, g_head, w_dw, b_dw, g_cn, b_cn, w_out, g_ffn,
                        w_ff1, w_ff2, g_ple, w_ple_gate, w_ple_proj)
    tail_ws = (lw["w_out"], lw["g_ffn"], lw["w_ff1"], lw["w_ff2"], lw["g_ple"], lw["w_gate"],
               lw["w_proj"], g_fin)

    prompt_ws = (lw["g_mix"], lw["w_in"], lw["b_in"], lw["g_head"], lw["w_dw"], lw["b_dw"],
                 lw["g_cn"], lw["b_cn"]) + tail_ws
    y_p, c_p, n_p, m_p, conv_p = _prompt_call(x_prompt, p_prompt[i], prompt_ws)
    m_p = m_p[:, 0, F_LANE:F_LANE + N_HEADS]

    qkvo, gates, u = _sample_in_call(x_sample, lw["g_mix"], lw["w_in"], lw["b_in"])
    m0 = jnp.pad(state_mlstm_m[i].astype(f32), ((0, 0), (F_LANE, GATE_LANES - F_LANE - N_HEADS)))
    mrep = jnp.repeat(m0, seq_len, axis=0)
    hm, cact, c_s, n_s, mt, conv_s = _sample_rec_call(
        seq_len, qkvo, gates, u, mrep, state_mlstm_C[i], state_mlstm_n[i].reshape(bs, -1),
        cache_conv[i].transpose(1, 0, 2), lw["g_head"], lw["w_dw"], lw["b_dw"], lw["g_cn"],
        lw["b_cn"])
    conv_s = conv_s.transpose(1, 0, 2)
    y_s = _sample_tail_call(x_sample, hm, cact, p_sample[i], tail_ws)
    m_s = mt[seq_len - 1::seq_len, F_LANE:F_LANE + N_HEADS]
    n_s = n_s.reshape(bs, N_HEADS, DK)

    stack = lambda a: a[None]
    return (y_p, y_s, stack(c_p), stack(n_p), stack(m_p), stack(conv_p),
            stack(c_s), stack(n_s), stack(m_s), stack(conv_s))
```

```python
import functools

import jax
import jax.numpy as jnp
from jax import lax
from jax.experimental import pallas as pl
from jax.experimental.pallas import tpu as pltpu

f32 = jnp.float32
bf16 = jnp.bfloat16

N_HEADS = 4
DK = 128
D_MLSTM = N_HEADS * DK
CONV_WIDTH = 31
CONV_BUF = CONV_WIDTH - 1
EPS = 1e-6
SUBLANES = 8
GATE_LANES = 128
F_LANE = N_HEADS
CHUNK = 128
TILE_ROWS = 256
HIST_ROWS = 32
SAMPLE_GROUP = 32
FFN_PIECES = 4
SEQ_UNROLL = 8
VMEM_LIMIT_BYTES = 56 * 1024 * 1024


def _bdot(a, b):
    return jnp.dot(a, b, preferred_element_type=f32)


def _dot_nt(a, b):
    return lax.dot_general(a, b, (((1,), (1,)), ((), ())), preferred_element_type=f32)


def _dot_tn(a, b):
    return lax.dot_general(a, b, (((0,), (0,)), ((), ())), preferred_element_type=f32)


def _rms(x, g):
    y = x * lax.rsqrt(jnp.mean(x * x, axis=-1, keepdims=True) + EPS)
    return y * g


def _ln(x):
    mu = jnp.mean(x, axis=-1, keepdims=True)
    xc = x - mu
    return xc * lax.rsqrt(jnp.mean(xc * xc, axis=-1, keepdims=True) + EPS)


def _exact_left(sel16, x):
    hi = x.astype(bf16)
    r = x - hi.astype(f32)
    mid = r.astype(bf16)
    lo = (r - mid.astype(f32)).astype(bf16)
    return _bdot(sel16, hi) + _bdot(sel16, mid) + _bdot(sel16, lo)


def _lane_pick(rows, lane_ids):
    out = jnp.zeros((rows[0].shape[0], GATE_LANES), f32)
    for h, r in enumerate(rows):
        out = jnp.where(lane_ids == F_LANE + h, r, out)
    return out


def _in_proj(x, g_mix_ref, w_in_ref, b_in_ref):
    xn = _rms(x, g_mix_ref[...]).astype(bf16)

    def proj(a, b):
        return _bdot(xn, w_in_ref[:, a:b]) + b_in_ref[:, a:b]

    d = D_MLSTM
    q = proj(0, d)
    k = proj(d, 2 * d) * (DK ** -0.5)
    v = proj(2 * d, 3 * d)
    o = proj(3 * d, 4 * d)
    gates = proj(4 * d, 4 * d + GATE_LANES)
    c0 = 4 * d + GATE_LANES
    ga = proj(c0, c0 + d)
    gg = proj(c0 + d, c0 + 2 * d)
    u = ga * jax.nn.sigmoid(gg)
    return q, k, v, o, gates, u


def _gate_cumsum(gates, lcum16):
    lf = jax.nn.log_sigmoid(gates)
    bc = _exact_left(lcum16, lf)
    return lf, bc, gates.T, bc.T


def _chunk_weights(gates, bc, gates_t, bc_t, h, mask, m_prev):
    icol = gates[:, h:h + 1]
    bcol = bc[:, F_LANE + h:F_LANE + h + 1]
    irow = gates_t[h:h + 1, :]
    brow = bc_t[F_LANE + h:F_LANE + h + 1, :]
    logw = jnp.where(mask, (bcol - brow) + irow, -jnp.inf)
    m_intra = jnp.max(logw, axis=-1, keepdims=True)
    log_inter = bcol + m_prev
    m_t = jnp.maximum(log_inter, m_intra)
    w_intra = jnp.exp(logw - m_t)
    w_inter = jnp.exp(log_inter - m_t)
    return icol, bcol, m_t, w_intra, w_inter


def _v_aug(v):
    lane = lax.broadcasted_iota(jnp.int32, v.shape, 1)
    return jnp.concatenate([v, jnp.where(lane == 0, 1.0, 0.0)], axis=1)


def _intra(q16, k16, vaug16, w_intra):
    s = _dot_nt(q16, k16)
    a = (w_intra * s).astype(bf16)
    return _bdot(a, vaug16)


def _finish(nd, w_inter, num_inter, den_inter, m_t):
    num = nd[:, :DK] + w_inter * num_inter
    den = nd[:, DK:DK + 1] + w_inter * den_inter
    return num / jnp.maximum(jnp.abs(den), jnp.exp(-m_t))


def _head_out(h, o, g_head):
    return (_ln(h) * g_head) * jax.nn.sigmoid(o)


def _conv_post(c, g_cn_ref, b_cn_ref):
    c = _ln(c) * g_cn_ref[...] + b_cn_ref[...]
    return c * jax.nn.sigmoid(c)


def _out_proj(x, hm, c, w_out_ref):
    d = D_MLSTM
    return x + (_bdot(hm.astype(bf16), w_out_ref[0:d, :]) + _bdot(c.astype(bf16), w_out_ref[d:, :]))


def _ffn_part(xn16, w_ff1_ref, w_ff2_ref, j0, j1):
    f = jnp.maximum(_bdot(xn16, w_ff1_ref[:, j0:j1]), 0.0)
    return _bdot((f * f).astype(bf16), w_ff2_ref[j0:j1, :])


def _ple_final(x, p, g_ple_ref, w_gate_ref, w_proj_ref, g_final_ref):
    gate = jax.nn.sigmoid(_bdot(_rms(x, g_ple_ref[...]).astype(bf16), w_gate_ref[...]))
    x = x + gate * _bdot(p.astype(bf16), w_proj_ref[...])
    return _rms(x, g_final_ref[...])


def _cumsum_rows(x):
    sub = SUBLANES
    pos = lax.broadcasted_iota(jnp.int32, x.shape, 0) & (sub - 1)
    sh = 1
    while sh < sub:
        x = x + jnp.where(pos >= sh, pltpu.roll(x, sh, axis=0), 0.0)
        sh *= 2
    tiles = []
    carry = None
    for r0 in range(0, x.shape[0], sub):
        tile = x[r0:r0 + sub]
        if carry is not None:
            tile = tile + carry
        carry = tile[sub - 1:sub]
        tiles.append(tile)
    return jnp.concatenate(tiles, axis=0)


def _prompt_kernel(nt, xf_ref, xb_ref, p_ref, g_mix_ref, w_in_ref, b_in_ref, g_head_ref, w_dw_ref,
                   b_dw_ref, g_cn_ref, b_cn_ref, w_out_ref, g_ffn_ref, w_ff1_ref, w_ff2_ref,
                   g_ple_ref, w_gate_ref, w_proj_ref, g_final_ref,
                   y_ref, c_out_ref, n_out_ref, m_out_ref, conv_out_ref,
                   ct_s, m_s, ext_s, hm_s, cact_s, acc_s, xn_s, q_s, k_s, v_s, o_s, g_s, gc_s,
                   hh_s, cc_s):
    s = pl.program_id(0)
    n_tiles = pl.num_programs(0) - 1
    t = lax.rem(lax.rem(s, n_tiles), nt)
    tm = xf_ref.shape[0]
    off = HIST_ROWS - CONV_BUF

    @pl.when(s == 0)
    def _():
        hm_s[...] = jnp.zeros_like(hm_s)
        cact_s[...] = jnp.zeros_like(cact_s)

    @pl.when(t == 0)
    def _():
        ct_s[...] = jnp.zeros_like(ct_s)
        m_s[...] = jnp.zeros_like(m_s)
        ext_s[0:HIST_ROWS, :] = jnp.zeros((HIST_ROWS, ext_s.shape[1]), f32)

    x1 = _out_proj(xb_ref[...], hm_s[...], cact_s[...], w_out_ref)
    acc_s[...] = x1
    xn_s[...] = _rms(x1, g_ffn_ref[...]).astype(bf16)
    q, k, v, o, gates, u = _in_proj(xf_ref[...], g_mix_ref, w_in_ref, b_in_ref)
    q_s[...] = q.astype(bf16)
    k_s[...] = k.astype(bf16)
    v_s[...] = v
    o_s[...] = o
    g_s[...] = gates
    ext_s[HIST_ROWS:HIST_ROWS + tm, :] = u

    rid = lax.broadcasted_iota(jnp.int32, (CHUNK, CHUNK), 0)
    cid = lax.broadcasted_iota(jnp.int32, (CHUNK, CHUNK), 1)
    causal = cid <= rid
    lane_row = lax.broadcasted_iota(jnp.int32, (1, GATE_LANES), 1)

    def ffn_piece(j0, j1):
        acc_s[...] += _ffn_part(xn_s[...], w_ff1_ref, w_ff2_ref, j0, j1)

    def conv_block(r0, rb):
        sub = SUBLANES
        acc = None
        for ph in range(sub):
            n_rows = rb if ph == 0 else rb + sub
            grp = None
            for j in range(ph, off + CONV_WIDTH, sub):
                if j < off:
                    continue
                term = w_dw_ref[j - off:j - off + 1, :] * ext_s[r0 + j - ph:r0 + j - ph + n_rows, :]
                grp = term if grp is None else grp + term
            part = grp if ph == 0 else grp[ph:ph + rb]
            acc = part if acc is None else acc + part
        cc_s[r0:r0 + rb, :] = acc + b_dw_ref[...]

    def mlstm_piece(ci, h):
        rows = slice(ci * CHUNK, (ci + 1) * CHUNK)
        cols = slice(h * DK, (h + 1) * DK)
        gates_c = g_s[rows, :]
        if h == 0:
            bc = _cumsum_rows(jax.nn.log_sigmoid(gates_c))
            gc_s[0] = bc
            gc_s[1] = gates_c.T
            gc_s[2] = bc.T
        m_row = m_s[...]
        m_prev = m_row[:, F_LANE + h:F_LANE + h + 1]
        icol, bcol, m_t, w_intra, w_inter = _chunk_weights(
            gates_c, gc_s[0], gc_s[1], gc_s[2], h, causal, m_prev)
        q16 = q_s[rows, cols]
        k16 = k_s[rows, cols]
        vaug = _v_aug(v_s[rows, cols])
        nd = _intra(q16, k16, vaug.astype(bf16), w_intra)
        ct = ct_s[h]
        qc = _bdot(q16, ct.astype(bf16))
        hh_s[rows, cols] = _finish(nd, w_inter, qc[:, :DK], qc[:, DK:DK + 1], m_t)
        m_new = m_t[CHUNK - 1:CHUNK]
        b_last = bcol[CHUNK - 1:CHUNK]
        gcol = jnp.exp(((b_last - bcol) + icol) - m_new)
        decay = jnp.exp(b_last + m_prev - m_new)
        ct_s[h] = decay * ct + _dot_tn(k16, (gcol * vaug).astype(bf16))
        m_s[...] = jnp.where(lane_row == F_LANE + h, m_new, m_row)

    n_piece = (tm // CHUNK) * N_HEADS
    d_ff = w_ff1_ref.shape[1]
    fstep, rb = d_ff // FFN_PIECES, tm // n_piece
    for j in range(n_piece):
        if j % (n_piece // FFN_PIECES) == 0:
            jf = j // (n_piece // FFN_PIECES)
            ffn_piece(jf * fstep, (jf + 1) * fstep)
        conv_block(j * rb, rb)
        mlstm_piece(j // N_HEADS, j % N_HEADS)

    y_ref[...] = _ple_final(acc_s[...], p_ref[...], g_ple_ref, w_gate_ref, w_proj_ref,
                            g_final_ref)
    hm = jnp.concatenate(
        [_head_out(hh_s[:, h * DK:(h + 1) * DK], o_s[:, h * DK:(h + 1) * DK],
                   g_head_ref[:, h * DK:(h + 1) * DK]) for h in range(N_HEADS)], axis=1)
    hm_s[...] = hm.astype(bf16)
    cact_s[...] = _conv_post(cc_s[...], g_cn_ref, b_cn_ref).astype(bf16)
    ext_s[0:HIST_ROWS, :] = ext_s[tm:tm + HIST_ROWS, :]

    @pl.when(jnp.logical_and(t == nt - 1, s < n_tiles))
    def _():
        for h in range(N_HEADS):
            ctt = ct_s[h].T
            c_out_ref[h] = ctt[:DK]
            n_out_ref[h:h + 1, :] = ctt[DK:DK + 1]
        m_out_ref[...] = jnp.broadcast_to(m_s[...], m_out_ref.shape)
        conv_out_ref[...] = ext_s[off:HIST_ROWS, :]


def _const_spec(shape):
    return pl.BlockSpec(shape, lambda *_: (0,) * len(shape), pipeline_mode=pl.Buffered(1))


def _weight_specs(ws):
    return [_const_spec(w.shape) for w in ws]


def _prompt_call(x, p, ws):
    B, T, D = x.shape
    tm = TILE_ROWS
    nt = T // tm
    n_tiles = B * nt
    d_conv = ws[4].shape[1]
    front = lambda s: (lax.rem(s, n_tiles) // nt, lax.rem(lax.rem(s, n_tiles), nt), 0)
    back = lambda s: (jnp.maximum(s - 1, 0) // nt, lax.rem(jnp.maximum(s - 1, 0), nt), 0)
    state = lambda s: (jnp.minimum(s, n_tiles - 1) // nt, 0, 0)
    sd = jax.ShapeDtypeStruct
    rows_buf = lambda w, dt: pltpu.VMEM((tm, w), dt)
    return pl.pallas_call(
        functools.partial(_prompt_kernel, nt),
        grid=(n_tiles + 1,),
        in_specs=[pl.BlockSpec((None, tm, D), front), pl.BlockSpec((None, tm, D), back),
                  pl.BlockSpec((None, tm, p.shape[2]), back)] + _weight_specs(ws),
        out_specs=(
            pl.BlockSpec((None, tm, D), back),
            pl.BlockSpec((None, N_HEADS, DK, DK), lambda s: state(s) + (0,)),
            pl.BlockSpec((None, N_HEADS, DK), state),
            pl.BlockSpec((None, SUBLANES, GATE_LANES), state),
            pl.BlockSpec((None, CONV_BUF, d_conv), state),
        ),
        out_shape=(
            sd((B, T, D), f32),
            sd((B, N_HEADS, DK, DK), f32),
            sd((B, N_HEADS, DK), f32),
            sd((B, SUBLANES, GATE_LANES), f32),
            sd((B, CONV_BUF, d_conv), f32),
        ),
        scratch_shapes=[
            pltpu.VMEM((N_HEADS, DK, 2 * DK), f32),
            pltpu.VMEM((1, GATE_LANES), f32),
            pltpu.VMEM((HIST_ROWS + tm, d_conv), f32),
            rows_buf(D_MLSTM, bf16), rows_buf(d_conv, bf16),
            rows_buf(D, f32), rows_buf(D, bf16),
            rows_buf(D_MLSTM, bf16), rows_buf(D_MLSTM, bf16),
            rows_buf(D_MLSTM, f32), rows_buf(D_MLSTM, f32),
            rows_buf(GATE_LANES, f32),
            pltpu.VMEM((3, CHUNK, GATE_LANES), f32),
            rows_buf(D_MLSTM, f32), rows_buf(d_conv, f32),
        ],
        compiler_params=pltpu.CompilerParams(
            dimension_semantics=("arbitrary",), vmem_limit_bytes=VMEM_LIMIT_BYTES),
        name="prompt_layer",
    )(x, x, p, *ws)


def _sample_rec_kernel(x_ref, mrep_ref, c_ref, n_ref, hist_ref, g_mix_ref, w_in_ref, b_in_ref,
                       g_head_ref, w_dw_ref, b_dw_ref, g_cn_ref, b_cn_ref,
                       hm_ref, cact_ref, c_out_ref, n_out_ref, mt_ref, hist_out_ref,
                       numi_s, dec_s, gk_s, q_s):
    nb_blk, seq_len, d_model = x_ref.shape
    rows = nb_blk * seq_len
    nb = rows // seq_len
    shift = seq_len.bit_length() - 1
    d = D_MLSTM

    rid = lax.broadcasted_iota(jnp.int32, (rows, rows), 0)
    cid = lax.broadcasted_iota(jnp.int32, (rows, rows), 1)
    same = (rid >> shift) == (cid >> shift)
    mask = jnp.logical_and(same, cid <= rid)
    lcum16 = jnp.where(mask, 1.0, 0.0).astype(bf16)
    bsum16 = jnp.where(same, 1.0, 0.0).astype(bf16)
    plast16 = jnp.where(cid == ((rid >> shift) << shift) + (seq_len - 1), 1.0, 0.0).astype(bf16)
    lane_g = lax.broadcasted_iota(jnp.int32, (rows, GATE_LANES), 1)

    q, k, v, o, gates, u = _in_proj(x_ref[...].reshape(rows, d_model), g_mix_ref, w_in_ref,
                                    b_in_ref)
    q_s[...] = q
    lf, bc, gates_t, bc_t = _gate_cumsum(gates, lcum16)
    blast = _exact_left(bsum16, lf)
    mrep = mrep_ref[...]

    q16 = q.astype(bf16)
    k16 = k.astype(bf16)

    per_head = []
    for h in range(N_HEADS):
        cols = slice(h * DK, (h + 1) * DK)
        m_prev = mrep[:, F_LANE + h:F_LANE + h + 1]
        icol, bcol, m_t, w_intra, w_inter = _chunk_weights(gates, bc, gates_t, bc_t, h, mask, m_prev)
        vaug = _v_aug(v[:, cols])
        nd = _intra(q16[:, cols], k16[:, cols], vaug.astype(bf16), w_intra)
        per_head.append((icol, bcol, m_t, w_inter, nd))
    mt_all = _lane_pick([ph[2] for ph in per_head], lane_g)
    mt_ref[...] = mt_all
    mnew = _exact_left(plast16, mt_all)
    dec_s[...] = jnp.exp(blast + mrep - mnew)
    gvt16 = []
    for h in range(N_HEADS):
        cols = slice(h * DK, (h + 1) * DK)
        icol, bcol, _, _, _ = per_head[h]
        lanes = slice(F_LANE + h, F_LANE + h + 1)
        gcol = jnp.exp(((blast[:, lanes] - bcol) + icol) - mnew[:, lanes])
        gvt16.append((gcol * v[:, cols]).T.astype(bf16))
        gk_s[:, cols] = gcol * k[:, cols]

    rsel = lax.broadcasted_iota(jnp.int32, (rows, nb), 0)
    bsel = lax.broadcasted_iota(jnp.int32, (rows, nb), 1)
    expand16 = jnp.where((rsel >> shift) == bsel, 1.0, 0.0).astype(bf16)
    nrep = _exact_left(expand16, n_ref[...])

    grp = lax.broadcasted_iota(jnp.int32, (rows, DK), 0) >> shift
    numi_s[...] = jnp.zeros_like(numi_s)

    sub = SUBLANES
    tile_shift = (sub // seq_len).bit_length() - 1
    in_tile = lax.broadcasted_iota(jnp.int32, (sub, d), 0) >> shift

    def per_seq(b, carry):
        mb = grp == b
        r0 = pl.multiple_of((b >> tile_shift) * sub, sub)
        mine = in_tile == (b & (sub // seq_len - 1))
        q8 = q_s[pl.ds(r0, sub), :]
        dec_b = dec_s[pl.ds(b * seq_len, 1), :]
        n_b = n_ref[pl.ds(b, 1), :]
        gk8 = jnp.sum(jnp.where(mine, gk_s[pl.ds(r0, sub), :], 0.0), axis=0, keepdims=True)
        n_new = []
        num_i = []
        for h in range(N_HEADS):
            cols = slice(h * DK, (h + 1) * DK)
            cb = c_ref[b, h]
            num_i.append(_dot_nt(q8[:, cols].astype(bf16), cb.astype(bf16)))
            kb = jnp.where(mb, k[:, cols], 0.0).astype(bf16)
            dec = dec_b[:, F_LANE + h:F_LANE + h + 1]
            c_out_ref[b, h] = dec * cb + _bdot(gvt16[h], kb)
            n_new.append(dec * n_b[:, cols] + gk8[:, cols])
        numi_s[pl.ds(r0, sub), :] = jnp.where(mine, jnp.concatenate(num_i, axis=1),
                                              numi_s[pl.ds(r0, sub), :])
        n_out_ref[pl.ds(b, 1), :] = jnp.concatenate(n_new, axis=1)
        return carry

    lax.fori_loop(0, nb, per_seq, 0, unroll=SEQ_UNROLL)

    hm = []
    for h in range(N_HEADS):
        cols = slice(h * DK, (h + 1) * DK)
        _, _, m_t, w_inter, nd = per_head[h]
        den_i = jnp.sum(q[:, cols] * nrep[:, cols], axis=-1, keepdims=True)
        hh = _finish(nd, w_inter, numi_s[:, cols], den_i, m_t)
        hm.append(_head_out(hh, o[:, cols], g_head_ref[:, cols]))
    hm_ref[...] = jnp.concatenate(hm, axis=1)

    rt = lax.broadcasted_iota(jnp.int32, (rows, rows), 0)
    rbm = lax.broadcasted_iota(jnp.int32, (rows, rows), 1)
    nb_shift = nb.bit_length() - 1
    to_tm16 = jnp.where(rbm == ((rt & (nb - 1)) << shift) + (rt >> nb_shift), 1.0, 0.0).astype(bf16)
    to_bm16 = jnp.where(rt == ((rbm & (nb - 1)) << shift) + (rbm >> nb_shift), 1.0, 0.0).astype(bf16)
    u_tm = _exact_left(to_tm16, u)
    u_steps = [u_tm[s * nb:(s + 1) * nb] for s in range(seq_len)]
    c_steps = []
    for t in range(seq_len):
        acc = None
        for j in range(t, CONV_BUF):
            term = w_dw_ref[j - t:j - t + 1, :] * hist_ref[j]
            acc = term if acc is None else acc + term
        for s in range(t + 1):
            acc = acc + w_dw_ref[CONV_BUF + s - t:CONV_BUF + s - t + 1, :] * u_steps[s]
        c_steps.append(acc + b_dw_ref[...])
    cact_tm = _conv_post(jnp.concatenate(c_steps, axis=0), g_cn_ref, b_cn_ref)
    cact_ref[...] = _bdot(to_bm16, cact_tm.astype(bf16))
    hist_out_ref[0:CONV_BUF - seq_len] = hist_ref[seq_len:CONV_BUF]
    for s in range(seq_len):
        hist_out_ref[CONV_BUF - seq_len + s] = u_steps[s]


def _sample_rec_call(x, mrep, c_state, n_state, hist, ws):
    nb_total, seq_len, D = x.shape
    R = nb_total * seq_len
    gb = SAMPLE_GROUP
    rows = gb * seq_len
    d_conv = hist.shape[2]
    sd = jax.ShapeDtypeStruct
    rspec = lambda w: pl.BlockSpec((rows, w), lambda i: (i, 0))
    return pl.pallas_call(
        _sample_rec_kernel,
        grid=(nb_total // gb,),
        in_specs=[pl.BlockSpec((gb, seq_len, D), lambda i: (i, 0, 0)), rspec(GATE_LANES),
                  pl.BlockSpec((gb, N_HEADS, DK, DK), lambda i: (i, 0, 0, 0)),
                  pl.BlockSpec((gb, D_MLSTM), lambda i: (i, 0)),
                  pl.BlockSpec((CONV_BUF, gb, d_conv), lambda i: (0, i, 0))]
        + _weight_specs(ws),
        out_specs=(rspec(D_MLSTM), rspec(d_conv),
                   pl.BlockSpec((gb, N_HEADS, DK, DK), lambda i: (i, 0, 0, 0)),
                   pl.BlockSpec((gb, D_MLSTM), lambda i: (i, 0)),
                   rspec(GATE_LANES),
                   pl.BlockSpec((CONV_BUF, gb, d_conv), lambda i: (0, i, 0))),
        out_shape=(sd((R, D_MLSTM), f32), sd((R, d_conv), f32),
                   sd(c_state.shape, f32), sd(n_state.shape, f32),
                   sd((R, GATE_LANES), f32), sd(hist.shape, f32)),
        scratch_shapes=[pltpu.VMEM((rows, D_MLSTM), f32), pltpu.VMEM((rows, GATE_LANES), f32),
                        pltpu.VMEM((rows, D_MLSTM), f32), pltpu.VMEM((rows, D_MLSTM), f32)],
        compiler_params=pltpu.CompilerParams(
            dimension_semantics=("arbitrary",), vmem_limit_bytes=VMEM_LIMIT_BYTES),
        name="sample_recurrent",
    )(x, mrep, c_state, n_state, hist, *ws)


def _sample_tail_kernel(x_ref, hm_ref, cact_ref, p_ref, w_out_ref, g_ffn_ref, w_ff1_ref, w_ff2_ref,
                        g_ple_ref, w_gate_ref, w_proj_ref, g_final_ref, y_ref, acc_s, xn_s):
    j = pl.program_id(0)
    nb, seq_len, d_model = x_ref.shape
    rows = nb * seq_len

    @pl.when(j == 0)
    def _():
        x1 = _out_proj(x_ref[...].reshape(rows, d_model), hm_ref[...], cact_ref[...], w_out_ref)
        acc_s[...] = x1
        xn_s[...] = _rms(x1, g_ffn_ref[...]).astype(bf16)

    step = w_ff1_ref.shape[1]
    acc_s[...] += _ffn_part(xn_s[...], w_ff1_ref, w_ff2_ref, 0, step)

    @pl.when(j == pl.num_programs(0) - 1)
    def _():
        y = _ple_final(acc_s[...], p_ref[...].reshape(rows, p_ref.shape[2]), g_ple_ref, w_gate_ref,
                       w_proj_ref, g_final_ref)
        y_ref[...] = y.reshape(nb, seq_len, d_model)


def _sample_tail_call(x, hm, cact, p, ws):
    bs, seq_len, D = x.shape
    R = bs * seq_len
    w_out, g_ffn, w_ff1, w_ff2, g_ple, w_gate, w_proj, g_fin = ws
    d_ff = w_ff1.shape[1]
    step = d_ff // FFN_PIECES
    return pl.pallas_call(
        _sample_tail_kernel,
        grid=(FFN_PIECES,),
        in_specs=[_const_spec(x.shape), _const_spec(hm.shape), _const_spec(cact.shape),
                  _const_spec(p.shape), _const_spec(w_out.shape), _const_spec(g_ffn.shape),
                  pl.BlockSpec((w_ff1.shape[0], step), lambda j: (0, j)),
                  pl.BlockSpec((step, w_ff2.shape[1]), lambda j: (j, 0)),
                  _const_spec(g_ple.shape), _const_spec(w_gate.shape), _const_spec(w_proj.shape),
                  _const_spec(g_fin.shape)],
        out_specs=_const_spec(x.shape),
        out_shape=jax.ShapeDtypeStruct(x.shape, f32),
        scratch_shapes=[pltpu.VMEM((R, D), f32), pltpu.VMEM((R, D), bf16)],
        compiler_params=pltpu.CompilerParams(
            dimension_semantics=("arbitrary",), vmem_limit_bytes=VMEM_LIMIT_BYTES),
        name="sample_tail",
    )(x, hm, cact, p, *ws)


def _w_in_layout_kernel(d4, n_gate, wt_ref, o_ref):
    blk = TILE_ROWS
    rest = wt_ref.shape[0] - d4 - n_gate
    for c in range(0, d4, blk):
        o_ref[:, c:c + blk] = wt_ref[c:c + blk, :].T.astype(bf16)
    lane = lax.broadcasted_iota(jnp.int32, (wt_ref.shape[1], GATE_LANES), 1)
    gates = wt_ref[d4:d4 + GATE_LANES, :].T
    o_ref[:, d4:d4 + GATE_LANES] = jnp.where(lane < n_gate, gates, 0.0).astype(bf16)
    for c in range(0, rest, blk):
        o_ref[:, d4 + GATE_LANES + c:d4 + GATE_LANES + c + blk] = (
            wt_ref[d4 + n_gate + c:d4 + n_gate + c + blk, :].T.astype(bf16))


def _w_in_layout_call(wt, d4, n_gate):
    width, rows = wt.shape
    out_w = width - n_gate + GATE_LANES
    return pl.pallas_call(
        functools.partial(_w_in_layout_kernel, d4, n_gate),
        grid=(1,),
        in_specs=[_const_spec(wt.shape)],
        out_specs=_const_spec((rows, out_w)),
        out_shape=jax.ShapeDtypeStruct((rows, out_w), bf16),
        compiler_params=pltpu.CompilerParams(
            dimension_semantics=("arbitrary",), vmem_limit_bytes=VMEM_LIMIT_BYTES),
        name="w_in_layout",
    )(wt)


def _layer_weights(i, g_mix, w_in, b_in, g_head, w_dw, b_dw, g_cn, b_cn, w_out, g_ffn, w_ff1,
                   w_ff2, g_ple, w_ple_gate, w_ple_proj):
    row = lambda a: a[i].reshape(1, -1).astype(f32)
    d4 = 4 * D_MLSTM
    n_gate = 2 * N_HEADS
    pad = GATE_LANES - n_gate
    w_in_p = _w_in_layout_call(w_in[i].T, d4, n_gate)
    b = b_in[i]
    b_in_p = jnp.concatenate(
        [b[:d4], jnp.pad(b[d4:d4 + n_gate], (0, pad)), b[d4 + n_gate:]]).reshape(1, -1).astype(f32)
    w_dw_p = jnp.pad(w_dw[i].astype(f32), ((0, HIST_ROWS - CONV_WIDTH), (0, 0)))
    return dict(
        g_mix=row(g_mix), w_in=w_in_p, b_in=b_in_p, g_head=row(g_head), w_dw=w_dw_p,
        b_dw=row(b_dw), g_cn=row(g_cn), b_cn=row(b_cn), w_out=w_out[i].astype(bf16),
        g_ffn=row(g_ffn), w_ff1=w_ff1[i].astype(bf16), w_ff2=w_ff2[i].astype(bf16),
        g_ple=row(g_ple), w_gate=w_ple_gate[i].astype(bf16), w_proj=w_ple_proj[i].astype(bf16))


def kernel(x_prompt, x_sample, state_mlstm_C, state_mlstm_n, state_mlstm_m, cache_conv, p_prompt,
           p_sample, g_mix, w_in, b_in, g_head, w_dw, b_dw, g_cn, b_cn, w_out, g_ffn, w_ff1, w_ff2,
           g_ple, w_ple_gate, w_ple_proj, g_final):
    depth = w_in.shape[0]
    assert depth == 1, "the final norm is fused into the layer kernels"
    bs, seq_len, d_model = x_sample.shape
    assert seq_len & (seq_len - 1) == 0 and seq_len <= SUBLANES
    assert SAMPLE_GROUP & (SAMPLE_GROUP - 1) == 0 and SAMPLE_GROUP * seq_len == CHUNK
    g_fin = g_final.reshape(1, -1).astype(f32)

    i = 0
    lw = _layer_weights(i, g_mix, w_in, b_in, g_head, w_dw, b_dw, g_cn, b_cn, w_out, g_ffn,
                        w_ff1, w_ff2, g_ple, w_ple_gate, w_ple_proj)
    tail_ws = (lw["w_out"], lw["g_ffn"], lw["w_ff1"], lw["w_ff2"], lw["g_ple"], lw["w_gate"],
               lw["w_proj"], g_fin)

    prompt_ws = (lw["g_mix"], lw["w_in"], lw["b_in"], lw["g_head"], lw["w_dw"], lw["b_dw"],
                 lw["g_cn"], lw["b_cn"]) + tail_ws
    y_p, c_p, n_p, m_p, conv_p = _prompt_call(x_prompt, p_prompt[i], prompt_ws)
    m_p = m_p[:, 0, F_LANE:F_LANE + N_HEADS]

    m0 = jnp.pad(state_mlstm_m[i].astype(f32), ((0, 0), (F_LANE, GATE_LANES - F_LANE - N_HEADS)))
    mrep = jnp.repeat(m0, seq_len, axis=0)
    hm, cact, c_s, n_s, mt, conv_s = _sample_rec_call(
        x_sample, mrep, state_mlstm_C[i], state_mlstm_n[i].reshape(bs, -1),
        cache_conv[i].transpose(1, 0, 2), prompt_ws[:8])
    conv_s = conv_s.transpose(1, 0, 2)
    y_s = _sample_tail_call(x_sample, hm, cact, p_sample[i], tail_ws)
    m_s = mt[seq_len - 1::seq_len, F_LANE:F_LANE + N_HEADS]
    n_s = n_s.reshape(bs, N_HEADS, DK)

    stack = lambda a: a[None]
    return (y_p, y_s, stack(c_p), stack(n_p), stack(m_p), stack(conv_p),
            stack(c_s), stack(n_s), stack(m_s), stack(conv_s))
```

```python
import functools

import jax
import jax.numpy as jnp
from jax import lax
from jax.experimental import pallas as pl
from jax.experimental.pallas import tpu as pltpu

f32 = jnp.float32
bf16 = jnp.bfloat16

N_HEADS = 4
DK = 128
D_MLSTM = N_HEADS * DK
CONV_WIDTH = 31
CONV_BUF = CONV_WIDTH - 1
EPS = 1e-6
SUBLANES = 8
GATE_LANES = 128
F_LANE = N_HEADS
CHUNK = 128
TILE_ROWS = 256
HIST_ROWS = 32
SAMPLE_GROUP = 32
FFN_PIECES = 4
SEQ_UNROLL = 8
VMEM_LIMIT_BYTES = 56 * 1024 * 1024


def _bdot(a, b):
    return jnp.dot(a, b, preferred_element_type=f32)


def _dot_nt(a, b):
    return lax.dot_general(a, b, (((1,), (1,)), ((), ())), preferred_element_type=f32)


def _dot_tn(a, b):
    return lax.dot_general(a, b, (((0,), (0,)), ((), ())), preferred_element_type=f32)


def _rms(x, g):
    y = x * lax.rsqrt(jnp.mean(x * x, axis=-1, keepdims=True) + EPS)
    return y * g


def _ln(x):
    mu = jnp.mean(x, axis=-1, keepdims=True)
    xc = x - mu
    return xc * lax.rsqrt(jnp.mean(xc * xc, axis=-1, keepdims=True) + EPS)


def _exact_left(sel16, x):
    hi = x.astype(bf16)
    r = x - hi.astype(f32)
    mid = r.astype(bf16)
    lo = (r - mid.astype(f32)).astype(bf16)
    return _bdot(sel16, hi) + _bdot(sel16, mid) + _bdot(sel16, lo)


def _lane_pick(rows, lane_ids):
    out = jnp.zeros((rows[0].shape[0], GATE_LANES), f32)
    for h, r in enumerate(rows):
        out = jnp.where(lane_ids == F_LANE + h, r, out)
    return out


def _in_proj(x, g_mix_ref, w_in_ref, b_in_ref):
    xn = _rms(x, g_mix_ref[...]).astype(bf16)

    def proj(a, b):
        return _bdot(xn, w_in_ref[:, a:b]) + b_in_ref[:, a:b]

    d = D_MLSTM
    q = proj(0, d)
    k = proj(d, 2 * d) * (DK ** -0.5)
    v = proj(2 * d, 3 * d)
    o = proj(3 * d, 4 * d)
    gates = proj(4 * d, 4 * d + GATE_LANES)
    c0 = 4 * d + GATE_LANES
    ga = proj(c0, c0 + d)
    gg = proj(c0 + d, c0 + 2 * d)
    u = ga * jax.nn.sigmoid(gg)
    return q, k, v, o, gates, u


def _gate_cumsum(gates, lcum16):
    lf = jax.nn.log_sigmoid(gates)
    bc = _exact_left(lcum16, lf)
    return lf, bc, gates.T, bc.T


def _chunk_weights(gates, bc, gates_t, bc_t, h, mask, m_prev):
    icol = gates[:, h:h + 1]
    bcol = bc[:, F_LANE + h:F_LANE + h + 1]
    irow = gates_t[h:h + 1, :]
    brow = bc_t[F_LANE + h:F_LANE + h + 1, :]
    logw = jnp.where(mask, (bcol - brow) + irow, -jnp.inf)
    m_intra = jnp.max(logw, axis=-1, keepdims=True)
    log_inter = bcol + m_prev
    m_t = jnp.maximum(log_inter, m_intra)
    w_intra = jnp.exp(logw - m_t)
    w_inter = jnp.exp(log_inter - m_t)
    return icol, bcol, m_t, w_intra, w_inter


def _v_aug(v):
    lane = lax.broadcasted_iota(jnp.int32, v.shape, 1)
    return jnp.concatenate([v, jnp.where(lane == 0, 1.0, 0.0)], axis=1)


def _intra(q16, k16, vaug16, w_intra):
    s = _dot_nt(q16, k16)
    a = (w_intra * s).astype(bf16)
    return _bdot(a, vaug16)


def _finish(nd, w_inter, num_inter, den_inter, m_t):
    num = nd[:, :DK] + w_inter * num_inter
    den = nd[:, DK:DK + 1] + w_inter * den_inter
    return num / jnp.maximum(jnp.abs(den), jnp.exp(-m_t))


def _head_out(h, o, g_head):
    return (_ln(h) * g_head) * jax.nn.sigmoid(o)


def _conv_post(c, g_cn_ref, b_cn_ref):
    c = _ln(c) * g_cn_ref[...] + b_cn_ref[...]
    return c * jax.nn.sigmoid(c)


def _out_proj(x, hm, c, w_out_ref):
    d = D_MLSTM
    return x + (_bdot(hm.astype(bf16), w_out_ref[0:d, :]) + _bdot(c.astype(bf16), w_out_ref[d:, :]))


def _ffn_part(xn16, w_ff1_ref, w_ff2_ref, j0, j1):
    f = jnp.maximum(_bdot(xn16, w_ff1_ref[:, j0:j1]), 0.0)
    return _bdot((f * f).astype(bf16), w_ff2_ref[j0:j1, :])


def _ple_final(x, p, g_ple_ref, w_gate_ref, w_proj_ref, g_final_ref):
    gate = jax.nn.sigmoid(_bdot(_rms(x, g_ple_ref[...]).astype(bf16), w_gate_ref[...]))
    x = x + gate * _bdot(p.astype(bf16), w_proj_ref[...])
    return _rms(x, g_final_ref[...])


def _cumsum_rows(x):
    sub = SUBLANES
    pos = lax.broadcasted_iota(jnp.int32, x.shape, 0) & (sub - 1)
    sh = 1
    while sh < sub:
        x = x + jnp.where(pos >= sh, pltpu.roll(x, sh, axis=0), 0.0)
        sh *= 2
    tiles = []
    carry = None
    for r0 in range(0, x.shape[0], sub):
        tile = x[r0:r0 + sub]
        if carry is not None:
            tile = tile + carry
        carry = tile[sub - 1:sub]
        tiles.append(tile)
    return jnp.concatenate(tiles, axis=0)


def _prompt_kernel(nt, xf_ref, xb_ref, p_ref, g_mix_ref, w_in_ref, b_in_ref, g_head_ref, w_dw_ref,
                   b_dw_ref, g_cn_ref, b_cn_ref, w_out_ref, g_ffn_ref, w_ff1_ref, w_ff2_ref,
                   g_ple_ref, w_gate_ref, w_proj_ref, g_final_ref,
                   y_ref, c_out_ref, n_out_ref, m_out_ref, conv_out_ref,
                   ct_s, n_s, m_s, ext_s, hm_s, cact_s, acc_s, xn_s, q_s, k_s, v_s, o_s, g_s, gc_s,
                   hh_s, cc_s):
    s = pl.program_id(0)
    n_tiles = pl.num_programs(0) - 1
    t = lax.rem(lax.rem(s, n_tiles), nt)
    tm = xf_ref.shape[0]
    off = HIST_ROWS - CONV_BUF

    @pl.when(s == 0)
    def _():
        hm_s[...] = jnp.zeros_like(hm_s)
        cact_s[...] = jnp.zeros_like(cact_s)

    @pl.when(t == 0)
    def _():
        ct_s[...] = jnp.zeros_like(ct_s)
        n_s[...] = jnp.zeros_like(n_s)
        m_s[...] = jnp.zeros_like(m_s)
        ext_s[0:HIST_ROWS, :] = jnp.zeros((HIST_ROWS, ext_s.shape[1]), f32)

    x1 = _out_proj(xb_ref[...], hm_s[...], cact_s[...], w_out_ref)
    acc_s[...] = x1
    xn_s[...] = _rms(x1, g_ffn_ref[...]).astype(bf16)
    q, k, v, o, gates, u = _in_proj(xf_ref[...], g_mix_ref, w_in_ref, b_in_ref)
    q_s[...] = q.astype(bf16)
    k_s[...] = k.astype(bf16)
    v_s[...] = v
    o_s[...] = o
    g_s[...] = gates
    ext_s[HIST_ROWS:HIST_ROWS + tm, :] = u

    rid = lax.broadcasted_iota(jnp.int32, (CHUNK, CHUNK), 0)
    cid = lax.broadcasted_iota(jnp.int32, (CHUNK, CHUNK), 1)
    causal = cid <= rid
    lane_row = lax.broadcasted_iota(jnp.int32, (1, GATE_LANES), 1)

    def ffn_piece(j0, j1):
        acc_s[...] += _ffn_part(xn_s[...], w_ff1_ref, w_ff2_ref, j0, j1)

    def conv_block(r0, rb):
        sub = SUBLANES
        acc = None
        for ph in range(sub):
            n_rows = rb if ph == 0 else rb + sub
            grp = None
            for j in range(ph, off + CONV_WIDTH, sub):
                if j < off:
                    continue
                term = w_dw_ref[j - off:j - off + 1, :] * ext_s[r0 + j - ph:r0 + j - ph + n_rows, :]
                grp = term if grp is None else grp + term
            part = grp if ph == 0 else grp[ph:ph + rb]
            acc = part if acc is None else acc + part
        cc_s[r0:r0 + rb, :] = acc + b_dw_ref[...]

    def mlstm_pair(ci, hp):
        rows = slice(ci * CHUNK, (ci + 1) * CHUNK)
        heads = (2 * hp, 2 * hp + 1)
        cols2 = slice(heads[0] * DK, (heads[1] + 1) * DK)
        gates_c = g_s[rows, :]
        if hp == 0:
            bc = _cumsum_rows(jax.nn.log_sigmoid(gates_c))
            gc_s[0] = bc
            gc_s[1] = gates_c.T
            gc_s[2] = bc.T
        m_row = m_s[...]
        q2 = q_s[rows, cols2]
        k2 = k_s[rows, cols2]
        v2 = v_s[rows, cols2]
        zero16 = jnp.zeros((CHUNK, DK), bf16)

        def blockdiag(a0, a1):
            return jnp.concatenate([jnp.concatenate([a0, zero16], axis=1),
                                    jnp.concatenate([zero16, a1], axis=1)], axis=0)

        s2 = _dot_nt(q2, blockdiag(k2[:, :DK], k2[:, DK:]))
        per = []
        for i, h in enumerate(heads):
            m_prev = m_row[:, F_LANE + h:F_LANE + h + 1]
            icol, bcol, m_t, w_intra, w_inter = _chunk_weights(
                gates_c, gc_s[0], gc_s[1], gc_s[2], h, causal, m_prev)
            a = w_intra * s2[:, i * DK:(i + 1) * DK]
            per.append((h, m_prev, icol, bcol, m_t, w_inter, a))
        a2 = jnp.concatenate([per[0][6], per[1][6]], axis=1).astype(bf16)
        v16 = v2.astype(bf16)
        num2 = _bdot(a2, blockdiag(v16[:, :DK], v16[:, DK:]))
        ct0, ct1 = ct_s[heads[0]], ct_s[heads[1]]
        qc2 = _bdot(q2, blockdiag(ct0.astype(bf16), ct1.astype(bf16)))
        m_new_row = m_row
        for i, (h, m_prev, icol, bcol, m_t, w_inter, a) in enumerate(per):
            hc = slice(i * DK, (i + 1) * DK)
            n_row = n_s[h:h + 1, :]
            den = (jnp.sum(a, axis=-1, keepdims=True)
                   + w_inter * jnp.sum(q2[:, hc].astype(f32) * n_row, axis=-1, keepdims=True))
            num = num2[:, hc] + w_inter * qc2[:, hc]
            hh_s[rows, h * DK:(h + 1) * DK] = num / jnp.maximum(jnp.abs(den), jnp.exp(-m_t))
            m_new = m_t[CHUNK - 1:CHUNK]
            b_last = bcol[CHUNK - 1:CHUNK]
            gcol = jnp.exp(((b_last - bcol) + icol) - m_new)
            decay = jnp.exp(b_last + m_prev - m_new)
            ct = ct0 if i == 0 else ct1
            ct_s[h] = decay * ct + _dot_tn(k2[:, hc], (gcol * v2[:, hc]).astype(bf16))
            n_s[h:h + 1, :] = decay * n_row + jnp.sum(gcol * k2[:, hc].astype(f32), axis=0,
                                                      keepdims=True)
            m_new_row = jnp.where(lane_row == F_LANE + h, m_new, m_new_row)
        m_s[...] = m_new_row

    n_piece = (tm // CHUNK) * N_HEADS
    d_ff = w_ff1_ref.shape[1]
    fstep, rb = d_ff // FFN_PIECES, tm // n_piece
    for j in range(n_piece):
        if j % (n_piece // FFN_PIECES) == 0:
            jf = j // (n_piece // FFN_PIECES)
            ffn_piece(jf * fstep, (jf + 1) * fstep)
        conv_block(j * rb, rb)
        if j % 2 == 0:
            mlstm_pair(j // N_HEADS, (j % N_HEADS) // 2)

    y_ref[...] = _ple_final(acc_s[...], p_ref[...], g_ple_ref, w_gate_ref, w_proj_ref,
                            g_final_ref)
    hm = jnp.concatenate(
        [_head_out(hh_s[:, h * DK:(h + 1) * DK], o_s[:, h * DK:(h + 1) * DK],
                   g_head_ref[:, h * DK:(h + 1) * DK]) for h in range(N_HEADS)], axis=1)
    hm_s[...] = hm.astype(bf16)
    cact_s[...] = _conv_post(cc_s[...], g_cn_ref, b_cn_ref).astype(bf16)
    ext_s[0:HIST_ROWS, :] = ext_s[tm:tm + HIST_ROWS, :]

    @pl.when(jnp.logical_and(t == nt - 1, s < n_tiles))
    def _():
        for h in range(N_HEADS):
            c_out_ref[h] = ct_s[h].T
        n_out_ref[...] = n_s[0:N_HEADS, :]
        m_out_ref[...] = jnp.broadcast_to(m_s[...], m_out_ref.shape)
        conv_out_ref[...] = ext_s[off:HIST_ROWS, :]


def _const_spec(shape):
    return pl.BlockSpec(shape, lambda *_: (0,) * len(shape), pipeline_mode=pl.Buffered(1))


def _weight_specs(ws):
    return [_const_spec(w.shape) for w in ws]


def _prompt_call(x, p, ws):
    B, T, D = x.shape
    tm = TILE_ROWS
    nt = T // tm
    n_tiles = B * nt
    d_conv = ws[4].shape[1]
    front = lambda s: (lax.rem(s, n_tiles) // nt, lax.rem(lax.rem(s, n_tiles), nt), 0)
    back = lambda s: (jnp.maximum(s - 1, 0) // nt, lax.rem(jnp.maximum(s - 1, 0), nt), 0)
    state = lambda s: (jnp.minimum(s, n_tiles - 1) // nt, 0, 0)
    sd = jax.ShapeDtypeStruct
    rows_buf = lambda w, dt: pltpu.VMEM((tm, w), dt)
    return pl.pallas_call(
        functools.partial(_prompt_kernel, nt),
        grid=(n_tiles + 1,),
        in_specs=[pl.BlockSpec((None, tm, D), front), pl.BlockSpec((None, tm, D), back),
                  pl.BlockSpec((None, tm, p.shape[2]), back)] + _weight_specs(ws),
        out_specs=(
            pl.BlockSpec((None, tm, D), back),
            pl.BlockSpec((None, N_HEADS, DK, DK), lambda s: state(s) + (0,)),
            pl.BlockSpec((None, N_HEADS, DK), state),
            pl.BlockSpec((None, SUBLANES, GATE_LANES), state),
            pl.BlockSpec((None, CONV_BUF, d_conv), state),
        ),
        out_shape=(
            sd((B, T, D), f32),
            sd((B, N_HEADS, DK, DK), f32),
            sd((B, N_HEADS, DK), f32),
            sd((B, SUBLANES, GATE_LANES), f32),
            sd((B, CONV_BUF, d_conv), f32),
        ),
        scratch_shapes=[
            pltpu.VMEM((N_HEADS, DK, DK), f32),
            pltpu.VMEM((SUBLANES, DK), f32),
            pltpu.VMEM((1, GATE_LANES), f32),
            pltpu.VMEM((HIST_ROWS + tm, d_conv), f32),
            rows_buf(D_MLSTM, bf16), rows_buf(d_conv, bf16),
            rows_buf(D, f32), rows_buf(D, bf16),
            rows_buf(D_MLSTM, bf16), rows_buf(D_MLSTM, bf16),
            rows_buf(D_MLSTM, f32), rows_buf(D_MLSTM, f32),
            rows_buf(GATE_LANES, f32),
            pltpu.VMEM((3, CHUNK, GATE_LANES), f32),
            rows_buf(D_MLSTM, f32), rows_buf(d_conv, f32),
        ],
        compiler_params=pltpu.CompilerParams(
            dimension_semantics=("arbitrary",), vmem_limit_bytes=VMEM_LIMIT_BYTES),
        name="prompt_layer",
    )(x, x, p, *ws)


def _sample_in_kernel(x_ref, g_mix_ref, w_in_ref, b_in_ref, qkvo_ref, gates_ref, u_ref):
    nb, seq_len, d_model = x_ref.shape
    x = x_ref[...].reshape(nb * seq_len, d_model)
    q, k, v, o, gates, u = _in_proj(x, g_mix_ref, w_in_ref, b_in_ref)
    qkvo_ref[...] = jnp.concatenate([q, k, v, o], axis=1)
    gates_ref[...] = gates
    u_ref[...] = u


def _sample_in_call(x, g_mix, w_in, b_in):
    bs, seq_len, D = x.shape
    R = bs * seq_len
    tm = TILE_ROWS
    d_conv = (w_in.shape[1] - 4 * D_MLSTM - GATE_LANES) // 2
    sd = jax.ShapeDtypeStruct
    return pl.pallas_call(
        _sample_in_kernel,
        grid=(R // tm,),
        in_specs=[pl.BlockSpec((tm // seq_len, seq_len, D), lambda i: (i, 0, 0))]
        + _weight_specs((g_mix, w_in, b_in)),
        out_specs=(pl.BlockSpec((tm, 4 * D_MLSTM), lambda i: (i, 0)),
                   pl.BlockSpec((tm, GATE_LANES), lambda i: (i, 0)),
                   pl.BlockSpec((tm, d_conv), lambda i: (i, 0))),
        out_shape=(sd((R, 4 * D_MLSTM), f32), sd((R, GATE_LANES), f32), sd((R, d_conv), f32)),
        compiler_params=pltpu.CompilerParams(
            dimension_semantics=("arbitrary",), vmem_limit_bytes=VMEM_LIMIT_BYTES),
        name="sample_in_proj",
    )(x, g_mix, w_in, b_in)


def _sample_rec_kernel(seq_len, qkvo_ref, gates_ref, u_ref, mrep_ref, c_ref, n_ref, hist_ref,
                       g_head_ref, w_dw_ref, b_dw_ref, g_cn_ref, b_cn_ref,
                       hm_ref, cact_ref, c_out_ref, n_out_ref, mt_ref, hist_out_ref,
                       numi_s, dec_s, gk_s):
    rows = qkvo_ref.shape[0]
    nb = rows // seq_len
    shift = seq_len.bit_length() - 1
    d = D_MLSTM

    rid = lax.broadcasted_iota(jnp.int32, (rows, rows), 0)
    cid = lax.broadcasted_iota(jnp.int32, (rows, rows), 1)
    same = (rid >> shift) == (cid >> shift)
    mask = jnp.logical_and(same, cid <= rid)
    lcum16 = jnp.where(mask, 1.0, 0.0).astype(bf16)
    bsum16 = jnp.where(same, 1.0, 0.0).astype(bf16)
    plast16 = jnp.where(cid == ((rid >> shift) << shift) + (seq_len - 1), 1.0, 0.0).astype(bf16)
    lane_g = lax.broadcasted_iota(jnp.int32, (rows, GATE_LANES), 1)

    gates = gates_ref[...]
    lf, bc, gates_t, bc_t = _gate_cumsum(gates, lcum16)
    blast = _exact_left(bsum16, lf)
    mrep = mrep_ref[...]

    q = qkvo_ref[:, 0:d]
    k = qkvo_ref[:, d:2 * d]
    v = qkvo_ref[:, 2 * d:3 * d]
    o = qkvo_ref[:, 3 * d:4 * d]
    q16 = q.astype(bf16)
    k16 = k.astype(bf16)

    per_head = []
    for h in range(N_HEADS):
        cols = slice(h * DK, (h + 1) * DK)
        m_prev = mrep[:, F_LANE + h:F_LANE + h + 1]
        icol, bcol, m_t, w_intra, w_inter = _chunk_weights(gates, bc, gates_t, bc_t, h, mask, m_prev)
        vaug = _v_aug(v[:, cols])
        nd = _intra(q16[:, cols], k16[:, cols], vaug.astype(bf16), w_intra)
        per_head.append((icol, bcol, m_t, w_inter, nd))
    mt_all = _lane_pick([ph[2] for ph in per_head], lane_g)
    mt_ref[...] = mt_all
    mnew = _exact_left(plast16, mt_all)
    dec_s[...] = jnp.exp(blast + mrep - mnew)
    gvt16 = []
    for h in range(N_HEADS):
        cols = slice(h * DK, (h + 1) * DK)
        icol, bcol, _, _, _ = per_head[h]
        lanes = slice(F_LANE + h, F_LANE + h + 1)
        gcol = jnp.exp(((blast[:, lanes] - bcol) + icol) - mnew[:, lanes])
        gvt16.append((gcol * v[:, cols]).T.astype(bf16))
        gk_s[:, cols] = gcol * k[:, cols]

    rsel = lax.broadcasted_iota(jnp.int32, (rows, nb), 0)
    bsel = lax.broadcasted_iota(jnp.int32, (rows, nb), 1)
    expand16 = jnp.where((rsel >> shift) == bsel, 1.0, 0.0).astype(bf16)
    nrep = _exact_left(expand16, n_ref[...])

    grp = lax.broadcasted_iota(jnp.int32, (rows, DK), 0) >> shift
    numi_s[...] = jnp.zeros_like(numi_s)

    sub = SUBLANES
    tile_shift = (sub // seq_len).bit_length() - 1
    in_tile = lax.broadcasted_iota(jnp.int32, (sub, d), 0) >> shift

    def per_seq(b, carry):
        mb = grp == b
        r0 = pl.multiple_of((b >> tile_shift) * sub, sub)
        mine = in_tile == (b & (sub // seq_len - 1))
        q8 = qkvo_ref[pl.ds(r0, sub), 0:d]
        dec_b = dec_s[pl.ds(b * seq_len, 1), :]
        n_b = n_ref[pl.ds(b, 1), :]
        gk8 = jnp.sum(jnp.where(mine, gk_s[pl.ds(r0, sub), :], 0.0), axis=0, keepdims=True)
        n_new = []
        num_i = []
        for h in range(N_HEADS):
            cols = slice(h * DK, (h + 1) * DK)
            cb = c_ref[b, h]
            num_i.append(_dot_nt(q8[:, cols].astype(bf16), cb.astype(bf16)))
            kb = jnp.where(mb, k[:, cols], 0.0).astype(bf16)
            dec = dec_b[:, F_LANE + h:F_LANE + h + 1]
            c_out_ref[b, h] = dec * cb + _bdot(gvt16[h], kb)
            n_new.append(dec * n_b[:, cols] + gk8[:, cols])
        numi_s[pl.ds(r0, sub), :] = jnp.where(mine, jnp.concatenate(num_i, axis=1),
                                              numi_s[pl.ds(r0, sub), :])
        n_out_ref[pl.ds(b, 1), :] = jnp.concatenate(n_new, axis=1)
        return carry

    lax.fori_loop(0, nb, per_seq, 0, unroll=SEQ_UNROLL)

    hm = []
    for h in range(N_HEADS):
        cols = slice(h * DK, (h + 1) * DK)
        _, _, m_t, w_inter, nd = per_head[h]
        den_i = jnp.sum(q[:, cols] * nrep[:, cols], axis=-1, keepdims=True)
        hh = _finish(nd, w_inter, numi_s[:, cols], den_i, m_t)
        hm.append(_head_out(hh, o[:, cols], g_head_ref[:, cols]))
    hm_ref[...] = jnp.concatenate(hm, axis=1)

    rt = lax.broadcasted_iota(jnp.int32, (rows, rows), 0)
    rbm = lax.broadcasted_iota(jnp.int32, (rows, rows), 1)
    nb_shift = nb.bit_length() - 1
    to_tm16 = jnp.where(rbm == ((rt & (nb - 1)) << shift) + (rt >> nb_shift), 1.0, 0.0).astype(bf16)
    to_bm16 = jnp.where(rt == ((rbm & (nb - 1)) << shift) + (rbm >> nb_shift), 1.0, 0.0).astype(bf16)
    u_tm = _exact_left(to_tm16, u_ref[...])
    u_steps = [u_tm[s * nb:(s + 1) * nb] for s in range(seq_len)]
    c_steps = []
    for t in range(seq_len):
        acc = None
        for j in range(t, CONV_BUF):
            term = w_dw_ref[j - t:j - t + 1, :] * hist_ref[j]
            acc = term if acc is None else acc + term
        for s in range(t + 1):
            acc = acc + w_dw_ref[CONV_BUF + s - t:CONV_BUF + s - t + 1, :] * u_steps[s]
        c_steps.append(acc + b_dw_ref[...])
    cact_tm = _conv_post(jnp.concatenate(c_steps, axis=0), g_cn_ref, b_cn_ref)
    cact_ref[...] = _bdot(to_bm16, cact_tm.astype(bf16))
    hist_out_ref[0:CONV_BUF - seq_len] = hist_ref[seq_len:CONV_BUF]
    for s in range(seq_len):
        hist_out_ref[CONV_BUF - seq_len + s] = u_steps[s]


def _sample_rec_call(seq_len, qkvo, gates, u, mrep, c_state, n_state, hist,
                     g_head, w_dw, b_dw, g_cn, b_cn):
    R = qkvo.shape[0]
    nb_total = c_state.shape[0]
    gb = SAMPLE_GROUP
    rows = gb * seq_len
    d_conv = u.shape[1]
    sd = jax.ShapeDtypeStruct
    rspec = lambda w: pl.BlockSpec((rows, w), lambda i: (i, 0))
    return pl.pallas_call(
        functools.partial(_sample_rec_kernel, seq_len),
        grid=(nb_total // gb,),
        in_specs=[rspec(qkvo.shape[1]), rspec(GATE_LANES), rspec(d_conv), rspec(GATE_LANES),
                  pl.BlockSpec((gb, N_HEADS, DK, DK), lambda i: (i, 0, 0, 0)),
                  pl.BlockSpec((gb, D_MLSTM), lambda i: (i, 0)),
                  pl.BlockSpec((CONV_BUF, gb, d_conv), lambda i: (0, i, 0))]
        + _weight_specs((g_head, w_dw, b_dw, g_cn, b_cn)),
        out_specs=(rspec(D_MLSTM), rspec(d_conv),
                   pl.BlockSpec((gb, N_HEADS, DK, DK), lambda i: (i, 0, 0, 0)),
                   pl.BlockSpec((gb, D_MLSTM), lambda i: (i, 0)),
                   rspec(GATE_LANES),
                   pl.BlockSpec((CONV_BUF, gb, d_conv), lambda i: (0, i, 0))),
        out_shape=(sd((R, D_MLSTM), f32), sd((R, d_conv), f32),
                   sd(c_state.shape, f32), sd(n_state.shape, f32),
                   sd((R, GATE_LANES), f32), sd(hist.shape, f32)),
        scratch_shapes=[pltpu.VMEM((rows, D_MLSTM), f32), pltpu.VMEM((rows, GATE_LANES), f32),
                        pltpu.VMEM((rows, D_MLSTM), f32)],
        compiler_params=pltpu.CompilerParams(
            dimension_semantics=("arbitrary",), vmem_limit_bytes=VMEM_LIMIT_BYTES),
        name="sample_recurrent",
    )(qkvo, gates, u, mrep, c_state, n_state, hist, g_head, w_dw, b_dw, g_cn, b_cn)


def _sample_tail_kernel(x_ref, hm_ref, cact_ref, p_ref, w_out_ref, g_ffn_ref, w_ff1_ref, w_ff2_ref,
                        g_ple_ref, w_gate_ref, w_proj_ref, g_final_ref, y_ref, acc_s, xn_s):
    j = pl.program_id(0)
    nb, seq_len, d_model = x_ref.shape
    rows = nb * seq_len

    @pl.when(j == 0)
    def _():
        x1 = _out_proj(x_ref[...].reshape(rows, d_model), hm_ref[...], cact_ref[...], w_out_ref)
        acc_s[...] = x1
        xn_s[...] = _rms(x1, g_ffn_ref[...]).astype(bf16)

    step = w_ff1_ref.shape[1]
    acc_s[...] += _ffn_part(xn_s[...], w_ff1_ref, w_ff2_ref, 0, step)

    @pl.when(j == pl.num_programs(0) - 1)
    def _():
        y = _ple_final(acc_s[...], p_ref[...].reshape(rows, p_ref.shape[2]), g_ple_ref, w_gate_ref,
                       w_proj_ref, g_final_ref)
        y_ref[...] = y.reshape(nb, seq_len, d_model)


def _sample_tail_call(x, hm, cact, p, ws):
    bs, seq_len, D = x.shape
    R = bs * seq_len
    w_out, g_ffn, w_ff1, w_ff2, g_ple, w_gate, w_proj, g_fin = ws
    d_ff = w_ff1.shape[1]
    step = d_ff // FFN_PIECES
    return pl.pallas_call(
        _sample_tail_kernel,
        grid=(FFN_PIECES,),
        in_specs=[_const_spec(x.shape), _const_spec(hm.shape), _const_spec(cact.shape),
                  _const_spec(p.shape), _const_spec(w_out.shape), _const_spec(g_ffn.shape),
                  pl.BlockSpec((w_ff1.shape[0], step), lambda j: (0, j)),
                  pl.BlockSpec((step, w_ff2.shape[1]), lambda j: (j, 0)),
                  _const_spec(g_ple.shape), _const_spec(w_gate.shape), _const_spec(w_proj.shape),
                  _const_spec(g_fin.shape)],
        out_specs=_const_spec(x.shape),
        out_shape=jax.ShapeDtypeStruct(x.shape, f32),
        scratch_shapes=[pltpu.VMEM((R, D), f32), pltpu.VMEM((R, D), bf16)],
        compiler_params=pltpu.CompilerParams(
            dimension_semantics=("arbitrary",), vmem_limit_bytes=VMEM_LIMIT_BYTES),
        name="sample_tail",
    )(x, hm, cact, p, *ws)


def _w_in_layout_kernel(d4, n_gate, wt_ref, o_ref):
    blk = TILE_ROWS
    rest = wt_ref.shape[0] - d4 - n_gate
    for c in range(0, d4, blk):
        o_ref[:, c:c + blk] = wt_ref[c:c + blk, :].T.astype(bf16)
    lane = lax.broadcasted_iota(jnp.int32, (wt_ref.shape[1], GATE_LANES), 1)
    gates = wt_ref[d4:d4 + GATE_LANES, :].T
    o_ref[:, d4:d4 + GATE_LANES] = jnp.where(lane < n_gate, gates, 0.0).astype(bf16)
    for c in range(0, rest, blk):
        o_ref[:, d4 + GATE_LANES + c:d4 + GATE_LANES + c + blk] = (
            wt_ref[d4 + n_gate + c:d4 + n_gate + c + blk, :].T.astype(bf16))


def _w_in_layout_call(wt, d4, n_gate):
    width, rows = wt.shape
    out_w = width - n_gate + GATE_LANES
    return pl.pallas_call(
        functools.partial(_w_in_layout_kernel, d4, n_gate),
        grid=(1,),
        in_specs=[_const_spec(wt.shape)],
        out_specs=_const_spec((rows, out_w)),
        out_shape=jax.ShapeDtypeStruct((rows, out_w), bf16),
        compiler_params=pltpu.CompilerParams(
            dimension_semantics=("arbitrary",), vmem_limit_bytes=VMEM_LIMIT_BYTES),
        name="w_in_layout",
    )(wt)


def _layer_weights(i, g_mix, w_in, b_in, g_head, w_dw, b_dw, g_cn, b_cn, w_out, g_ffn, w_ff1,
                   w_ff2, g_ple, w_ple_gate, w_ple_proj):
    row = lambda a: a[i].reshape(1, -1).astype(f32)
    d4 = 4 * D_MLSTM
    n_gate = 2 * N_HEADS
    pad = GATE_LANES - n_gate
    w_in_p = _w_in_layout_call(w_in[i].T, d4, n_gate)
    b = b_in[i]
    b_in_p = jnp.concatenate(
        [b[:d4], jnp.pad(b[d4:d4 + n_gate], (0, pad)), b[d4 + n_gate:]]).reshape(1, -1).astype(f32)
    w_dw_p = jnp.pad(w_dw[i].astype(f32), ((0, HIST_ROWS - CONV_WIDTH), (0, 0)))
    return dict(
        g_mix=row(g_mix), w_in=w_in_p, b_in=b_in_p, g_head=row(g_head), w_dw=w_dw_p,
        b_dw=row(b_dw), g_cn=row(g_cn), b_cn=row(b_cn), w_out=w_out[i].astype(bf16),
        g_ffn=row(g_ffn), w_ff1=w_ff1[i].astype(bf16), w_ff2=w_ff2[i].astype(bf16),
        g_ple=row(g_ple), w_gate=w_ple_gate[i].astype(bf16), w_proj=w_ple_proj[i].astype(bf16))


def kernel(x_prompt, x_sample, state_mlstm_C, state_mlstm_n, state_mlstm_m, cache_conv, p_prompt,
           p_sample, g_mix, w_in, b_in, g_head, w_dw, b_dw, g_cn, b_cn, w_out, g_ffn, w_ff1, w_ff2,
           g_ple, w_ple_gate, w_ple_proj, g_final):
    depth = w_in.shape[0]
    assert depth == 1, "the final norm is fused into the layer kernels"
    bs, seq_len, d_model = x_sample.shape
    assert seq_len & (seq_len - 1) == 0 and seq_len <= SUBLANES
    assert SAMPLE_GROUP & (SAMPLE_GROUP - 1) == 0 and SAMPLE_GROUP * seq_len == CHUNK
    g_fin = g_final.reshape(1, -1).astype(f32)

    i = 0
    lw = _layer_weights(i, g_mix, w_in, b_in, g_head, w_dw, b_dw, g_cn, b_cn, w_out, g_ffn,
                        w_ff1, w_ff2, g_ple, w_ple_gate, w_ple_proj)
    tail_ws = (lw["w_out"], lw["g_ffn"], lw["w_ff1"], lw["w_ff2"], lw["g_ple"], lw["w_gate"],
               lw["w_proj"], g_fin)

    prompt_ws = (lw["g_mix"], lw["w_in"], lw["b_in"], lw["g_head"], lw["w_dw"], lw["b_dw"],
                 lw["g_cn"], lw["b_cn"]) + tail_ws
    y_p, c_p, n_p, m_p, conv_p = _prompt_call(x_prompt, p_prompt[i], prompt_ws)
    m_p = m_p[:, 0, F_LANE:F_LANE + N_HEADS]

    qkvo, gates, u = _sample_in_call(x_sample, lw["g_mix"], lw["w_in"], lw["b_in"])
    m0 = jnp.pad(state_mlstm_m[i].astype(f32), ((0, 0), (F_LANE, GATE_LANES - F_LANE - N_HEADS)))
    mrep = jnp.repeat(m0, seq_len, axis=0)
    hm, cact, c_s, n_s, mt, conv_s = _sample_rec_call(
        seq_len, qkvo, gates, u, mrep, state_mlstm_C[i], state_mlstm_n[i].reshape(bs, -1),
        cache_conv[i].transpose(1, 0, 2), lw["g_head"], lw["w_dw"], lw["b_dw"], lw["g_cn"],
        lw["b_cn"])
    conv_s = conv_s.transpose(1, 0, 2)
    y_s = _sample_tail_call(x_sample, hm, cact, p_sample[i], tail_ws)
    m_s = mt[seq_len - 1::seq_len, F_LANE:F_LANE + N_HEADS]
    n_s = n_s.reshape(bs, N_HEADS, DK)

    stack = lambda a: a[None]
    return (y_p, y_s, stack(c_p), stack(n_p), stack(m_p), stack(conv_p),
            stack(c_s), stack(n_s), stack(m_s), stack(conv_s))
```

```python
import functools

import jax
import jax.numpy as jnp
from jax import lax
from jax.experimental import pallas as pl
from jax.experimental.pallas import tpu as pltpu

f32 = jnp.float32
bf16 = jnp.bfloat16

N_HEADS = 4
DK = 128
D_MLSTM = N_HEADS * DK
CONV_WIDTH = 31
CONV_BUF = CONV_WIDTH - 1
EPS = 1e-6
SUBLANES = 8
GATE_LANES = 128
F_LANE = N_HEADS
CHUNK = 128
TILE_ROWS = 256
HIST_ROWS = 32
SAMPLE_GROUP = 32
FFN_PIECES = 2
SAMPLE_FFN_STEPS = 4
SEQ_UNROLL = 8
VMEM_LIMIT_BYTES = 56 * 1024 * 1024


def _bdot(a, b):
    return jnp.dot(a, b, preferred_element_type=f32)


def _dot_nt(a, b):
    return lax.dot_general(a, b, (((1,), (1,)), ((), ())), preferred_element_type=f32)


def _dot_tn(a, b):
    return lax.dot_general(a, b, (((0,), (0,)), ((), ())), preferred_element_type=f32)


def _rms(x, g):
    y = x * lax.rsqrt(jnp.mean(x * x, axis=-1, keepdims=True) + EPS)
    return y * g


def _ln(x):
    mu = jnp.mean(x, axis=-1, keepdims=True)
    xc = x - mu
    return xc * lax.rsqrt(jnp.mean(xc * xc, axis=-1, keepdims=True) + EPS)


def _exact_left(sel16, x):
    hi = x.astype(bf16)
    r = x - hi.astype(f32)
    mid = r.astype(bf16)
    lo = (r - mid.astype(f32)).astype(bf16)
    return _bdot(sel16, hi) + _bdot(sel16, mid) + _bdot(sel16, lo)


def _lane_pick(rows, lane_ids):
    out = jnp.zeros((rows[0].shape[0], GATE_LANES), f32)
    for h, r in enumerate(rows):
        out = jnp.where(lane_ids == F_LANE + h, r, out)
    return out


def _in_proj(x, g_mix_ref, w_in_ref, b_in_ref):
    xn = _rms(x, g_mix_ref[...]).astype(bf16)

    def proj(a, b):
        return _bdot(xn, w_in_ref[:, a:b]) + b_in_ref[:, a:b]

    d = D_MLSTM
    q = proj(0, d)
    k = proj(d, 2 * d) * (DK ** -0.5)
    v = proj(2 * d, 3 * d)
    o = proj(3 * d, 4 * d)
    gates = proj(4 * d, 4 * d + GATE_LANES)
    c0 = 4 * d + GATE_LANES
    ga = proj(c0, c0 + d)
    gg = proj(c0 + d, c0 + 2 * d)
    u = ga * jax.nn.sigmoid(gg)
    return q, k, v, o, gates, u


def _gate_cumsum(gates, lcum16):
    lf = jax.nn.log_sigmoid(gates)
    bc = _exact_left(lcum16, lf)
    return lf, bc, gates.T, bc.T


def _chunk_weights(gates, bc, gates_t, bc_t, h, mask, m_prev):
    icol = gates[:, h:h + 1]
    bcol = bc[:, F_LANE + h:F_LANE + h + 1]
    irow = gates_t[h:h + 1, :]
    brow = bc_t[F_LANE + h:F_LANE + h + 1, :]
    logw = jnp.where(mask, (bcol - brow) + irow, -jnp.inf)
    m_intra = jnp.max(logw, axis=-1, keepdims=True)
    log_inter = bcol + m_prev
    m_t = jnp.maximum(log_inter, m_intra)
    w_intra = jnp.exp(logw - m_t)
    w_inter = jnp.exp(log_inter - m_t)
    return icol, bcol, m_t, w_intra, w_inter


def _v_aug(v):
    lane = lax.broadcasted_iota(jnp.int32, v.shape, 1)
    return jnp.concatenate([v, jnp.where(lane == 0, 1.0, 0.0)], axis=1)


def _intra(q16, k16, vaug16, w_intra):
    s = _dot_nt(q16, k16)
    a = (w_intra * s).astype(bf16)
    return _bdot(a, vaug16)


def _finish(nd, w_inter, num_inter, den_inter, m_t):
    num = nd[:, :DK] + w_inter * num_inter
    den = nd[:, DK:DK + 1] + w_inter * den_inter
    return num / jnp.maximum(jnp.abs(den), jnp.exp(-m_t))


def _head_out(h, o, g_head):
    return (_ln(h) * g_head) * jax.nn.sigmoid(o)


def _conv_post(c, g_cn_ref, b_cn_ref):
    c = _ln(c) * g_cn_ref[...] + b_cn_ref[...]
    return c * jax.nn.sigmoid(c)


def _out_proj(x, hm, c, w_out_ref):
    d = D_MLSTM
    return x + (_bdot(hm.astype(bf16), w_out_ref[0:d, :]) + _bdot(c.astype(bf16), w_out_ref[d:, :]))


def _ffn_part(xn16, w_ff1_ref, w_ff2_ref, j0, j1):
    f = jnp.maximum(_bdot(xn16, w_ff1_ref[:, j0:j1]), 0.0)
    return _bdot((f * f).astype(bf16), w_ff2_ref[j0:j1, :])


def _ple_final(x, p, g_ple_ref, w_gate_ref, w_proj_ref, g_final_ref):
    gate = jax.nn.sigmoid(_bdot(_rms(x, g_ple_ref[...]).astype(bf16), w_gate_ref[...]))
    x = x + gate * _bdot(p.astype(bf16), w_proj_ref[...])
    return _rms(x, g_final_ref[...])


def _cumsum_rows(x):
    sub = SUBLANES
    pos = lax.broadcasted_iota(jnp.int32, x.shape, 0) & (sub - 1)
    sh = 1
    while sh < sub:
        x = x + jnp.where(pos >= sh, pltpu.roll(x, sh, axis=0), 0.0)
        sh *= 2
    tiles = []
    carry = None
    for r0 in range(0, x.shape[0], sub):
        tile = x[r0:r0 + sub]
        if carry is not None:
            tile = tile + carry
        carry = tile[sub - 1:sub]
        tiles.append(tile)
    return jnp.concatenate(tiles, axis=0)


def _prompt_kernel(nt, xf_ref, xb_ref, p_ref, g_mix_ref, w_in_ref, b_in_ref, g_head_ref, w_dw_ref,
                   b_dw_ref, g_cn_ref, b_cn_ref, w_out_ref, g_ffn_ref, w_ff1_ref, w_ff2_ref,
                   g_ple_ref, w_gate_ref, w_proj_ref, g_final_ref,
                   y_ref, c_out_ref, n_out_ref, m_out_ref, conv_out_ref,
                   ct_s, n_s, m_s, ext_s, hm_s, cact_s, acc_s, xn_s, q_s, k_s, v_s, o_s, g_s, gc_s,
                   hh_s, cc_s):
    s = pl.program_id(0)
    n_tiles = pl.num_programs(0) - 1
    t = lax.rem(lax.rem(s, n_tiles), nt)
    tm = xf_ref.shape[0]
    off = HIST_ROWS - CONV_BUF

    @pl.when(s == 0)
    def _():
        hm_s[...] = jnp.zeros_like(hm_s)
        cact_s[...] = jnp.zeros_like(cact_s)

    @pl.when(t == 0)
    def _():
        ct_s[...] = jnp.zeros_like(ct_s)
        n_s[...] = jnp.zeros_like(n_s)
        m_s[...] = jnp.zeros_like(m_s)
        ext_s[0:HIST_ROWS, :] = jnp.zeros((HIST_ROWS, ext_s.shape[1]), f32)

    x1 = _out_proj(xb_ref[...], hm_s[...], cact_s[...], w_out_ref)
    acc_s[...] = x1
    xn_s[...] = _rms(x1, g_ffn_ref[...]).astype(bf16)
    q, k, v, o, gates, u = _in_proj(xf_ref[...], g_mix_ref, w_in_ref, b_in_ref)
    q_s[...] = q.astype(bf16)
    k_s[...] = k.astype(bf16)
    v_s[...] = v
    o_s[...] = o
    g_s[...] = gates
    ext_s[HIST_ROWS:HIST_ROWS + tm, :] = u

    rid = lax.broadcasted_iota(jnp.int32, (CHUNK, CHUNK), 0)
    cid = lax.broadcasted_iota(jnp.int32, (CHUNK, CHUNK), 1)
    causal = cid <= rid
    lane_row = lax.broadcasted_iota(jnp.int32, (1, GATE_LANES), 1)

    def ffn_piece(j0, j1):
        acc_s[...] += _ffn_part(xn_s[...], w_ff1_ref, w_ff2_ref, j0, j1)

    def conv_block(r0, rb):
        sub = SUBLANES
        acc = None
        for ph in range(sub):
            n_rows = rb if ph == 0 else rb + sub
            grp = None
            for j in range(ph, off + CONV_WIDTH, sub):
                if j < off:
                    continue
                term = w_dw_ref[j - off:j - off + 1, :] * ext_s[r0 + j - ph:r0 + j - ph + n_rows, :]
                grp = term if grp is None else grp + term
            part = grp if ph == 0 else grp[ph:ph + rb]
            acc = part if acc is None else acc + part
        cc_s[r0:r0 + rb, :] = acc + b_dw_ref[...]

    def mlstm_pair(ci, hp):
        rows = slice(ci * CHUNK, (ci + 1) * CHUNK)
        heads = (2 * hp, 2 * hp + 1)
        cols2 = slice(heads[0] * DK, (heads[1] + 1) * DK)
        gates_c = g_s[rows, :]
        if hp == 0:
            bc = _cumsum_rows(jax.nn.log_sigmoid(gates_c))
            gc_s[0] = bc
            gc_s[1] = gates_c.T
            gc_s[2] = bc.T
        m_row = m_s[...]
        q2 = q_s[rows, cols2]
        k2 = k_s[rows, cols2]
        v2 = v_s[rows, cols2]
        zero16 = jnp.zeros((CHUNK, DK), bf16)

        def blockdiag(a0, a1):
            return jnp.concatenate([jnp.concatenate([a0, zero16], axis=1),
                                    jnp.concatenate([zero16, a1], axis=1)], axis=0)

        s2 = _dot_nt(q2, blockdiag(k2[:, :DK], k2[:, DK:]))
        per = []
        for i, h in enumerate(heads):
            m_prev = m_row[:, F_LANE + h:F_LANE + h + 1]
            icol, bcol, m_t, w_intra, w_inter = _chunk_weights(
                gates_c, gc_s[0], gc_s[1], gc_s[2], h, causal, m_prev)
            a = w_intra * s2[:, i * DK:(i + 1) * DK]
            per.append((h, m_prev, icol, bcol, m_t, w_inter, a))
        a2 = jnp.concatenate([per[0][6], per[1][6]], axis=1).astype(bf16)
        v16 = v2.astype(bf16)
        num2 = _bdot(a2, blockdiag(v16[:, :DK], v16[:, DK:]))
        ct0, ct1 = ct_s[heads[0]], ct_s[heads[1]]
        qc2 = _bdot(q2, blockdiag(ct0.astype(bf16), ct1.astype(bf16)))
        m_new_row = m_row
        for i, (h, m_prev, icol, bcol, m_t, w_inter, a) in enumerate(per):
            hc = slice(i * DK, (i + 1) * DK)
            n_row = n_s[h:h + 1, :]
            den = (jnp.sum(a, axis=-1, keepdims=True)
                   + w_inter * jnp.sum(q2[:, hc].astype(f32) * n_row, axis=-1, keepdims=True))
            num = num2[:, hc] + w_inter * qc2[:, hc]
            hh_s[rows, h * DK:(h + 1) * DK] = num / jnp.maximum(jnp.abs(den), jnp.exp(-m_t))
            m_new = m_t[CHUNK - 1:CHUNK]
            b_last = bcol[CHUNK - 1:CHUNK]
            gcol = jnp.exp(((b_last - bcol) + icol) - m_new)
            decay = jnp.exp(b_last + m_prev - m_new)
            ct = ct0 if i == 0 else ct1
            ct_s[h] = decay * ct + _dot_tn(k2[:, hc], (gcol * v2[:, hc]).astype(bf16))
            n_s[h:h + 1, :] = decay * n_row + jnp.sum(gcol * k2[:, hc].astype(f32), axis=0,
                                                      keepdims=True)
            m_new_row = jnp.where(lane_row == F_LANE + h, m_new, m_new_row)
        m_s[...] = m_new_row

    n_piece = (tm // CHUNK) * N_HEADS
    d_ff = w_ff1_ref.shape[1]
    fstep, rb = d_ff // FFN_PIECES, tm // n_piece
    for j in range(n_piece):
        if j % (n_piece // FFN_PIECES) == 0:
            jf = j // (n_piece // FFN_PIECES)
            ffn_piece(jf * fstep, (jf + 1) * fstep)
        conv_block(j * rb, rb)
        if j % 2 == 0:
            mlstm_pair(j // N_HEADS, (j % N_HEADS) // 2)

    y_ref[...] = _ple_final(acc_s[...], p_ref[...], g_ple_ref, w_gate_ref, w_proj_ref,
                            g_final_ref)
    hm = jnp.concatenate(
        [_head_out(hh_s[:, h * DK:(h + 1) * DK], o_s[:, h * DK:(h + 1) * DK],
                   g_head_ref[:, h * DK:(h + 1) * DK]) for h in range(N_HEADS)], axis=1)
    hm_s[...] = hm.astype(bf16)
    cact_s[...] = _conv_post(cc_s[...], g_cn_ref, b_cn_ref).astype(bf16)
    ext_s[0:HIST_ROWS, :] = ext_s[tm:tm + HIST_ROWS, :]

    @pl.when(jnp.logical_and(t == nt - 1, s < n_tiles))
    def _():
        for h in range(N_HEADS):
            c_out_ref[h] = ct_s[h].T
        n_out_ref[...] = n_s[0:N_HEADS, :]
        m_out_ref[...] = jnp.broadcast_to(m_s[...], m_out_ref.shape)
        conv_out_ref[...] = ext_s[off:HIST_ROWS, :]


def _const_spec(shape):
    return pl.BlockSpec(shape, lambda *_: (0,) * len(shape), pipeline_mode=pl.Buffered(1))


def _weight_specs(ws):
    return [_const_spec(w.shape) for w in ws]


def _prompt_call(x, p, ws):
    B, T, D = x.shape
    tm = TILE_ROWS
    nt = T // tm
    n_tiles = B * nt
    d_conv = ws[4].shape[1]
    front = lambda s: (lax.rem(s, n_tiles) // nt, lax.rem(lax.rem(s, n_tiles), nt), 0)
    back = lambda s: (jnp.maximum(s - 1, 0) // nt, lax.rem(jnp.maximum(s - 1, 0), nt), 0)
    state = lambda s: (jnp.minimum(s, n_tiles - 1) // nt, 0, 0)
    sd = jax.ShapeDtypeStruct
    rows_buf = lambda w, dt: pltpu.VMEM((tm, w), dt)
    return pl.pallas_call(
        functools.partial(_prompt_kernel, nt),
        grid=(n_tiles + 1,),
        in_specs=[pl.BlockSpec((None, tm, D), front), pl.BlockSpec((None, tm, D), back),
                  pl.BlockSpec((None, tm, p.shape[2]), back)] + _weight_specs(ws),
        out_specs=(
            pl.BlockSpec((None, tm, D), back),
            pl.BlockSpec((None, N_HEADS, DK, DK), lambda s: state(s) + (0,)),
            pl.BlockSpec((None, N_HEADS, DK), state),
            pl.BlockSpec((None, SUBLANES, GATE_LANES), state),
            pl.BlockSpec((None, CONV_BUF, d_conv), state),
        ),
        out_shape=(
            sd((B, T, D), f32),
            sd((B, N_HEADS, DK, DK), f32),
            sd((B, N_HEADS, DK), f32),
            sd((B, SUBLANES, GATE_LANES), f32),
            sd((B, CONV_BUF, d_conv), f32),
        ),
        scratch_shapes=[
            pltpu.VMEM((N_HEADS, DK, DK), f32),
            pltpu.VMEM((SUBLANES, DK), f32),
            pltpu.VMEM((1, GATE_LANES), f32),
            pltpu.VMEM((HIST_ROWS + tm, d_conv), f32),
            rows_buf(D_MLSTM, bf16), rows_buf(d_conv, bf16),
            rows_buf(D, f32), rows_buf(D, bf16),
            rows_buf(D_MLSTM, bf16), rows_buf(D_MLSTM, bf16),
            rows_buf(D_MLSTM, f32), rows_buf(D_MLSTM, f32),
            rows_buf(GATE_LANES, f32),
            pltpu.VMEM((3, CHUNK, GATE_LANES), f32),
            rows_buf(D_MLSTM, f32), rows_buf(d_conv, f32),
        ],
        compiler_params=pltpu.CompilerParams(
            dimension_semantics=("arbitrary",), vmem_limit_bytes=VMEM_LIMIT_BYTES),
        name="prompt_layer",
    )(x, x, p, *ws)


def _sample_in_kernel(x_ref, g_mix_ref, w_in_ref, b_in_ref, qkvo_ref, gates_ref, u_ref):
    nb, seq_len, d_model = x_ref.shape
    x = x_ref[...].reshape(nb * seq_len, d_model)
    q, k, v, o, gates, u = _in_proj(x, g_mix_ref, w_in_ref, b_in_ref)
    qkvo_ref[...] = jnp.concatenate([q, k, v, o], axis=1)
    gates_ref[...] = gates
    u_ref[...] = u


def _sample_in_call(x, g_mix, w_in, b_in):
    bs, seq_len, D = x.shape
    R = bs * seq_len
    tm = TILE_ROWS
    d_conv = (w_in.shape[1] - 4 * D_MLSTM - GATE_LANES) // 2
    sd = jax.ShapeDtypeStruct
    return pl.pallas_call(
        _sample_in_kernel,
        grid=(R // tm,),
        in_specs=[pl.BlockSpec((tm // seq_len, seq_len, D), lambda i: (i, 0, 0))]
        + _weight_specs((g_mix, w_in, b_in)),
        out_specs=(pl.BlockSpec((tm, 4 * D_MLSTM), lambda i: (i, 0)),
                   pl.BlockSpec((tm, GATE_LANES), lambda i: (i, 0)),
                   pl.BlockSpec((tm, d_conv), lambda i: (i, 0))),
        out_shape=(sd((R, 4 * D_MLSTM), f32), sd((R, GATE_LANES), f32), sd((R, d_conv), f32)),
        compiler_params=pltpu.CompilerParams(
            dimension_semantics=("arbitrary",), vmem_limit_bytes=VMEM_LIMIT_BYTES),
        name="sample_in_proj",
    )(x, g_mix, w_in, b_in)


def _sample_rec_kernel(seq_len, qkvo_ref, gates_ref, u_ref, mrep_ref, c_ref, n_ref, hist_ref,
                       g_head_ref, w_dw_ref, b_dw_ref, g_cn_ref, b_cn_ref,
                       hm_ref, cact_ref, c_out_ref, n_out_ref, mt_ref, hist_out_ref,
                       numi_s, dec_s, gk_s):
    rows = qkvo_ref.shape[0]
    nb = rows // seq_len
    shift = seq_len.bit_length() - 1
    d = D_MLSTM

    rid = lax.broadcasted_iota(jnp.int32, (rows, rows), 0)
    cid = lax.broadcasted_iota(jnp.int32, (rows, rows), 1)
    same = (rid >> shift) == (cid >> shift)
    mask = jnp.logical_and(same, cid <= rid)
    lcum16 = jnp.where(mask, 1.0, 0.0).astype(bf16)
    bsum16 = jnp.where(same, 1.0, 0.0).astype(bf16)
    plast16 = jnp.where(cid == ((rid >> shift) << shift) + (seq_len - 1), 1.0, 0.0).astype(bf16)
    lane_g = lax.broadcasted_iota(jnp.int32, (rows, GATE_LANES), 1)

    gates = gates_ref[...]
    lf, bc, gates_t, bc_t = _gate_cumsum(gates, lcum16)
    blast = _exact_left(bsum16, lf)
    mrep = mrep_ref[...]

    q = qkvo_ref[:, 0:d]
    k = qkvo_ref[:, d:2 * d]
    v = qkvo_ref[:, 2 * d:3 * d]
    o = qkvo_ref[:, 3 * d:4 * d]
    q16 = q.astype(bf16)
    k16 = k.astype(bf16)

    per_head = []
    for h in range(N_HEADS):
        cols = slice(h * DK, (h + 1) * DK)
        m_prev = mrep[:, F_LANE + h:F_LANE + h + 1]
        icol, bcol, m_t, w_intra, w_inter = _chunk_weights(gates, bc, gates_t, bc_t, h, mask, m_prev)
        vaug = _v_aug(v[:, cols])
        nd = _intra(q16[:, cols], k16[:, cols], vaug.astype(bf16), w_intra)
        per_head.append((icol, bcol, m_t, w_inter, nd))
    mt_all = _lane_pick([ph[2] for ph in per_head], lane_g)
    mt_ref[...] = mt_all
    mnew = _exact_left(plast16, mt_all)
    dec_s[...] = jnp.exp(blast + mrep - mnew)
    gvt16 = []
    for h in range(N_HEADS):
        cols = slice(h * DK, (h + 1) * DK)
        icol, bcol, _, _, _ = per_head[h]
        lanes = slice(F_LANE + h, F_LANE + h + 1)
        gcol = jnp.exp(((blast[:, lanes] - bcol) + icol) - mnew[:, lanes])
        gvt16.append((gcol * v[:, cols]).T.astype(bf16))
        gk_s[:, cols] = gcol * k[:, cols]

    rsel = lax.broadcasted_iota(jnp.int32, (rows, nb), 0)
    bsel = lax.broadcasted_iota(jnp.int32, (rows, nb), 1)
    expand16 = jnp.where((rsel >> shift) == bsel, 1.0, 0.0).astype(bf16)
    nrep = _exact_left(expand16, n_ref[...])

    grp = lax.broadcasted_iota(jnp.int32, (rows, DK), 0) >> shift
    numi_s[...] = jnp.zeros_like(numi_s)

    sub = SUBLANES
    tile_shift = (sub // seq_len).bit_length() - 1
    in_tile = lax.broadcasted_iota(jnp.int32, (sub, d), 0) >> shift

    def per_seq(b, carry):
        mb = grp == b
        r0 = pl.multiple_of((b >> tile_shift) * sub, sub)
        mine = in_tile == (b & (sub // seq_len - 1))
        q8 = qkvo_ref[pl.ds(r0, sub), 0:d]
        dec_b = dec_s[pl.ds(b * seq_len, 1), :]
        n_b = n_ref[pl.ds(b, 1), :]
        gk8 = jnp.sum(jnp.where(mine, gk_s[pl.ds(r0, sub), :], 0.0), axis=0, keepdims=True)
        n_new = []
        num_i = []
        for h in range(N_HEADS):
            cols = slice(h * DK, (h + 1) * DK)
            cb = c_ref[b, h]
            num_i.append(_dot_nt(q8[:, cols].astype(bf16), cb.astype(bf16)))
            kb = jnp.where(mb, k[:, cols], 0.0).astype(bf16)
            dec = dec_b[:, F_LANE + h:F_LANE + h + 1]
            c_out_ref[b, h] = dec * cb + _bdot(gvt16[h], kb)
            n_new.append(dec * n_b[:, cols] + gk8[:, cols])
        numi_s[pl.ds(r0, sub), :] = jnp.where(mine, jnp.concatenate(num_i, axis=1),
                                              numi_s[pl.ds(r0, sub), :])
        n_out_ref[pl.ds(b, 1), :] = jnp.concatenate(n_new, axis=1)
        return carry

    lax.fori_loop(0, nb, per_seq, 0, unroll=SEQ_UNROLL)

    hm = []
    for h in range(N_HEADS):
        cols = slice(h * DK, (h + 1) * DK)
        _, _, m_t, w_inter, nd = per_head[h]
        den_i = jnp.sum(q[:, cols] * nrep[:, cols], axis=-1, keepdims=True)
        hh = _finish(nd, w_inter, numi_s[:, cols], den_i, m_t)
        hm.append(_head_out(hh, o[:, cols], g_head_ref[:, cols]))
    hm_ref[...] = jnp.concatenate(hm, axis=1)

    rt = lax.broadcasted_iota(jnp.int32, (rows, rows), 0)
    rbm = lax.broadcasted_iota(jnp.int32, (rows, rows), 1)
    nb_shift = nb.bit_length() - 1
    to_tm16 = jnp.where(rbm == ((rt & (nb - 1)) << shift) + (rt >> nb_shift), 1.0, 0.0).astype(bf16)
    to_bm16 = jnp.where(rt == ((rbm & (nb - 1)) << shift) + (rbm >> nb_shift), 1.0, 0.0).astype(bf16)
    u_tm = _exact_left(to_tm16, u_ref[...])
    u_steps = [u_tm[s * nb:(s + 1) * nb] for s in range(seq_len)]
    c_steps = []
    for t in range(seq_len):
        acc = None
        for j in range(t, CONV_BUF):
            term = w_dw_ref[j - t:j - t + 1, :] * hist_ref[j]
            acc = term if acc is None else acc + term
        for s in range(t + 1):
            acc = acc + w_dw_ref[CONV_BUF + s - t:CONV_BUF + s - t + 1, :] * u_steps[s]
        c_steps.append(acc + b_dw_ref[...])
    cact_tm = _conv_post(jnp.concatenate(c_steps, axis=0), g_cn_ref, b_cn_ref)
    cact_ref[...] = _bdot(to_bm16, cact_tm.astype(bf16))
    hist_out_ref[0:CONV_BUF - seq_len] = hist_ref[seq_len:CONV_BUF]
    for s in range(seq_len):
        hist_out_ref[CONV_BUF - seq_len + s] = u_steps[s]


def _sample_rec_call(seq_len, qkvo, gates, u, mrep, c_state, n_state, hist,
                     g_head, w_dw, b_dw, g_cn, b_cn):
    R = qkvo.shape[0]
    nb_total = c_state.shape[0]
    gb = SAMPLE_GROUP
    rows = gb * seq_len
    d_conv = u.shape[1]
    sd = jax.ShapeDtypeStruct
    rspec = lambda w: pl.BlockSpec((rows, w), lambda i: (i, 0))
    return pl.pallas_call(
        functools.partial(_sample_rec_kernel, seq_len),
        grid=(nb_total // gb,),
        in_specs=[rspec(qkvo.shape[1]), rspec(GATE_LANES), rspec(d_conv), rspec(GATE_LANES),
                  pl.BlockSpec((gb, N_HEADS, DK, DK), lambda i: (i, 0, 0, 0)),
                  pl.BlockSpec((gb, D_MLSTM), lambda i: (i, 0)),
                  pl.BlockSpec((CONV_BUF, gb, d_conv), lambda i: (0, i, 0))]
        + _weight_specs((g_head, w_dw, b_dw, g_cn, b_cn)),
        out_specs=(rspec(D_MLSTM), rspec(d_conv),
                   pl.BlockSpec((gb, N_HEADS, DK, DK), lambda i: (i, 0, 0, 0)),
                   pl.BlockSpec((gb, D_MLSTM), lambda i: (i, 0)),
                   rspec(GATE_LANES),
                   pl.BlockSpec((CONV_BUF, gb, d_conv), lambda i: (0, i, 0))),
        out_shape=(sd((R, D_MLSTM), f32), sd((R, d_conv), f32),
                   sd(c_state.shape, f32), sd(n_state.shape, f32),
                   sd((R, GATE_LANES), f32), sd(hist.shape, f32)),
        scratch_shapes=[pltpu.VMEM((rows, D_MLSTM), f32), pltpu.VMEM((rows, GATE_LANES), f32),
                        pltpu.VMEM((rows, D_MLSTM), f32)],
        compiler_params=pltpu.CompilerParams(
            dimension_semantics=("arbitrary",), vmem_limit_bytes=VMEM_LIMIT_BYTES),
        name="sample_recurrent",
    )(qkvo, gates, u, mrep, c_state, n_state, hist, g_head, w_dw, b_dw, g_cn, b_cn)


def _sample_tail_kernel(x_ref, hm_ref, cact_ref, p_ref, w_out_ref, g_ffn_ref, w_ff1_ref, w_ff2_ref,
                        g_ple_ref, w_gate_ref, w_proj_ref, g_final_ref, y_ref, acc_s, xn_s):
    j = pl.program_id(0)
    nb, seq_len, d_model = x_ref.shape
    rows = nb * seq_len

    @pl.when(j == 0)
    def _():
        x1 = _out_proj(x_ref[...].reshape(rows, d_model), hm_ref[...], cact_ref[...], w_out_ref)
        acc_s[...] = x1
        xn_s[...] = _rms(x1, g_ffn_ref[...]).astype(bf16)

    step = w_ff1_ref.shape[1]
    acc_s[...] += _ffn_part(xn_s[...], w_ff1_ref, w_ff2_ref, 0, step)

    @pl.when(j == pl.num_programs(0) - 1)
    def _():
        y = _ple_final(acc_s[...], p_ref[...].reshape(rows, p_ref.shape[2]), g_ple_ref, w_gate_ref,
                       w_proj_ref, g_final_ref)
        y_ref[...] = y.reshape(nb, seq_len, d_model)


def _sample_tail_call(x, hm, cact, p, ws):
    bs, seq_len, D = x.shape
    R = bs * seq_len
    w_out, g_ffn, w_ff1, w_ff2, g_ple, w_gate, w_proj, g_fin = ws
    d_ff = w_ff1.shape[1]
    step = d_ff // SAMPLE_FFN_STEPS
    return pl.pallas_call(
        _sample_tail_kernel,
        grid=(SAMPLE_FFN_STEPS,),
        in_specs=[_const_spec(x.shape), _const_spec(hm.shape), _const_spec(cact.shape),
                  _const_spec(p.shape), _const_spec(w_out.shape), _const_spec(g_ffn.shape),
                  pl.BlockSpec((w_ff1.shape[0], step), lambda j: (0, j)),
                  pl.BlockSpec((step, w_ff2.shape[1]), lambda j: (j, 0)),
                  _const_spec(g_ple.shape), _const_spec(w_gate.shape), _const_spec(w_proj.shape),
                  _const_spec(g_fin.shape)],
        out_specs=_const_spec(x.shape),
        out_shape=jax.ShapeDtypeStruct(x.shape, f32),
        scratch_shapes=[pltpu.VMEM((R, D), f32), pltpu.VMEM((R, D), bf16)],
        compiler_params=pltpu.CompilerParams(
            dimension_semantics=("arbitrary",), vmem_limit_bytes=VMEM_LIMIT_BYTES),
        name="sample_tail",
    )(x, hm, cact, p, *ws)


def _w_in_layout_kernel(d4, n_gate, wt_ref, o_ref):
    blk = TILE_ROWS
    rest = wt_ref.shape[0] - d4 - n_gate
    for c in range(0, d4, blk):
        o_ref[:, c:c + blk] = wt_ref[c:c + blk, :].T.astype(bf16)
    lane = lax.broadcasted_iota(jnp.int32, (wt_ref.shape[1], GATE_LANES), 1)
    gates = wt_ref[d4:d4 + GATE_LANES, :].T
    o_ref[:, d4:d4 + GATE_LANES] = jnp.where(lane < n_gate, gates, 0.0).astype(bf16)
    for c in range(0, rest, blk):
        o_ref[:, d4 + GATE_LANES + c:d4 + GATE_LANES + c + blk] = (
            wt_ref[d4 + n_gate + c:d4 + n_gate + c + blk, :].T.astype(bf16))


def _w_in_layout_call(wt, d4, n_gate):
    width, rows = wt.shape
    out_w = width - n_gate + GATE_LANES
    return pl.pallas_call(
        functools.partial(_w_in_layout_kernel, d4, n_gate),
        grid=(1,),
        in_specs=[_const_spec(wt.shape)],
        out_specs=_const_spec((rows, out_w)),
        out_shape=jax.ShapeDtypeStruct((rows, out_w), bf16),
        compiler_params=pltpu.CompilerParams(
            dimension_semantics=("arbitrary",), vmem_limit_bytes=VMEM_LIMIT_BYTES),
        name="w_in_layout",
    )(wt)


def _layer_weights(i, g_mix, w_in, b_in, g_head, w_dw, b_dw, g_cn, b_cn, w_out, g_ffn, w_ff1,
                   w_ff2, g_ple, w_ple_gate, w_ple_proj):
    row = lambda a: a[i].reshape(1, -1).astype(f32)
    d4 = 4 * D_MLSTM
    n_gate = 2 * N_HEADS
    pad = GATE_LANES - n_gate
    w_in_p = _w_in_layout_call(w_in[i].T, d4, n_gate)
    b = b_in[i]
    b_in_p = jnp.concatenate(
        [b[:d4], jnp.pad(b[d4:d4 + n_gate], (0, pad)), b[d4 + n_gate:]]).reshape(1, -1).astype(f32)
    w_dw_p = jnp.pad(w_dw[i].astype(f32), ((0, HIST_ROWS - CONV_WIDTH), (0, 0)))
    return dict(
        g_mix=row(g_mix), w_in=w_in_p, b_in=b_in_p, g_head=row(g_head), w_dw=w_dw_p,
        b_dw=row(b_dw), g_cn=row(g_cn), b_cn=row(b_cn), w_out=w_out[i].astype(bf16),
        g_ffn=row(g_ffn), w_ff1=w_ff1[i].astype(bf16), w_ff2=w_ff2[i].astype(bf16),
        g_ple=row(g_ple), w_gate=w_ple_gate[i].astype(bf16), w_proj=w_ple_proj[i].astype(bf16))


def kernel(x_prompt, x_sample, state_mlstm_C, state_mlstm_n, state_mlstm_m, cache_conv, p_prompt,
           p_sample, g_mix, w_in, b_in, g_head, w_dw, b_dw, g_cn, b_cn, w_out, g_ffn, w_ff1, w_ff2,
           g_ple, w_ple_gate, w_ple_proj, g_final):
    depth = w_in.shape[0]
    assert depth == 1, "the final norm is fused into the layer kernels"
    bs, seq_len, d_model = x_sample.shape
    assert seq_len & (seq_len - 1) == 0 and seq_len <= SUBLANES
    assert SAMPLE_GROUP & (SAMPLE_GROUP - 1) == 0 and SAMPLE_GROUP * seq_len == CHUNK
    g_fin = g_final.reshape(1, -1).astype(f32)

    i = 0
    lw = _layer_weights(i, g_mix, w_in, b_in, g_head, w_dw, b_dw, g_cn, b_cn, w_out, g_ffn,
                        w_ff1, w_ff2, g_ple, w_ple_gate, w_ple_proj)
    tail_ws = (lw["w_out"], lw["g_ffn"], lw["w_ff1"], lw["w_ff2"], lw["g_ple"], lw["w_gate"],
               lw["w_proj"], g_fin)

    prompt_ws = (lw["g_mix"], lw["w_in"], lw["b_in"], lw["g_head"], lw["w_dw"], lw["b_dw"],
                 lw["g_cn"], lw["b_cn"]) + tail_ws
    y_p, c_p, n_p, m_p, conv_p = _prompt_call(x_prompt, p_prompt[i], prompt_ws)
    m_p = m_p[:, 0, F_LANE:F_LANE + N_HEADS]

    qkvo, gates, u = _sample_in_call(x_sample, lw["g_mix"], lw["w_in"], lw["b_in"])
    m0 = jnp.pad(state_mlstm_m[i].astype(f32), ((0, 0), (F_LANE, GATE_LANES - F_LANE - N_HEADS)))
    mrep = jnp.repeat(m0, seq_len, axis=0)
    hm, cact, c_s, n_s, mt, conv_s = _sample_rec_call(
        seq_len, qkvo, gates, u, mrep, state_mlstm_C[i], state_mlstm_n[i].reshape(bs, -1),
        cache_conv[i].transpose(1, 0, 2), lw["g_head"], lw["w_dw"], lw["b_dw"], lw["g_cn"],
        lw["b_cn"])
    conv_s = conv_s.transpose(1, 0, 2)
    y_s = _sample_tail_call(x_sample, hm, cact, p_sample[i], tail_ws)
    m_s = mt[seq_len - 1::seq_len, F_LANE:F_LANE + N_HEADS]
    n_s = n_s.reshape(bs, N_HEADS, DK)

    stack = lambda a: a[None]
    return (y_p, y_s, stack(c_p), stack(n_p), stack(m_p), stack(conv_p),
            stack(c_s), stack(n_s), stack(m_s), stack(conv_s))
```

```python
import functools

import jax
import jax.numpy as jnp
from jax import lax
from jax.experimental import pallas as pl
from jax.experimental.pallas import tpu as pltpu

f32 = jnp.float32
bf16 = jnp.bfloat16

N_HEADS = 4
DK = 128
D_MLSTM = N_HEADS * DK
CONV_WIDTH = 31
CONV_BUF = CONV_WIDTH - 1
EPS = 1e-6
SUBLANES = 8
GATE_LANES = 128
F_LANE = N_HEADS
CHUNK = 128
TILE_ROWS = 256
PROMPT_TILE_ROWS = 512
HIST_ROWS = 32
SAMPLE_GROUP = 32
FFN_PIECES = 4
SEQ_UNROLL = 8
VMEM_LIMIT_BYTES = 60 * 1024 * 1024


def _bdot(a, b):
    return jnp.dot(a, b, preferred_element_type=f32)


def _dot_nt(a, b):
    return lax.dot_general(a, b, (((1,), (1,)), ((), ())), preferred_element_type=f32)


def _dot_tn(a, b):
    return lax.dot_general(a, b, (((0,), (0,)), ((), ())), preferred_element_type=f32)


def _rms(x, g):
    y = x * lax.rsqrt(jnp.mean(x * x, axis=-1, keepdims=True) + EPS)
    return y * g


def _ln(x):
    mu = jnp.mean(x, axis=-1, keepdims=True)
    xc = x - mu
    return xc * lax.rsqrt(jnp.mean(xc * xc, axis=-1, keepdims=True) + EPS)


def _exact_left(sel16, x):
    hi = x.astype(bf16)
    r = x - hi.astype(f32)
    mid = r.astype(bf16)
    lo = (r - mid.astype(f32)).astype(bf16)
    return _bdot(sel16, hi) + _bdot(sel16, mid) + _bdot(sel16, lo)


def _lane_pick(rows, lane_ids):
    out = jnp.zeros((rows[0].shape[0], GATE_LANES), f32)
    for h, r in enumerate(rows):
        out = jnp.where(lane_ids == F_LANE + h, r, out)
    return out


def _in_proj(x, g_mix_ref, w_in_ref, b_in_ref):
    xn = _rms(x, g_mix_ref[...]).astype(bf16)

    def proj(a, b):
        return _bdot(xn, w_in_ref[:, a:b]) + b_in_ref[:, a:b]

    d = D_MLSTM
    q = proj(0, d)
    k = proj(d, 2 * d) * (DK ** -0.5)
    v = proj(2 * d, 3 * d)
    o = proj(3 * d, 4 * d)
    gates = proj(4 * d, 4 * d + GATE_LANES)
    c0 = 4 * d + GATE_LANES
    ga = proj(c0, c0 + d)
    gg = proj(c0 + d, c0 + 2 * d)
    u = ga * jax.nn.sigmoid(gg)
    return q, k, v, o, gates, u


def _gate_cumsum(gates, lcum16):
    lf = jax.nn.log_sigmoid(gates)
    bc = _exact_left(lcum16, lf)
    return lf, bc, gates.T, bc.T


def _chunk_weights(gates, bc, gates_t, bc_t, h, mask, m_prev):
    icol = gates[:, h:h + 1]
    bcol = bc[:, F_LANE + h:F_LANE + h + 1]
    irow = gates_t[h:h + 1, :]
    brow = bc_t[F_LANE + h:F_LANE + h + 1, :]
    logw = jnp.where(mask, (bcol - brow) + irow, -jnp.inf)
    m_intra = jnp.max(logw, axis=-1, keepdims=True)
    log_inter = bcol + m_prev
    m_t = jnp.maximum(log_inter, m_intra)
    w_intra = jnp.exp(logw - m_t)
    w_inter = jnp.exp(log_inter - m_t)
    return icol, bcol, m_t, w_intra, w_inter


def _v_aug(v):
    lane = lax.broadcasted_iota(jnp.int32, v.shape, 1)
    return jnp.concatenate([v, jnp.where(lane == 0, 1.0, 0.0)], axis=1)


def _intra(q16, k16, vaug16, w_intra):
    s = _dot_nt(q16, k16)
    a = (w_intra * s).astype(bf16)
    return _bdot(a, vaug16)


def _finish(nd, w_inter, num_inter, den_inter, m_t):
    num = nd[:, :DK] + w_inter * num_inter
    den = nd[:, DK:DK + 1] + w_inter * den_inter
    return num / jnp.maximum(jnp.abs(den), jnp.exp(-m_t))


def _head_out(h, o, g_head):
    return (_ln(h) * g_head) * jax.nn.sigmoid(o)


def _conv_post(c, g_cn_ref, b_cn_ref):
    c = _ln(c) * g_cn_ref[...] + b_cn_ref[...]
    return c * jax.nn.sigmoid(c)


def _out_proj(x, hm, c, w_out_ref):
    d = D_MLSTM
    return x + (_bdot(hm.astype(bf16), w_out_ref[0:d, :]) + _bdot(c.astype(bf16), w_out_ref[d:, :]))


def _ffn_part(xn16, w_ff1_ref, w_ff2_ref, j0, j1):
    f = jnp.maximum(_bdot(xn16, w_ff1_ref[:, j0:j1]), 0.0)
    return _bdot((f * f).astype(bf16), w_ff2_ref[j0:j1, :])


def _ple_final(x, p, g_ple_ref, w_gate_ref, w_proj_ref, g_final_ref):
    gate = jax.nn.sigmoid(_bdot(_rms(x, g_ple_ref[...]).astype(bf16), w_gate_ref[...]))
    x = x + gate * _bdot(p.astype(bf16), w_proj_ref[...])
    return _rms(x, g_final_ref[...])


def _cumsum_rows(x):
    sub = SUBLANES
    pos = lax.broadcasted_iota(jnp.int32, x.shape, 0) & (sub - 1)
    sh = 1
    while sh < sub:
        x = x + jnp.where(pos >= sh, pltpu.roll(x, sh, axis=0), 0.0)
        sh *= 2
    tiles = []
    carry = None
    for r0 in range(0, x.shape[0], sub):
        tile = x[r0:r0 + sub]
        if carry is not None:
            tile = tile + carry
        carry = tile[sub - 1:sub]
        tiles.append(tile)
    return jnp.concatenate(tiles, axis=0)


def _prompt_kernel(nt, xf_ref, xb_ref, p_ref, g_mix_ref, w_in_ref, b_in_ref, g_head_ref, w_dw_ref,
                   b_dw_ref, g_cn_ref, b_cn_ref, w_out_ref, g_ffn_ref, w_ff1_ref, w_ff2_ref,
                   g_ple_ref, w_gate_ref, w_proj_ref, g_final_ref,
                   y_ref, c_out_ref, n_out_ref, m_out_ref, conv_out_ref,
                   ct_s, n_s, m_s, ext_s, hm_s, cact_s, acc_s, xn_s, q_s, k_s, v_s, o_s, g_s, gc_s,
                   hh_s, cc_s):
    s = pl.program_id(0)
    n_tiles = pl.num_programs(0) - 1
    t = lax.rem(lax.rem(s, n_tiles), nt)
    tm = xf_ref.shape[0]
    off = HIST_ROWS - CONV_BUF

    @pl.when(s == 0)
    def _():
        hm_s[...] = jnp.zeros_like(hm_s)
        cact_s[...] = jnp.zeros_like(cact_s)

    @pl.when(t == 0)
    def _():
        ct_s[...] = jnp.zeros_like(ct_s)
        n_s[...] = jnp.zeros_like(n_s)
        m_s[...] = jnp.zeros_like(m_s)
        ext_s[0:HIST_ROWS, :] = jnp.zeros((HIST_ROWS, ext_s.shape[1]), f32)

    x1 = _out_proj(xb_ref[...], hm_s[...], cact_s[...], w_out_ref)
    acc_s[...] = x1
    xn_s[...] = _rms(x1, g_ffn_ref[...]).astype(bf16)
    q, k, v, o, gates, u = _in_proj(xf_ref[...], g_mix_ref, w_in_ref, b_in_ref)
    q_s[...] = q.astype(bf16)
    k_s[...] = k.astype(bf16)
    v_s[...] = v
    o_s[...] = o
    g_s[...] = gates
    ext_s[HIST_ROWS:HIST_ROWS + tm, :] = u

    rid = lax.broadcasted_iota(jnp.int32, (CHUNK, CHUNK), 0)
    cid = lax.broadcasted_iota(jnp.int32, (CHUNK, CHUNK), 1)
    causal = cid <= rid
    lane_row = lax.broadcasted_iota(jnp.int32, (1, GATE_LANES), 1)

    def ffn_piece(j0, j1):
        acc_s[...] += _ffn_part(xn_s[...], w_ff1_ref, w_ff2_ref, j0, j1)

    def conv_block(r0, rb):
        sub = SUBLANES
        acc = None
        for ph in range(sub):
            n_rows = rb if ph == 0 else rb + sub
            grp = None
            for j in range(ph, off + CONV_WIDTH, sub):
                if j < off:
                    continue
                term = w_dw_ref[j - off:j - off + 1, :] * ext_s[r0 + j - ph:r0 + j - ph + n_rows, :]
                grp = term if grp is None else grp + term
            part = grp if ph == 0 else grp[ph:ph + rb]
            acc = part if acc is None else acc + part
        cc_s[r0:r0 + rb, :] = acc + b_dw_ref[...]

    def mlstm_pair(ci, hp):
        rows = slice(ci * CHUNK, (ci + 1) * CHUNK)
        heads = (2 * hp, 2 * hp + 1)
        cols2 = slice(heads[0] * DK, (heads[1] + 1) * DK)
        gates_c = g_s[rows, :]
        if hp == 0:
            bc = _cumsum_rows(jax.nn.log_sigmoid(gates_c))
            gc_s[0] = bc
            gc_s[1] = gates_c.T
            gc_s[2] = bc.T
        m_row = m_s[...]
        q2 = q_s[rows, cols2]
        k2 = k_s[rows, cols2]
        v2 = v_s[rows, cols2]
        zero16 = jnp.zeros((CHUNK, DK), bf16)

        def blockdiag(a0, a1):
            return jnp.concatenate([jnp.concatenate([a0, zero16], axis=1),
                                    jnp.concatenate([zero16, a1], axis=1)], axis=0)

        s2 = _dot_nt(q2, blockdiag(k2[:, :DK], k2[:, DK:]))
        per = []
        for i, h in enumerate(heads):
            m_prev = m_row[:, F_LANE + h:F_LANE + h + 1]
            icol, bcol, m_t, w_intra, w_inter = _chunk_weights(
                gates_c, gc_s[0], gc_s[1], gc_s[2], h, causal, m_prev)
            a = w_intra * s2[:, i * DK:(i + 1) * DK]
            per.append((h, m_prev, icol, bcol, m_t, w_inter, a))
        a2 = jnp.concatenate([per[0][6], per[1][6]], axis=1).astype(bf16)
        v16 = v2.astype(bf16)
        num2 = _bdot(a2, blockdiag(v16[:, :DK], v16[:, DK:]))
        ct0, ct1 = ct_s[heads[0]], ct_s[heads[1]]
        qc2 = _bdot(q2, blockdiag(ct0.astype(bf16), ct1.astype(bf16)))
        m_new_row = m_row
        for i, (h, m_prev, icol, bcol, m_t, w_inter, a) in enumerate(per):
            hc = slice(i * DK, (i + 1) * DK)
            n_row = n_s[h:h + 1, :]
            den = (jnp.sum(a, axis=-1, keepdims=True)
                   + w_inter * jnp.sum(q2[:, hc].astype(f32) * n_row, axis=-1, keepdims=True))
            num = num2[:, hc] + w_inter * qc2[:, hc]
            hh_s[rows, h * DK:(h + 1) * DK] = num / jnp.maximum(jnp.abs(den), jnp.exp(-m_t))
            m_new = m_t[CHUNK - 1:CHUNK]
            b_last = bcol[CHUNK - 1:CHUNK]
            gcol = jnp.exp(((b_last - bcol) + icol) - m_new)
            decay = jnp.exp(b_last + m_prev - m_new)
            ct = ct0 if i == 0 else ct1
            ct_s[h] = decay * ct + _dot_tn(k2[:, hc], (gcol * v2[:, hc]).astype(bf16))
            n_s[h:h + 1, :] = decay * n_row + jnp.sum(gcol * k2[:, hc].astype(f32), axis=0,
                                                      keepdims=True)
            m_new_row = jnp.where(lane_row == F_LANE + h, m_new, m_new_row)
        m_s[...] = m_new_row

    n_piece = (tm // CHUNK) * N_HEADS
    d_ff = w_ff1_ref.shape[1]
    fstep, rb = d_ff // FFN_PIECES, tm // n_piece
    for j in range(n_piece):
        if j % (n_piece // FFN_PIECES) == 0:
            jf = j // (n_piece // FFN_PIECES)
            ffn_piece(jf * fstep, (jf + 1) * fstep)
        conv_block(j * rb, rb)
        if j % 2 == 0:
            mlstm_pair(j // N_HEADS, (j % N_HEADS) // 2)

    y_ref[...] = _ple_final(acc_s[...], p_ref[...], g_ple_ref, w_gate_ref, w_proj_ref,
                            g_final_ref)
    hm = jnp.concatenate(
        [_head_out(hh_s[:, h * DK:(h + 1) * DK], o_s[:, h * DK:(h + 1) * DK],
                   g_head_ref[:, h * DK:(h + 1) * DK]) for h in range(N_HEADS)], axis=1)
    hm_s[...] = hm.astype(bf16)
    cact_s[...] = _conv_post(cc_s[...], g_cn_ref, b_cn_ref).astype(bf16)
    ext_s[0:HIST_ROWS, :] = ext_s[tm:tm + HIST_ROWS, :]

    @pl.when(jnp.logical_and(t == nt - 1, s < n_tiles))
    def _():
        for h in range(N_HEADS):
            c_out_ref[h] = ct_s[h].T
        n_out_ref[...] = n_s[0:N_HEADS, :]
        m_out_ref[...] = jnp.broadcast_to(m_s[...], m_out_ref.shape)
        conv_out_ref[...] = ext_s[off:HIST_ROWS, :]


def _const_spec(shape):
    return pl.BlockSpec(shape, lambda *_: (0,) * len(shape), pipeline_mode=pl.Buffered(1))


def _weight_specs(ws):
    return [_const_spec(w.shape) for w in ws]


def _prompt_call(x, p, ws):
    B, T, D = x.shape
    tm = PROMPT_TILE_ROWS
    nt = T // tm
    n_tiles = B * nt
    d_conv = ws[4].shape[1]
    front = lambda s: (lax.rem(s, n_tiles) // nt, lax.rem(lax.rem(s, n_tiles), nt), 0)
    back = lambda s: (jnp.maximum(s - 1, 0) // nt, lax.rem(jnp.maximum(s - 1, 0), nt), 0)
    state = lambda s: (jnp.minimum(s, n_tiles - 1) // nt, 0, 0)
    sd = jax.ShapeDtypeStruct
    rows_buf = lambda w, dt: pltpu.VMEM((tm, w), dt)
    return pl.pallas_call(
        functools.partial(_prompt_kernel, nt),
        grid=(n_tiles + 1,),
        in_specs=[pl.BlockSpec((None, tm, D), front), pl.BlockSpec((None, tm, D), back),
                  pl.BlockSpec((None, tm, p.shape[2]), back)] + _weight_specs(ws),
        out_specs=(
            pl.BlockSpec((None, tm, D), back),
            pl.BlockSpec((None, N_HEADS, DK, DK), lambda s: state(s) + (0,)),
            pl.BlockSpec((None, N_HEADS, DK), state),
            pl.BlockSpec((None, SUBLANES, GATE_LANES), state),
            pl.BlockSpec((None, CONV_BUF, d_conv), state),
        ),
        out_shape=(
            sd((B, T, D), f32),
            sd((B, N_HEADS, DK, DK), f32),
            sd((B, N_HEADS, DK), f32),
            sd((B, SUBLANES, GATE_LANES), f32),
            sd((B, CONV_BUF, d_conv), f32),
        ),
        scratch_shapes=[
            pltpu.VMEM((N_HEADS, DK, DK), f32),
            pltpu.VMEM((SUBLANES, DK), f32),
            pltpu.VMEM((1, GATE_LANES), f32),
            pltpu.VMEM((HIST_ROWS + tm, d_conv), f32),
            rows_buf(D_MLSTM, bf16), rows_buf(d_conv, bf16),
            rows_buf(D, f32), rows_buf(D, bf16),
            rows_buf(D_MLSTM, bf16), rows_buf(D_MLSTM, bf16),
            rows_buf(D_MLSTM, f32), rows_buf(D_MLSTM, f32),
            rows_buf(GATE_LANES, f32),
            pltpu.VMEM((3, CHUNK, GATE_LANES), f32),
            rows_buf(D_MLSTM, f32), rows_buf(d_conv, f32),
        ],
        compiler_params=pltpu.CompilerParams(
            dimension_semantics=("arbitrary",), vmem_limit_bytes=VMEM_LIMIT_BYTES),
        name="prompt_layer",
    )(x, x, p, *ws)


def _sample_in_kernel(x_ref, g_mix_ref, w_in_ref, b_in_ref, qkvo_ref, gates_ref, u_ref):
    nb, seq_len, d_model = x_ref.shape
    x = x_ref[...].reshape(nb * seq_len, d_model)
    q, k, v, o, gates, u = _in_proj(x, g_mix_ref, w_in_ref, b_in_ref)
    qkvo_ref[...] = jnp.concatenate([q, k, v, o], axis=1)
    gates_ref[...] = gates
    u_ref[...] = u


def _sample_in_call(x, g_mix, w_in, b_in):
    bs, seq_len, D = x.shape
    R = bs * seq_len
    tm = TILE_ROWS
    d_conv = (w_in.shape[1] - 4 * D_MLSTM - GATE_LANES) // 2
    sd = jax.ShapeDtypeStruct
    return pl.pallas_call(
        _sample_in_kernel,
        grid=(R // tm,),
        in_specs=[pl.BlockSpec((tm // seq_len, seq_len, D), lambda i: (i, 0, 0))]
        + _weight_specs((g_mix, w_in, b_in)),
        out_specs=(pl.BlockSpec((tm, 4 * D_MLSTM), lambda i: (i, 0)),
                   pl.BlockSpec((tm, GATE_LANES), lambda i: (i, 0)),
                   pl.BlockSpec((tm, d_conv), lambda i: (i, 0))),
        out_shape=(sd((R, 4 * D_MLSTM), f32), sd((R, GATE_LANES), f32), sd((R, d_conv), f32)),
        compiler_params=pltpu.CompilerParams(
            dimension_semantics=("arbitrary",), vmem_limit_bytes=VMEM_LIMIT_BYTES),
        name="sample_in_proj",
    )(x, g_mix, w_in, b_in)


def _sample_rec_kernel(seq_len, qkvo_ref, gates_ref, u_ref, mrep_ref, c_ref, n_ref, hist_ref,
                       g_head_ref, w_dw_ref, b_dw_ref, g_cn_ref, b_cn_ref,
                       hm_ref, cact_ref, c_out_ref, n_out_ref, mt_ref, hist_out_ref,
                       numi_s, dec_s, gk_s):
    rows = qkvo_ref.shape[0]
    nb = rows // seq_len
    shift = seq_len.bit_length() - 1
    d = D_MLSTM

    rid = lax.broadcasted_iota(jnp.int32, (rows, rows), 0)
    cid = lax.broadcasted_iota(jnp.int32, (rows, rows), 1)
    same = (rid >> shift) == (cid >> shift)
    mask = jnp.logical_and(same, cid <= rid)
    lcum16 = jnp.where(mask, 1.0, 0.0).astype(bf16)
    bsum16 = jnp.where(same, 1.0, 0.0).astype(bf16)
    plast16 = jnp.where(cid == ((rid >> shift) << shift) + (seq_len - 1), 1.0, 0.0).astype(bf16)
    lane_g = lax.broadcasted_iota(jnp.int32, (rows, GATE_LANES), 1)

    gates = gates_ref[...]
    lf, bc, gates_t, bc_t = _gate_cumsum(gates, lcum16)
    blast = _exact_left(bsum16, lf)
    mrep = mrep_ref[...]

    q = qkvo_ref[:, 0:d]
    k = qkvo_ref[:, d:2 * d]
    v = qkvo_ref[:, 2 * d:3 * d]
    o = qkvo_ref[:, 3 * d:4 * d]
    q16 = q.astype(bf16)
    k16 = k.astype(bf16)

    per_head = []
    for h in range(N_HEADS):
        cols = slice(h * DK, (h + 1) * DK)
        m_prev = mrep[:, F_LANE + h:F_LANE + h + 1]
        icol, bcol, m_t, w_intra, w_inter = _chunk_weights(gates, bc, gates_t, bc_t, h, mask, m_prev)
        vaug = _v_aug(v[:, cols])
        nd = _intra(q16[:, cols], k16[:, cols], vaug.astype(bf16), w_intra)
        per_head.append((icol, bcol, m_t, w_inter, nd))
    mt_all = _lane_pick([ph[2] for ph in per_head], lane_g)
    mt_ref[...] = mt_all
    mnew = _exact_left(plast16, mt_all)
    dec_s[...] = jnp.exp(blast + mrep - mnew)
    gvt16 = []
    for h in range(N_HEADS):
        cols = slice(h * DK, (h + 1) * DK)
        icol, bcol, _, _, _ = per_head[h]
        lanes = slice(F_LANE + h, F_LANE + h + 1)
        gcol = jnp.exp(((blast[:, lanes] - bcol) + icol) - mnew[:, lanes])
        gvt16.append((gcol * v[:, cols]).T.astype(bf16))
        gk_s[:, cols] = gcol * k[:, cols]

    rsel = lax.broadcasted_iota(jnp.int32, (rows, nb), 0)
    bsel = lax.broadcasted_iota(jnp.int32, (rows, nb), 1)
    expand16 = jnp.where((rsel >> shift) == bsel, 1.0, 0.0).astype(bf16)
    nrep = _exact_left(expand16, n_ref[...])

    grp = lax.broadcasted_iota(jnp.int32, (rows, DK), 0) >> shift
    numi_s[...] = jnp.zeros_like(numi_s)

    sub = SUBLANES
    tile_shift = (sub // seq_len).bit_length() - 1
    in_tile = lax.broadcasted_iota(jnp.int32, (sub, d), 0) >> shift

    def per_seq(b, carry):
        mb = grp == b
        r0 = pl.multiple_of((b >> tile_shift) * sub, sub)
        mine = in_tile == (b & (sub // seq_len - 1))
        q8 = qkvo_ref[pl.ds(r0, sub), 0:d]
        dec_b = dec_s[pl.ds(b * seq_len, 1), :]
        n_b = n_ref[pl.ds(b, 1), :]
        gk8 = jnp.sum(jnp.where(mine, gk_s[pl.ds(r0, sub), :], 0.0), axis=0, keepdims=True)
        n_new = []
        num_i = []
        for h in range(N_HEADS):
            cols = slice(h * DK, (h + 1) * DK)
            cb = c_ref[b, h]
            num_i.append(_dot_nt(q8[:, cols].astype(bf16), cb.astype(bf16)))
            kb = jnp.where(mb, k[:, cols], 0.0).astype(bf16)
            dec = dec_b[:, F_LANE + h:F_LANE + h + 1]
            c_out_ref[b, h] = dec * cb + _bdot(gvt16[h], kb)
            n_new.append(dec * n_b[:, cols] + gk8[:, cols])
        numi_s[pl.ds(r0, sub), :] = jnp.where(mine, jnp.concatenate(num_i, axis=1),
                                              numi_s[pl.ds(r0, sub), :])
        n_out_ref[pl.ds(b, 1), :] = jnp.concatenate(n_new, axis=1)
        return carry

    lax.fori_loop(0, nb, per_seq, 0, unroll=SEQ_UNROLL)

    hm = []
    for h in range(N_HEADS):
        cols = slice(h * DK, (h + 1) * DK)
        _, _, m_t, w_inter, nd = per_head[h]
        den_i = jnp.sum(q[:, cols] * nrep[:, cols], axis=-1, keepdims=True)
        hh = _finish(nd, w_inter, numi_s[:, cols], den_i, m_t)
        hm.append(_head_out(hh, o[:, cols], g_head_ref[:, cols]))
    hm_ref[...] = jnp.concatenate(hm, axis=1)

    rt = lax.broadcasted_iota(jnp.int32, (rows, rows), 0)
    rbm = lax.broadcasted_iota(jnp.int32, (rows, rows), 1)
    nb_shift = nb.bit_length() - 1
    to_tm16 = jnp.where(rbm == ((rt & (nb - 1)) << shift) + (rt >> nb_shift), 1.0, 0.0).astype(bf16)
    to_bm16 = jnp.where(rt == ((rbm & (nb - 1)) << shift) + (rbm >> nb_shift), 1.0, 0.0).astype(bf16)
    u_tm = _exact_left(to_tm16, u_ref[...])
    u_steps = [u_tm[s * nb:(s + 1) * nb] for s in range(seq_len)]
    c_steps = []
    for t in range(seq_len):
        acc = None
        for j in range(t, CONV_BUF):
            term = w_dw_ref[j - t:j - t + 1, :] * hist_ref[j]
            acc = term if acc is None else acc + term
        for s in range(t + 1):
            acc = acc + w_dw_ref[CONV_BUF + s - t:CONV_BUF + s - t + 1, :] * u_steps[s]
        c_steps.append(acc + b_dw_ref[...])
    cact_tm = _conv_post(jnp.concatenate(c_steps, axis=0), g_cn_ref, b_cn_ref)
    cact_ref[...] = _bdot(to_bm16, cact_tm.astype(bf16))
    hist_out_ref[0:CONV_BUF - seq_len] = hist_ref[seq_len:CONV_BUF]
    for s in range(seq_len):
        hist_out_ref[CONV_BUF - seq_len + s] = u_steps[s]


def _sample_rec_call(seq_len, qkvo, gates, u, mrep, c_state, n_state, hist,
                     g_head, w_dw, b_dw, g_cn, b_cn):
    R = qkvo.shape[0]
    nb_total = c_state.shape[0]
    gb = SAMPLE_GROUP
    rows = gb * seq_len
    d_conv = u.shape[1]
    sd = jax.ShapeDtypeStruct
    rspec = lambda w: pl.BlockSpec((rows, w), lambda i: (i, 0))
    return pl.pallas_call(
        functools.partial(_sample_rec_kernel, seq_len),
        grid=(nb_total // gb,),
        in_specs=[rspec(qkvo.shape[1]), rspec(GATE_LANES), rspec(d_conv), rspec(GATE_LANES),
                  pl.BlockSpec((gb, N_HEADS, DK, DK), lambda i: (i, 0, 0, 0)),
                  pl.BlockSpec((gb, D_MLSTM), lambda i: (i, 0)),
                  pl.BlockSpec((CONV_BUF, gb, d_conv), lambda i: (0, i, 0))]
        + _weight_specs((g_head, w_dw, b_dw, g_cn, b_cn)),
        out_specs=(rspec(D_MLSTM), rspec(d_conv),
                   pl.BlockSpec((gb, N_HEADS, DK, DK), lambda i: (i, 0, 0, 0)),
                   pl.BlockSpec((gb, D_MLSTM), lambda i: (i, 0)),
                   rspec(GATE_LANES),
                   pl.BlockSpec((CONV_BUF, gb, d_conv), lambda i: (0, i, 0))),
        out_shape=(sd((R, D_MLSTM), f32), sd((R, d_conv), f32),
                   sd(c_state.shape, f32), sd(n_state.shape, f32),
                   sd((R, GATE_LANES), f32), sd(hist.shape, f32)),
        scratch_shapes=[pltpu.VMEM((rows, D_MLSTM), f32), pltpu.VMEM((rows, GATE_LANES), f32),
                        pltpu.VMEM((rows, D_MLSTM), f32)],
        compiler_params=pltpu.CompilerParams(
            dimension_semantics=("arbitrary",), vmem_limit_bytes=VMEM_LIMIT_BYTES),
        name="sample_recurrent",
    )(qkvo, gates, u, mrep, c_state, n_state, hist, g_head, w_dw, b_dw, g_cn, b_cn)


def _sample_tail_kernel(x_ref, hm_ref, cact_ref, p_ref, w_out_ref, g_ffn_ref, w_ff1_ref, w_ff2_ref,
                        g_ple_ref, w_gate_ref, w_proj_ref, g_final_ref, y_ref, acc_s, xn_s):
    j = pl.program_id(0)
    nb, seq_len, d_model = x_ref.shape
    rows = nb * seq_len

    @pl.when(j == 0)
    def _():
        x1 = _out_proj(x_ref[...].reshape(rows, d_model), hm_ref[...], cact_ref[...], w_out_ref)
        acc_s[...] = x1
        xn_s[...] = _rms(x1, g_ffn_ref[...]).astype(bf16)

    step = w_ff1_ref.shape[1]
    acc_s[...] += _ffn_part(xn_s[...], w_ff1_ref, w_ff2_ref, 0, step)

    @pl.when(j == pl.num_programs(0) - 1)
    def _():
        y = _ple_final(acc_s[...], p_ref[...].reshape(rows, p_ref.shape[2]), g_ple_ref, w_gate_ref,
                       w_proj_ref, g_final_ref)
        y_ref[...] = y.reshape(nb, seq_len, d_model)


def _sample_tail_call(x, hm, cact, p, ws):
    bs, seq_len, D = x.shape
    R = bs * seq_len
    w_out, g_ffn, w_ff1, w_ff2, g_ple, w_gate, w_proj, g_fin = ws
    d_ff = w_ff1.shape[1]
    step = d_ff // FFN_PIECES
    return pl.pallas_call(
        _sample_tail_kernel,
        grid=(FFN_PIECES,),
        in_specs=[_const_spec(x.shape), _const_spec(hm.shape), _const_spec(cact.shape),
                  _const_spec(p.shape), _const_spec(w_out.shape), _const_spec(g_ffn.shape),
                  pl.BlockSpec((w_ff1.shape[0], step), lambda j: (0, j)),
                  pl.BlockSpec((step, w_ff2.shape[1]), lambda j: (j, 0)),
                  _const_spec(g_ple.shape), _const_spec(w_gate.shape), _const_spec(w_proj.shape),
                  _const_spec(g_fin.shape)],
        out_specs=_const_spec(x.shape),
        out_shape=jax.ShapeDtypeStruct(x.shape, f32),
        scratch_shapes=[pltpu.VMEM((R, D), f32), pltpu.VMEM((R, D), bf16)],
        compiler_params=pltpu.CompilerParams(
            dimension_semantics=("arbitrary",), vmem_limit_bytes=VMEM_LIMIT_BYTES),
        name="sample_tail",
    )(x, hm, cact, p, *ws)


def _w_in_layout_kernel(d4, n_gate, wt_ref, o_ref):
    blk = TILE_ROWS
    rest = wt_ref.shape[0] - d4 - n_gate
    for c in range(0, d4, blk):
        o_ref[:, c:c + blk] = wt_ref[c:c + blk, :].T.astype(bf16)
    lane = lax.broadcasted_iota(jnp.int32, (wt_ref.shape[1], GATE_LANES), 1)
    gates = wt_ref[d4:d4 + GATE_LANES, :].T
    o_ref[:, d4:d4 + GATE_LANES] = jnp.where(lane < n_gate, gates, 0.0).astype(bf16)
    for c in range(0, rest, blk):
        o_ref[:, d4 + GATE_LANES + c:d4 + GATE_LANES + c + blk] = (
            wt_ref[d4 + n_gate + c:d4 + n_gate + c + blk, :].T.astype(bf16))


def _w_in_layout_call(wt, d4, n_gate):
    width, rows = wt.shape
    out_w = width - n_gate + GATE_LANES
    return pl.pallas_call(
        functools.partial(_w_in_layout_kernel, d4, n_gate),
        grid=(1,),
        in_specs=[_const_spec(wt.shape)],
        out_specs=_const_spec((rows, out_w)),
        out_shape=jax.ShapeDtypeStruct((rows, out_w), bf16),
        compiler_params=pltpu.CompilerParams(
            dimension_semantics=("arbitrary",), vmem_limit_bytes=VMEM_LIMIT_BYTES),
        name="w_in_layout",
    )(wt)


def _layer_weights(i, g_mix, w_in, b_in, g_head, w_dw, b_dw, g_cn, b_cn, w_out, g_ffn, w_ff1,
                   w_ff2, g_ple, w_ple_gate, w_ple_proj):
    row = lambda a: a[i].reshape(1, -1).astype(f32)
    d4 = 4 * D_MLSTM
    n_gate = 2 * N_HEADS
    pad = GATE_LANES - n_gate
    w_in_p = _w_in_layout_call(w_in[i].T, d4, n_gate)
    b = b_in[i]
    b_in_p = jnp.concatenate(
        [b[:d4], jnp.pad(b[d4:d4 + n_gate], (0, pad)), b[d4 + n_gate:]]).reshape(1, -1).astype(f32)
    w_dw_p = jnp.pad(w_dw[i].astype(f32), ((0, HIST_ROWS - CONV_WIDTH), (0, 0)))
    return dict(
        g_mix=row(g_mix), w_in=w_in_p, b_in=b_in_p, g_head=row(g_head), w_dw=w_dw_p,
        b_dw=row(b_dw), g_cn=row(g_cn), b_cn=row(b_cn), w_out=w_out[i].astype(bf16),
        g_ffn=row(g_ffn), w_ff1=w_ff1[i].astype(bf16), w_ff2=w_ff2[i].astype(bf16),
        g_ple=row(g_ple), w_gate=w_ple_gate[i].astype(bf16), w_proj=w_ple_proj[i].astype(bf16))


def kernel(x_prompt, x_sample, state_mlstm_C, state_mlstm_n, state_mlstm_m, cache_conv, p_prompt,
           p_sample, g_mix, w_in, b_in, g_head, w_dw, b_dw, g_cn, b_cn, w_out, g_ffn, w_ff1, w_ff2,
           g_ple, w_ple_gate, w_ple_proj, g_final):
    depth = w_in.shape[0]
    assert depth == 1, "the final norm is fused into the layer kernels"
    bs, seq_len, d_model = x_sample.shape
    assert seq_len & (seq_len - 1) == 0 and seq_len <= SUBLANES
    assert SAMPLE_GROUP & (SAMPLE_GROUP - 1) == 0 and SAMPLE_GROUP * seq_len == CHUNK
    g_fin = g_final.reshape(1, -1).astype(f32)

    i = 0
    lw = _layer_weights(i, g_mix, w_in, b_in, g_head, w_dw, b_dw, g_cn, b_cn, w_out, g_ffn,
                        w_ff1, w_ff2, g_ple, w_ple_gate, w_ple_proj)
    tail_ws = (lw["w_out"], lw["g_ffn"], lw["w_ff1"], lw["w_ff2"], lw["g_ple"], lw["w_gate"],
               lw["w_proj"], g_fin)

    prompt_ws = (lw["g_mix"], lw["w_in"], lw["b_in"], lw["g_head"], lw["w_dw"], lw["b_dw"],
                 lw["g_cn"], lw["b_cn"]) + tail_ws
    y_p, c_p, n_p, m_p, conv_p = _prompt_call(x_prompt, p_prompt[i], prompt_ws)
    m_p = m_p[:, 0, F_LANE:F_LANE + N_HEADS]

    qkvo, gates, u = _sample_in_call(x_sample, lw["g_mix"], lw["w_in"], lw["b_in"])
    m0 = jnp.pad(state_mlstm_m[i].astype(f32), ((0, 0), (F_LANE, GATE_LANES - F_LANE - N_HEADS)))
    mrep = jnp.repeat(m0, seq_len, axis=0)
    hm, cact, c_s, n_s, mt, conv_s = _sample_rec_call(
        seq_len, qkvo, gates, u, mrep, state_mlstm_C[i], state_mlstm_n[i].reshape(bs, -1),
        cache_conv[i].transpose(1, 0, 2), lw["g_head"], lw["w_dw"], lw["b_dw"], lw["g_cn"],
        lw["b_cn"])
    conv_s = conv_s.transpose(1, 0, 2)
    y_s = _sample_tail_call(x_sample, hm, cact, p_sample[i], tail_ws)
    m_s = mt[seq_len - 1::seq_len, F_LANE:F_LANE + N_HEADS]
    n_s = n_s.reshape(bs, N_HEADS, DK)

    stack = lambda a: a[None]
    return (y_p, y_s, stack(c_p), stack(n_p), stack(m_p), stack(conv_p),
            stack(c_s), stack(n_s), stack(m_s), stack(conv_s))
```

```python
import functools

import jax
import jax.numpy as jnp
from jax import lax
from jax.experimental import pallas as pl
from jax.experimental.pallas import tpu as pltpu

f32 = jnp.float32
bf16 = jnp.bfloat16

N_HEADS = 4
DK = 128
D_MLSTM = N_HEADS * DK
CONV_WIDTH = 31
CONV_BUF = CONV_WIDTH - 1
EPS = 1e-6
SUBLANES = 8
GATE_LANES = 128
F_LANE = N_HEADS
CHUNK = 128
TILE_ROWS = 256
PROMPT_TILE_ROWS = 512
HIST_ROWS = 32
SAMPLE_GROUP = 32
FFN_PIECES = 2
SAMPLE_FFN_STEPS = 4
SEQ_UNROLL = 8
VMEM_LIMIT_BYTES = 60 * 1024 * 1024


def _bdot(a, b):
    return jnp.dot(a, b, preferred_element_type=f32)


def _dot_nt(a, b):
    return lax.dot_general(a, b, (((1,), (1,)), ((), ())), preferred_element_type=f32)


def _dot_tn(a, b):
    return lax.dot_general(a, b, (((0,), (0,)), ((), ())), preferred_element_type=f32)


def _rms(x, g):
    y = x * lax.rsqrt(jnp.mean(x * x, axis=-1, keepdims=True) + EPS)
    return y * g


def _ln(x):
    mu = jnp.mean(x, axis=-1, keepdims=True)
    xc = x - mu
    return xc * lax.rsqrt(jnp.mean(xc * xc, axis=-1, keepdims=True) + EPS)


def _exact_left(sel16, x):
    hi = x.astype(bf16)
    r = x - hi.astype(f32)
    mid = r.astype(bf16)
    lo = (r - mid.astype(f32)).astype(bf16)
    return _bdot(sel16, hi) + _bdot(sel16, mid) + _bdot(sel16, lo)


def _lane_pick(rows, lane_ids):
    out = jnp.zeros((rows[0].shape[0], GATE_LANES), f32)
    for h, r in enumerate(rows):
        out = jnp.where(lane_ids == F_LANE + h, r, out)
    return out


def _in_proj(x, g_mix_ref, w_in_ref, b_in_ref):
    xn = _rms(x, g_mix_ref[...]).astype(bf16)

    def proj(a, b):
        return _bdot(xn, w_in_ref[:, a:b]) + b_in_ref[:, a:b]

    d = D_MLSTM
    q = proj(0, d)
    k = proj(d, 2 * d) * (DK ** -0.5)
    v = proj(2 * d, 3 * d)
    o = proj(3 * d, 4 * d)
    gates = proj(4 * d, 4 * d + GATE_LANES)
    c0 = 4 * d + GATE_LANES
    ga = proj(c0, c0 + d)
    gg = proj(c0 + d, c0 + 2 * d)
    u = ga * jax.nn.sigmoid(gg)
    return q, k, v, o, gates, u


def _gate_cumsum(gates, lcum16):
    lf = jax.nn.log_sigmoid(gates)
    bc = _exact_left(lcum16, lf)
    return lf, bc, gates.T, bc.T


def _chunk_weights(gates, bc, gates_t, bc_t, h, mask, m_prev):
    icol = gates[:, h:h + 1]
    bcol = bc[:, F_LANE + h:F_LANE + h + 1]
    irow = gates_t[h:h + 1, :]
    brow = bc_t[F_LANE + h:F_LANE + h + 1, :]
    logw = jnp.where(mask, (bcol - brow) + irow, -jnp.inf)
    m_intra = jnp.max(logw, axis=-1, keepdims=True)
    log_inter = bcol + m_prev
    m_t = jnp.maximum(log_inter, m_intra)
    w_intra = jnp.exp(logw - m_t)
    w_inter = jnp.exp(log_inter - m_t)
    return icol, bcol, m_t, w_intra, w_inter


def _v_aug(v):
    lane = lax.broadcasted_iota(jnp.int32, v.shape, 1)
    return jnp.concatenate([v, jnp.where(lane == 0, 1.0, 0.0)], axis=1)


def _intra(q16, k16, vaug16, w_intra):
    s = _dot_nt(q16, k16)
    a = (w_intra * s).astype(bf16)
    return _bdot(a, vaug16)


def _finish(nd, w_inter, num_inter, den_inter, m_t):
    num = nd[:, :DK] + w_inter * num_inter
    den = nd[:, DK:DK + 1] + w_inter * den_inter
    return num / jnp.maximum(jnp.abs(den), jnp.exp(-m_t))


def _head_out(h, o, g_head):
    return (_ln(h) * g_head) * jax.nn.sigmoid(o)


def _conv_post(c, g_cn_ref, b_cn_ref):
    c = _ln(c) * g_cn_ref[...] + b_cn_ref[...]
    return c * jax.nn.sigmoid(c)


def _out_proj(x, hm, c, w_out_ref):
    d = D_MLSTM
    return x + (_bdot(hm.astype(bf16), w_out_ref[0:d, :]) + _bdot(c.astype(bf16), w_out_ref[d:, :]))


def _ffn_part(xn16, w_ff1_ref, w_ff2_ref, j0, j1):
    f = jnp.maximum(_bdot(xn16, w_ff1_ref[:, j0:j1]), 0.0)
    return _bdot((f * f).astype(bf16), w_ff2_ref[j0:j1, :])


def _ple_final(x, p, g_ple_ref, w_gate_ref, w_proj_ref, g_final_ref):
    gate = jax.nn.sigmoid(_bdot(_rms(x, g_ple_ref[...]).astype(bf16), w_gate_ref[...]))
    x = x + gate * _bdot(p.astype(bf16), w_proj_ref[...])
    return _rms(x, g_final_ref[...])


def _cumsum_rows(x):
    sub = SUBLANES
    pos = lax.broadcasted_iota(jnp.int32, x.shape, 0) & (sub - 1)
    sh = 1
    while sh < sub:
        x = x + jnp.where(pos >= sh, pltpu.roll(x, sh, axis=0), 0.0)
        sh *= 2
    tiles = []
    carry = None
    for r0 in range(0, x.shape[0], sub):
        tile = x[r0:r0 + sub]
        if carry is not None:
            tile = tile + carry
        carry = tile[sub - 1:sub]
        tiles.append(tile)
    return jnp.concatenate(tiles, axis=0)


def _prompt_kernel(nt, xf_ref, xb_ref, p_ref, g_mix_ref, w_in_ref, b_in_ref, g_head_ref, w_dw_ref,
                   b_dw_ref, g_cn_ref, b_cn_ref, w_out_ref, g_ffn_ref, w_ff1_ref, w_ff2_ref,
                   g_ple_ref, w_gate_ref, w_proj_ref, g_final_ref,
                   y_ref, c_out_ref, n_out_ref, m_out_ref, conv_out_ref,
                   ct_s, n_s, m_s, ext_s, hm_s, cact_s, acc_s, xn_s, q_s, k_s, v_s, o_s, g_s, gc_s,
                   hh_s, cc_s):
    s = pl.program_id(0)
    n_tiles = pl.num_programs(0) - 1
    t = lax.rem(lax.rem(s, n_tiles), nt)
    tm = xf_ref.shape[0]
    off = HIST_ROWS - CONV_BUF

    @pl.when(s == 0)
    def _():
        hm_s[...] = jnp.zeros_like(hm_s)
        cact_s[...] = jnp.zeros_like(cact_s)

    @pl.when(t == 0)
    def _():
        ct_s[...] = jnp.zeros_like(ct_s)
        n_s[...] = jnp.zeros_like(n_s)
        m_s[...] = jnp.zeros_like(m_s)
        ext_s[0:HIST_ROWS, :] = jnp.zeros((HIST_ROWS, ext_s.shape[1]), f32)

    x1 = _out_proj(xb_ref[...], hm_s[...], cact_s[...], w_out_ref)
    acc_s[...] = x1
    xn_s[...] = _rms(x1, g_ffn_ref[...]).astype(bf16)
    q, k, v, o, gates, u = _in_proj(xf_ref[...], g_mix_ref, w_in_ref, b_in_ref)
    q_s[...] = q.astype(bf16)
    k_s[...] = k.astype(bf16)
    v_s[...] = v
    o_s[...] = o
    g_s[...] = gates
    ext_s[HIST_ROWS:HIST_ROWS + tm, :] = u

    rid = lax.broadcasted_iota(jnp.int32, (CHUNK, CHUNK), 0)
    cid = lax.broadcasted_iota(jnp.int32, (CHUNK, CHUNK), 1)
    causal = cid <= rid
    lane_row = lax.broadcasted_iota(jnp.int32, (1, GATE_LANES), 1)

    def ffn_piece(j0, j1):
        acc_s[...] += _ffn_part(xn_s[...], w_ff1_ref, w_ff2_ref, j0, j1)

    def conv_block(r0, rb):
        sub = SUBLANES
        acc = None
        for ph in range(sub):
            n_rows = rb if ph == 0 else rb + sub
            grp = None
            for j in range(ph, off + CONV_WIDTH, sub):
                if j < off:
                    continue
                term = w_dw_ref[j - off:j - off + 1, :] * ext_s[r0 + j - ph:r0 + j - ph + n_rows, :]
                grp = term if grp is None else grp + term
            part = grp if ph == 0 else grp[ph:ph + rb]
            acc = part if acc is None else acc + part
        cc_s[r0:r0 + rb, :] = acc + b_dw_ref[...]

    def mlstm_pair(ci, hp):
        rows = slice(ci * CHUNK, (ci + 1) * CHUNK)
        heads = (2 * hp, 2 * hp + 1)
        cols2 = slice(heads[0] * DK, (heads[1] + 1) * DK)
        gates_c = g_s[rows, :]
        if hp == 0:
            bc = _cumsum_rows(jax.nn.log_sigmoid(gates_c))
            gc_s[0] = bc
            gc_s[1] = gates_c.T
            gc_s[2] = bc.T
        m_row = m_s[...]
        q2 = q_s[rows, cols2]
        k2 = k_s[rows, cols2]
        v2 = v_s[rows, cols2]
        zero16 = jnp.zeros((CHUNK, DK), bf16)

        def blockdiag(a0, a1):
            return jnp.concatenate([jnp.concatenate([a0, zero16], axis=1),
                                    jnp.concatenate([zero16, a1], axis=1)], axis=0)

        s2 = _dot_nt(q2, blockdiag(k2[:, :DK], k2[:, DK:]))
        per = []
        for i, h in enumerate(heads):
            m_prev = m_row[:, F_LANE + h:F_LANE + h + 1]
            icol, bcol, m_t, w_intra, w_inter = _chunk_weights(
                gates_c, gc_s[0], gc_s[1], gc_s[2], h, causal, m_prev)
            a = w_intra * s2[:, i * DK:(i + 1) * DK]
            per.append((h, m_prev, icol, bcol, m_t, w_inter, a))
        a2 = jnp.concatenate([per[0][6], per[1][6]], axis=1).astype(bf16)
        v16 = v2.astype(bf16)
        num2 = _bdot(a2, blockdiag(v16[:, :DK], v16[:, DK:]))
        ct0, ct1 = ct_s[heads[0]], ct_s[heads[1]]
        qc2 = _bdot(q2, blockdiag(ct0.astype(bf16), ct1.astype(bf16)))
        m_new_row = m_row
        for i, (h, m_prev, icol, bcol, m_t, w_inter, a) in enumerate(per):
            hc = slice(i * DK, (i + 1) * DK)
            n_row = n_s[h:h + 1, :]
            den = (jnp.sum(a, axis=-1, keepdims=True)
                   + w_inter * jnp.sum(q2[:, hc].astype(f32) * n_row, axis=-1, keepdims=True))
            num = num2[:, hc] + w_inter * qc2[:, hc]
            hh_s[rows, h * DK:(h + 1) * DK] = num / jnp.maximum(jnp.abs(den), jnp.exp(-m_t))
            m_new = m_t[CHUNK - 1:CHUNK]
            b_last = bcol[CHUNK - 1:CHUNK]
            gcol = jnp.exp(((b_last - bcol) + icol) - m_new)
            decay = jnp.exp(b_last + m_prev - m_new)
            ct = ct0 if i == 0 else ct1
            ct_s[h] = decay * ct + _dot_tn(k2[:, hc], (gcol * v2[:, hc]).astype(bf16))
            n_s[h:h + 1, :] = decay * n_row + jnp.sum(gcol * k2[:, hc].astype(f32), axis=0,
                                                      keepdims=True)
            m_new_row = jnp.where(lane_row == F_LANE + h, m_new, m_new_row)
        m_s[...] = m_new_row

    n_piece = (tm // CHUNK) * N_HEADS
    d_ff = w_ff1_ref.shape[1]
    fstep, rb = d_ff // FFN_PIECES, tm // n_piece
    for j in range(n_piece):
        if j % (n_piece // FFN_PIECES) == 0:
            jf = j // (n_piece // FFN_PIECES)
            ffn_piece(jf * fstep, (jf + 1) * fstep)
        conv_block(j * rb, rb)
        if j % 2 == 0:
            mlstm_pair(j // N_HEADS, (j % N_HEADS) // 2)

    y_ref[...] = _ple_final(acc_s[...], p_ref[...], g_ple_ref, w_gate_ref, w_proj_ref,
                            g_final_ref)
    hm = jnp.concatenate(
        [_head_out(hh_s[:, h * DK:(h + 1) * DK], o_s[:, h * DK:(h + 1) * DK],
                   g_head_ref[:, h * DK:(h + 1) * DK]) for h in range(N_HEADS)], axis=1)
    hm_s[...] = hm.astype(bf16)
    cact_s[...] = _conv_post(cc_s[...], g_cn_ref, b_cn_ref).astype(bf16)
    ext_s[0:HIST_ROWS, :] = ext_s[tm:tm + HIST_ROWS, :]

    @pl.when(jnp.logical_and(t == nt - 1, s < n_tiles))
    def _():
        for h in range(N_HEADS):
            c_out_ref[h] = ct_s[h].T
        n_out_ref[...] = n_s[0:N_HEADS, :]
        m_out_ref[...] = jnp.broadcast_to(m_s[...], m_out_ref.shape)
        conv_out_ref[...] = ext_s[off:HIST_ROWS, :]


def _const_spec(shape):
    return pl.BlockSpec(shape, lambda *_: (0,) * len(shape), pipeline_mode=pl.Buffered(1))


def _weight_specs(ws):
    return [_const_spec(w.shape) for w in ws]


def _prompt_call(x, p, ws):
    B, T, D = x.shape
    tm = PROMPT_TILE_ROWS
    nt = T // tm
    n_tiles = B * nt
    d_conv = ws[4].shape[1]
    front = lambda s: (lax.rem(s, n_tiles) // nt, lax.rem(lax.rem(s, n_tiles), nt), 0)
    back = lambda s: (jnp.maximum(s - 1, 0) // nt, lax.rem(jnp.maximum(s - 1, 0), nt), 0)
    state = lambda s: (jnp.minimum(s, n_tiles - 1) // nt, 0, 0)
    sd = jax.ShapeDtypeStruct
    rows_buf = lambda w, dt: pltpu.VMEM((tm, w), dt)
    return pl.pallas_call(
        functools.partial(_prompt_kernel, nt),
        grid=(n_tiles + 1,),
        in_specs=[pl.BlockSpec((None, tm, D), front), pl.BlockSpec((None, tm, D), back),
                  pl.BlockSpec((None, tm, p.shape[2]), back)] + _weight_specs(ws),
        out_specs=(
            pl.BlockSpec((None, tm, D), back),
            pl.BlockSpec((None, N_HEADS, DK, DK), lambda s: state(s) + (0,)),
            pl.BlockSpec((None, N_HEADS, DK), state),
            pl.BlockSpec((None, SUBLANES, GATE_LANES), state),
            pl.BlockSpec((None, CONV_BUF, d_conv), state),
        ),
        out_shape=(
            sd((B, T, D), f32),
            sd((B, N_HEADS, DK, DK), f32),
            sd((B, N_HEADS, DK), f32),
            sd((B, SUBLANES, GATE_LANES), f32),
            sd((B, CONV_BUF, d_conv), f32),
        ),
        scratch_shapes=[
            pltpu.VMEM((N_HEADS, DK, DK), f32),
            pltpu.VMEM((SUBLANES, DK), f32),
            pltpu.VMEM((1, GATE_LANES), f32),
            pltpu.VMEM((HIST_ROWS + tm, d_conv), f32),
            rows_buf(D_MLSTM, bf16), rows_buf(d_conv, bf16),
            rows_buf(D, f32), rows_buf(D, bf16),
            rows_buf(D_MLSTM, bf16), rows_buf(D_MLSTM, bf16),
            rows_buf(D_MLSTM, f32), rows_buf(D_MLSTM, f32),
            rows_buf(GATE_LANES, f32),
            pltpu.VMEM((3, CHUNK, GATE_LANES), f32),
            rows_buf(D_MLSTM, f32), rows_buf(d_conv, f32),
        ],
        compiler_params=pltpu.CompilerParams(
            dimension_semantics=("arbitrary",), vmem_limit_bytes=VMEM_LIMIT_BYTES),
        name="prompt_layer",
    )(x, x, p, *ws)


def _sample_in_kernel(x_ref, g_mix_ref, w_in_ref, b_in_ref, qkvo_ref, gates_ref, u_ref):
    nb, seq_len, d_model = x_ref.shape
    x = x_ref[...].reshape(nb * seq_len, d_model)
    q, k, v, o, gates, u = _in_proj(x, g_mix_ref, w_in_ref, b_in_ref)
    qkvo_ref[...] = jnp.concatenate([q, k, v, o], axis=1)
    gates_ref[...] = gates
    u_ref[...] = u


def _sample_in_call(x, g_mix, w_in, b_in):
    bs, seq_len, D = x.shape
    R = bs * seq_len
    tm = TILE_ROWS
    d_conv = (w_in.shape[1] - 4 * D_MLSTM - GATE_LANES) // 2
    sd = jax.ShapeDtypeStruct
    return pl.pallas_call(
        _sample_in_kernel,
        grid=(R // tm,),
        in_specs=[pl.BlockSpec((tm // seq_len, seq_len, D), lambda i: (i, 0, 0))]
        + _weight_specs((g_mix, w_in, b_in)),
        out_specs=(pl.BlockSpec((tm, 4 * D_MLSTM), lambda i: (i, 0)),
                   pl.BlockSpec((tm, GATE_LANES), lambda i: (i, 0)),
                   pl.BlockSpec((tm, d_conv), lambda i: (i, 0))),
        out_shape=(sd((R, 4 * D_MLSTM), f32), sd((R, GATE_LANES), f32), sd((R, d_conv), f32)),
        compiler_params=pltpu.CompilerParams(
            dimension_semantics=("arbitrary",), vmem_limit_bytes=VMEM_LIMIT_BYTES),
        name="sample_in_proj",
    )(x, g_mix, w_in, b_in)


def _sample_rec_kernel(seq_len, qkvo_ref, gates_ref, u_ref, mrep_ref, c_ref, n_ref, hist_ref,
                       g_head_ref, w_dw_ref, b_dw_ref, g_cn_ref, b_cn_ref,
                       hm_ref, cact_ref, c_out_ref, n_out_ref, mt_ref, hist_out_ref,
                       numi_s, dec_s, gk_s):
    rows = qkvo_ref.shape[0]
    nb = rows // seq_len
    shift = seq_len.bit_length() - 1
    d = D_MLSTM

    rid = lax.broadcasted_iota(jnp.int32, (rows, rows), 0)
    cid = lax.broadcasted_iota(jnp.int32, (rows, rows), 1)
    same = (rid >> shift) == (cid >> shift)
    mask = jnp.logical_and(same, cid <= rid)
    lcum16 = jnp.where(mask, 1.0, 0.0).astype(bf16)
    bsum16 = jnp.where(same, 1.0, 0.0).astype(bf16)
    plast16 = jnp.where(cid == ((rid >> shift) << shift) + (seq_len - 1), 1.0, 0.0).astype(bf16)
    lane_g = lax.broadcasted_iota(jnp.int32, (rows, GATE_LANES), 1)

    gates = gates_ref[...]
    lf, bc, gates_t, bc_t = _gate_cumsum(gates, lcum16)
    blast = _exact_left(bsum16, lf)
    mrep = mrep_ref[...]

    q = qkvo_ref[:, 0:d]
    k = qkvo_ref[:, d:2 * d]
    v = qkvo_ref[:, 2 * d:3 * d]
    o = qkvo_ref[:, 3 * d:4 * d]
    q16 = q.astype(bf16)
    k16 = k.astype(bf16)

    per_head = []
    for h in range(N_HEADS):
        cols = slice(h * DK, (h + 1) * DK)
        m_prev = mrep[:, F_LANE + h:F_LANE + h + 1]
        icol, bcol, m_t, w_intra, w_inter = _chunk_weights(gates, bc, gates_t, bc_t, h, mask, m_prev)
        vaug = _v_aug(v[:, cols])
        nd = _intra(q16[:, cols], k16[:, cols], vaug.astype(bf16), w_intra)
        per_head.append((icol, bcol, m_t, w_inter, nd))
    mt_all = _lane_pick([ph[2] for ph in per_head], lane_g)
    mt_ref[...] = mt_all
    mnew = _exact_left(plast16, mt_all)
    dec_s[...] = jnp.exp(blast + mrep - mnew)
    gvt16 = []
    for h in range(N_HEADS):
        cols = slice(h * DK, (h + 1) * DK)
        icol, bcol, _, _, _ = per_head[h]
        lanes = slice(F_LANE + h, F_LANE + h + 1)
        gcol = jnp.exp(((blast[:, lanes] - bcol) + icol) - mnew[:, lanes])
        gvt16.append((gcol * v[:, cols]).T.astype(bf16))
        gk_s[:, cols] = gcol * k[:, cols]

    rsel = lax.broadcasted_iota(jnp.int32, (rows, nb), 0)
    bsel = lax.broadcasted_iota(jnp.int32, (rows, nb), 1)
    expand16 = jnp.where((rsel >> shift) == bsel, 1.0, 0.0).astype(bf16)
    nrep = _exact_left(expand16, n_ref[...])

    grp = lax.broadcasted_iota(jnp.int32, (rows, DK), 0) >> shift
    numi_s[...] = jnp.zeros_like(numi_s)

    sub = SUBLANES
    tile_shift = (sub // seq_len).bit_length() - 1
    in_tile = lax.broadcasted_iota(jnp.int32, (sub, d), 0) >> shift

    def per_seq(b, carry):
        mb = grp == b
        r0 = pl.multiple_of((b >> tile_shift) * sub, sub)
        mine = in_tile == (b & (sub // seq_len - 1))
        q8 = qkvo_ref[pl.ds(r0, sub), 0:d]
        dec_b = dec_s[pl.ds(b * seq_len, 1), :]
        n_b = n_ref[pl.ds(b, 1), :]
        gk8 = jnp.sum(jnp.where(mine, gk_s[pl.ds(r0, sub), :], 0.0), axis=0, keepdims=True)
        n_new = []
        num_i = []
        for h in range(N_HEADS):
            cols = slice(h * DK, (h + 1) * DK)
            cb = c_ref[b, h]
            num_i.append(_dot_nt(q8[:, cols].astype(bf16), cb.astype(bf16)))
            kb = jnp.where(mb, k[:, cols], 0.0).astype(bf16)
            dec = dec_b[:, F_LANE + h:F_LANE + h + 1]
            c_out_ref[b, h] = dec * cb + _bdot(gvt16[h], kb)
            n_new.append(dec * n_b[:, cols] + gk8[:, cols])
        numi_s[pl.ds(r0, sub), :] = jnp.where(mine, jnp.concatenate(num_i, axis=1),
                                              numi_s[pl.ds(r0, sub), :])
        n_out_ref[pl.ds(b, 1), :] = jnp.concatenate(n_new, axis=1)
        return carry

    lax.fori_loop(0, nb, per_seq, 0, unroll=SEQ_UNROLL)

    hm = []
    for h in range(N_HEADS):
        cols = slice(h * DK, (h + 1) * DK)
        _, _, m_t, w_inter, nd = per_head[h]
        den_i = jnp.sum(q[:, cols] * nrep[:, cols], axis=-1, keepdims=True)
        hh = _finish(nd, w_inter, numi_s[:, cols], den_i, m_t)
        hm.append(_head_out(hh, o[:, cols], g_head_ref[:, cols]))
    hm_ref[...] = jnp.concatenate(hm, axis=1)

    rt = lax.broadcasted_iota(jnp.int32, (rows, rows), 0)
    rbm = lax.broadcasted_iota(jnp.int32, (rows, rows), 1)
    nb_shift = nb.bit_length() - 1
    to_tm16 = jnp.where(rbm == ((rt & (nb - 1)) << shift) + (rt >> nb_shift), 1.0, 0.0).astype(bf16)
    to_bm16 = jnp.where(rt == ((rbm & (nb - 1)) << shift) + (rbm >> nb_shift), 1.0, 0.0).astype(bf16)
    u_tm = _exact_left(to_tm16, u_ref[...])
    u_steps = [u_tm[s * nb:(s + 1) * nb] for s in range(seq_len)]
    c_steps = []
    for t in range(seq_len):
        acc = None
        for j in range(t, CONV_BUF):
            term = w_dw_ref[j - t:j - t + 1, :] * hist_ref[j]
            acc = term if acc is None else acc + term
        for s in range(t + 1):
            acc = acc + w_dw_ref[CONV_BUF + s - t:CONV_BUF + s - t + 1, :] * u_steps[s]
        c_steps.append(acc + b_dw_ref[...])
    cact_tm = _conv_post(jnp.concatenate(c_steps, axis=0), g_cn_ref, b_cn_ref)
    cact_ref[...] = _bdot(to_bm16, cact_tm.astype(bf16))
    hist_out_ref[0:CONV_BUF - seq_len] = hist_ref[seq_len:CONV_BUF]
    for s in range(seq_len):
        hist_out_ref[CONV_BUF - seq_len + s] = u_steps[s]


def _sample_rec_call(seq_len, qkvo, gates, u, mrep, c_state, n_state, hist,
                     g_head, w_dw, b_dw, g_cn, b_cn):
    R = qkvo.shape[0]
    nb_total = c_state.shape[0]
    gb = SAMPLE_GROUP
    rows = gb * seq_len
    d_conv = u.shape[1]
    sd = jax.ShapeDtypeStruct
    rspec = lambda w: pl.BlockSpec((rows, w), lambda i: (i, 0))
    return pl.pallas_call(
        functools.partial(_sample_rec_kernel, seq_len),
        grid=(nb_total // gb,),
        in_specs=[rspec(qkvo.shape[1]), rspec(GATE_LANES), rspec(d_conv), rspec(GATE_LANES),
                  pl.BlockSpec((gb, N_HEADS, DK, DK), lambda i: (i, 0, 0, 0)),
                  pl.BlockSpec((gb, D_MLSTM), lambda i: (i, 0)),
                  pl.BlockSpec((CONV_BUF, gb, d_conv), lambda i: (0, i, 0))]
        + _weight_specs((g_head, w_dw, b_dw, g_cn, b_cn)),
        out_specs=(rspec(D_MLSTM), rspec(d_conv),
                   pl.BlockSpec((gb, N_HEADS, DK, DK), lambda i: (i, 0, 0, 0)),
                   pl.BlockSpec((gb, D_MLSTM), lambda i: (i, 0)),
                   rspec(GATE_LANES),
                   pl.BlockSpec((CONV_BUF, gb, d_conv), lambda i: (0, i, 0))),
        out_shape=(sd((R, D_MLSTM), f32), sd((R, d_conv), f32),
                   sd(c_state.shape, f32), sd(n_state.shape, f32),
                   sd((R, GATE_LANES), f32), sd(hist.shape, f32)),
        scratch_shapes=[pltpu.VMEM((rows, D_MLSTM), f32), pltpu.VMEM((rows, GATE_LANES), f32),
                        pltpu.VMEM((rows, D_MLSTM), f32)],
        compiler_params=pltpu.CompilerParams(
            dimension_semantics=("arbitrary",), vmem_limit_bytes=VMEM_LIMIT_BYTES),
        name="sample_recurrent",
    )(qkvo, gates, u, mrep, c_state, n_state, hist, g_head, w_dw, b_dw, g_cn, b_cn)


def _sample_tail_kernel(x_ref, hm_ref, cact_ref, p_ref, w_out_ref, g_ffn_ref, w_ff1_ref, w_ff2_ref,
                        g_ple_ref, w_gate_ref, w_proj_ref, g_final_ref, y_ref, acc_s, xn_s):
    j = pl.program_id(0)
    nb, seq_len, d_model = x_ref.shape
    rows = nb * seq_len

    @pl.when(j == 0)
    def _():
        x1 = _out_proj(x_ref[...].reshape(rows, d_model), hm_ref[...], cact_ref[...], w_out_ref)
        acc_s[...] = x1
        xn_s[...] = _rms(x1, g_ffn_ref[...]).astype(bf16)

    step = w_ff1_ref.shape[1]
    acc_s[...] += _ffn_part(xn_s[...], w_ff1_ref, w_ff2_ref, 0, step)

    @pl.when(j == pl.num_programs(0) - 1)
    def _():
        y = _ple_final(acc_s[...], p_ref[...].reshape(rows, p_ref.shape[2]), g_ple_ref, w_gate_ref,
                       w_proj_ref, g_final_ref)
        y_ref[...] = y.reshape(nb, seq_len, d_model)


def _sample_tail_call(x, hm, cact, p, ws):
    bs, seq_len, D = x.shape
    R = bs * seq_len
    w_out, g_ffn, w_ff1, w_ff2, g_ple, w_gate, w_proj, g_fin = ws
    d_ff = w_ff1.shape[1]
    step = d_ff // SAMPLE_FFN_STEPS
    return pl.pallas_call(
        _sample_tail_kernel,
        grid=(SAMPLE_FFN_STEPS,),
        in_specs=[_const_spec(x.shape), _const_spec(hm.shape), _const_spec(cact.shape),
                  _const_spec(p.shape), _const_spec(w_out.shape), _const_spec(g_ffn.shape),
                  pl.BlockSpec((w_ff1.shape[0], step), lambda j: (0, j)),
                  pl.BlockSpec((step, w_ff2.shape[1]), lambda j: (j, 0)),
                  _const_spec(g_ple.shape), _const_spec(w_gate.shape), _const_spec(w_proj.shape),
                  _const_spec(g_fin.shape)],
        out_specs=_const_spec(x.shape),
        out_shape=jax.ShapeDtypeStruct(x.shape, f32),
        scratch_shapes=[pltpu.VMEM((R, D), f32), pltpu.VMEM((R, D), bf16)],
        compiler_params=pltpu.CompilerParams(
            dimension_semantics=("arbitrary",), vmem_limit_bytes=VMEM_LIMIT_BYTES),
        name="sample_tail",
    )(x, hm, cact, p, *ws)


def _w_in_layout_kernel(d4, n_gate, wt_ref, o_ref):
    blk = TILE_ROWS
    rest = wt_ref.shape[0] - d4 - n_gate
    for c in range(0, d4, blk):
        o_ref[:, c:c + blk] = wt_ref[c:c + blk, :].T.astype(bf16)
    lane = lax.broadcasted_iota(jnp.int32, (wt_ref.shape[1], GATE_LANES), 1)
    gates = wt_ref[d4:d4 + GATE_LANES, :].T
    o_ref[:, d4:d4 + GATE_LANES] = jnp.where(lane < n_gate, gates, 0.0).astype(bf16)
    for c in range(0, rest, blk):
        o_ref[:, d4 + GATE_LANES + c:d4 + GATE_LANES + c + blk] = (
            wt_ref[d4 + n_gate + c:d4 + n_gate + c + blk, :].T.astype(bf16))


def _w_in_layout_call(wt, d4, n_gate):
    width, rows = wt.shape
    out_w = width - n_gate + GATE_LANES
    return pl.pallas_call(
        functools.partial(_w_in_layout_kernel, d4, n_gate),
        grid=(1,),
        in_specs=[_const_spec(wt.shape)],
        out_specs=_const_spec((rows, out_w)),
        out_shape=jax.ShapeDtypeStruct((rows, out_w), bf16),
        compiler_params=pltpu.CompilerParams(
            dimension_semantics=("arbitrary",), vmem_limit_bytes=VMEM_LIMIT_BYTES),
        name="w_in_layout",
    )(wt)


def _layer_weights(i, g_mix, w_in, b_in, g_head, w_dw, b_dw, g_cn, b_cn, w_out, g_ffn, w_ff1,
                   w_ff2, g_ple, w_ple_gate, w_ple_proj):
    row = lambda a: a[i].reshape(1, -1).astype(f32)
    d4 = 4 * D_MLSTM
    n_gate = 2 * N_HEADS
    pad = GATE_LANES - n_gate
    w_in_p = _w_in_layout_call(w_in[i].T, d4, n_gate)
    b = b_in[i]
    b_in_p = jnp.concatenate(
        [b[:d4], jnp.pad(b[d4:d4 + n_gate], (0, pad)), b[d4 + n_gate:]]).reshape(1, -1).astype(f32)
    w_dw_p = jnp.pad(w_dw[i].astype(f32), ((0, HIST_ROWS - CONV_WIDTH), (0, 0)))
    return dict(
        g_mix=row(g_mix), w_in=w_in_p, b_in=b_in_p, g_head=row(g_head), w_dw=w_dw_p,
        b_dw=row(b_dw), g_cn=row(g_cn), b_cn=row(b_cn), w_out=w_out[i].astype(bf16),
        g_ffn=row(g_ffn), w_ff1=w_ff1[i].astype(bf16), w_ff2=w_ff2[i].astype(bf16),
        g_ple=row(g_ple), w_gate=w_ple_gate[i].astype(bf16), w_proj=w_ple_proj[i].astype(bf16))


def kernel(x_prompt, x_sample, state_mlstm_C, state_mlstm_n, state_mlstm_m, cache_conv, p_prompt,
           p_sample, g_mix, w_in, b_in, g_head, w_dw, b_dw, g_cn, b_cn, w_out, g_ffn, w_ff1, w_ff2,
           g_ple, w_ple_gate, w_ple_proj, g_final):
    depth = w_in.shape[0]
    assert depth == 1, "the final norm is fused into the layer kernels"
    bs, seq_len, d_model = x_sample.shape
    assert seq_len & (seq_len - 1) == 0 and seq_len <= SUBLANES
    assert SAMPLE_GROUP & (SAMPLE_GROUP - 1) == 0 and SAMPLE_GROUP * seq_len == CHUNK
    g_fin = g_final.reshape(1, -1).astype(f32)

    i = 0
    lw = _layer_weights(i, g_mix, w_in, b_in, g_head, w_dw, b_dw, g_cn, b_cn, w_out, g_ffn,
                        w_ff1, w_ff2, g_ple, w_ple_gate, w_ple_proj)
    tail_ws = (lw["w_out"], lw["g_ffn"], lw["w_ff1"], lw["w_ff2"], lw["g_ple"], lw["w_gate"],
               lw["w_proj"], g_fin)

    prompt_ws = (lw["g_mix"], lw["w_in"], lw["b_in"], lw["g_head"], lw["w_dw"], lw["b_dw"],
                 lw["g_cn"], lw["b_cn"]) + tail_ws
    y_p, c_p, n_p, m_p, conv_p = _prompt_call(x_prompt, p_prompt[i], prompt_ws)
    m_p = m_p[:, 0, F_LANE:F_LANE + N_HEADS]

    qkvo, gates, u = _sample_in_call(x_sample, lw["g_mix"], lw["w_in"], lw["b_in"])
    m0 = jnp.pad(state_mlstm_m[i].astype(f32), ((0, 0), (F_LANE, GATE_LANES - F_LANE - N_HEADS)))
    mrep = jnp.repeat(m0, seq_len, axis=0)
    hm, cact, c_s, n_s, mt, conv_s = _sample_rec_call(
        seq_len, qkvo, gates, u, mrep, state_mlstm_C[i], state_mlstm_n[i].reshape(bs, -1),
        cache_conv[i].transpose(1, 0, 2), lw["g_head"], lw["w_dw"], lw["b_dw"], lw["g_cn"],
        lw["b_cn"])
    conv_s = conv_s.transpose(1, 0, 2)
    y_s = _sample_tail_call(x_sample, hm, cact, p_sample[i], tail_ws)
    m_s = mt[seq_len - 1::seq_len, F_LANE:F_LANE + N_HEADS]
    n_s = n_s.reshape(bs, N_HEADS, DK)

    stack = lambda a: a[None]
    return (y_p, y_s, stack(c_p), stack(n_p), stack(m_p), stack(conv_p),
            stack(c_s), stack(n_s), stack(m_s), stack(conv_s))
```

```python
import functools

import jax
import jax.numpy as jnp
from jax import lax
from jax.experimental import pallas as pl
from jax.experimental.pallas import tpu as pltpu

f32 = jnp.float32
bf16 = jnp.bfloat16

N_HEADS = 4
DK = 128
D_MLSTM = N_HEADS * DK
CONV_WIDTH = 31
CONV_BUF = CONV_WIDTH - 1
EPS = 1e-6
SUBLANES = 8
GATE_LANES = 128
F_LANE = N_HEADS
CHUNK = 128
TILE_ROWS = 256
PROMPT_TILE_ROWS = 512
HIST_ROWS = 32
SAMPLE_GROUP = 32
FFN_PIECES = 2
SAMPLE_FFN_STEPS = 4
SEQ_UNROLL = 8
VMEM_LIMIT_BYTES = 60 * 1024 * 1024


def _bdot(a, b):
    return jnp.dot(a, b, preferred_element_type=f32)


def _dot_nt(a, b):
    return lax.dot_general(a, b, (((1,), (1,)), ((), ())), preferred_element_type=f32)


def _dot_tn(a, b):
    return lax.dot_general(a, b, (((0,), (0,)), ((), ())), preferred_element_type=f32)


def _rms(x, g):
    y = x * lax.rsqrt(jnp.mean(x * x, axis=-1, keepdims=True) + EPS)
    return y * g


def _ln(x):
    mu = jnp.mean(x, axis=-1, keepdims=True)
    xc = x - mu
    return xc * lax.rsqrt(jnp.mean(xc * xc, axis=-1, keepdims=True) + EPS)


def _exact_left(sel16, x):
    hi = x.astype(bf16)
    r = x - hi.astype(f32)
    mid = r.astype(bf16)
    lo = (r - mid.astype(f32)).astype(bf16)
    return _bdot(sel16, hi) + _bdot(sel16, mid) + _bdot(sel16, lo)


def _lane_pick(rows, lane_ids):
    out = jnp.zeros((rows[0].shape[0], GATE_LANES), f32)
    for h, r in enumerate(rows):
        out = jnp.where(lane_ids == F_LANE + h, r, out)
    return out


def _in_proj(x, g_mix_ref, w_in_ref, b_in_ref):
    xn = _rms(x, g_mix_ref[...]).astype(bf16)

    def proj(a, b):
        return _bdot(xn, w_in_ref[:, a:b]) + b_in_ref[:, a:b]

    d = D_MLSTM
    qk = proj(0, 2 * d)
    vo = proj(2 * d, 4 * d)
    q, k = qk[:, :d], qk[:, d:] * (DK ** -0.5)
    v, o = vo[:, :d], vo[:, d:]
    gates = proj(4 * d, 4 * d + GATE_LANES)
    c0 = 4 * d + GATE_LANES
    glu = proj(c0, c0 + 2 * d)
    u = glu[:, :d] * jax.nn.sigmoid(glu[:, d:])
    return q, k, v, o, gates, u


def _gate_cumsum(gates, lcum16):
    lf = jax.nn.log_sigmoid(gates)
    bc = _exact_left(lcum16, lf)
    return lf, bc, gates.T, bc.T


def _chunk_weights(gates, bc, gates_t, bc_t, h, mask, m_prev):
    icol = gates[:, h:h + 1]
    bcol = bc[:, F_LANE + h:F_LANE + h + 1]
    irow = gates_t[h:h + 1, :]
    brow = bc_t[F_LANE + h:F_LANE + h + 1, :]
    logw = jnp.where(mask, (bcol - brow) + irow, -jnp.inf)
    m_intra = jnp.max(logw, axis=-1, keepdims=True)
    log_inter = bcol + m_prev
    m_t = jnp.maximum(log_inter, m_intra)
    w_intra = jnp.exp(logw - m_t)
    w_inter = jnp.exp(log_inter - m_t)
    return icol, bcol, m_t, w_intra, w_inter


def _v_aug(v):
    lane = lax.broadcasted_iota(jnp.int32, v.shape, 1)
    return jnp.concatenate([v, jnp.where(lane == 0, 1.0, 0.0)], axis=1)


def _intra(q16, k16, vaug16, w_intra):
    s = _dot_nt(q16, k16)
    a = (w_intra * s).astype(bf16)
    return _bdot(a, vaug16)


def _finish(nd, w_inter, num_inter, den_inter, m_t):
    num = nd[:, :DK] + w_inter * num_inter
    den = nd[:, DK:DK + 1] + w_inter * den_inter
    return num / jnp.maximum(jnp.abs(den), jnp.exp(-m_t))


def _head_out(h, o, g_head):
    return (_ln(h) * g_head) * jax.nn.sigmoid(o)


def _conv_post(c, g_cn_ref, b_cn_ref):
    c = _ln(c) * g_cn_ref[...] + b_cn_ref[...]
    return c * jax.nn.sigmoid(c)


def _out_proj(x, hm, c, w_out_ref):
    d = D_MLSTM
    return x + (_bdot(hm.astype(bf16), w_out_ref[0:d, :]) + _bdot(c.astype(bf16), w_out_ref[d:, :]))


def _ffn_part(xn16, w_ff1_ref, w_ff2_ref, j0, j1):
    f = jnp.maximum(_bdot(xn16, w_ff1_ref[:, j0:j1]), 0.0)
    return _bdot((f * f).astype(bf16), w_ff2_ref[j0:j1, :])


def _ple_final(x, p, g_ple_ref, w_gate_ref, w_proj_ref, g_final_ref):
    gate = jax.nn.sigmoid(_bdot(_rms(x, g_ple_ref[...]).astype(bf16), w_gate_ref[...]))
    x = x + gate * _bdot(p.astype(bf16), w_proj_ref[...])
    return _rms(x, g_final_ref[...])


def _cumsum_rows(x):
    sub = SUBLANES
    pos = lax.broadcasted_iota(jnp.int32, x.shape, 0) & (sub - 1)
    sh = 1
    while sh < sub:
        x = x + jnp.where(pos >= sh, pltpu.roll(x, sh, axis=0), 0.0)
        sh *= 2
    tiles = []
    carry = None
    for r0 in range(0, x.shape[0], sub):
        tile = x[r0:r0 + sub]
        if carry is not None:
            tile = tile + carry
        carry = tile[sub - 1:sub]
        tiles.append(tile)
    return jnp.concatenate(tiles, axis=0)


def _prompt_kernel(nt, xf_ref, xb_ref, p_ref, g_mix_ref, w_in_ref, b_in_ref, g_head_ref, w_dw_ref,
                   b_dw_ref, g_cn_ref, b_cn_ref, w_out_ref, g_ffn_ref, w_ff1_ref, w_ff2_ref,
                   g_ple_ref, w_gate_ref, w_proj_ref, g_final_ref,
                   y_ref, c_out_ref, n_out_ref, m_out_ref, conv_out_ref,
                   ct_s, n_s, m_s, ext_s, hm_s, cact_s, acc_s, xn_s, q_s, k_s, v_s, o_s, g_s, gc_s,
                   hh_s, cc_s):
    s = pl.program_id(0)
    n_tiles = pl.num_programs(0) - 1
    t = lax.rem(lax.rem(s, n_tiles), nt)
    tm = xf_ref.shape[0]
    off = HIST_ROWS - CONV_BUF

    @pl.when(s == 0)
    def _():
        hm_s[...] = jnp.zeros_like(hm_s)
        cact_s[...] = jnp.zeros_like(cact_s)

    @pl.when(t == 0)
    def _():
        ct_s[...] = jnp.zeros_like(ct_s)
        n_s[...] = jnp.zeros_like(n_s)
        m_s[...] = jnp.zeros_like(m_s)
        ext_s[0:HIST_ROWS, :] = jnp.zeros((HIST_ROWS, ext_s.shape[1]), f32)

    x1 = _out_proj(xb_ref[...], hm_s[...], cact_s[...], w_out_ref)
    acc_s[...] = x1
    xn_s[...] = _rms(x1, g_ffn_ref[...]).astype(bf16)
    q, k, v, o, gates, u = _in_proj(xf_ref[...], g_mix_ref, w_in_ref, b_in_ref)
    q_s[...] = q.astype(bf16)
    k_s[...] = k.astype(bf16)
    v_s[...] = v
    o_s[...] = o
    g_s[...] = gates
    ext_s[HIST_ROWS:HIST_ROWS + tm, :] = u

    rid = lax.broadcasted_iota(jnp.int32, (CHUNK, CHUNK), 0)
    cid = lax.broadcasted_iota(jnp.int32, (CHUNK, CHUNK), 1)
    causal = cid <= rid
    lane_row = lax.broadcasted_iota(jnp.int32, (1, GATE_LANES), 1)

    def ffn_piece(j0, j1):
        acc_s[...] += _ffn_part(xn_s[...], w_ff1_ref, w_ff2_ref, j0, j1)

    def conv_block(r0, rb):
        sub = SUBLANES
        acc = None
        for ph in range(sub):
            n_rows = rb if ph == 0 else rb + sub
            grp = None
            for j in range(ph, off + CONV_WIDTH, sub):
                if j < off:
                    continue
                term = w_dw_ref[j - off:j - off + 1, :] * ext_s[r0 + j - ph:r0 + j - ph + n_rows, :]
                grp = term if grp is None else grp + term
            part = grp if ph == 0 else grp[ph:ph + rb]
            acc = part if acc is None else acc + part
        cc_s[r0:r0 + rb, :] = acc + b_dw_ref[...]

    def mlstm_pair(ci, hp):
        rows = slice(ci * CHUNK, (ci + 1) * CHUNK)
        heads = (2 * hp, 2 * hp + 1)
        cols2 = slice(heads[0] * DK, (heads[1] + 1) * DK)
        gates_c = g_s[rows, :]
        if hp == 0:
            bc = _cumsum_rows(jax.nn.log_sigmoid(gates_c))
            gc_s[0] = bc
            gc_s[1] = gates_c.T
            gc_s[2] = bc.T
        m_row = m_s[...]
        q2 = q_s[rows, cols2]
        k2 = k_s[rows, cols2]
        v2 = v_s[rows, cols2]
        zero16 = jnp.zeros((CHUNK, DK), bf16)

        def blockdiag(a0, a1):
            return jnp.concatenate([jnp.concatenate([a0, zero16], axis=1),
                                    jnp.concatenate([zero16, a1], axis=1)], axis=0)

        s2 = _dot_nt(q2, blockdiag(k2[:, :DK], k2[:, DK:]))
        per = []
        for i, h in enumerate(heads):
            m_prev = m_row[:, F_LANE + h:F_LANE + h + 1]
            icol, bcol, m_t, w_intra, w_inter = _chunk_weights(
                gates_c, gc_s[0], gc_s[1], gc_s[2], h, causal, m_prev)
            a = w_intra * s2[:, i * DK:(i + 1) * DK]
            per.append((h, m_prev, icol, bcol, m_t, w_inter, a))
        a2 = jnp.concatenate([per[0][6], per[1][6]], axis=1).astype(bf16)
        v16 = v2.astype(bf16)
        num2 = _bdot(a2, blockdiag(v16[:, :DK], v16[:, DK:]))
        ct0, ct1 = ct_s[heads[0]], ct_s[heads[1]]
        qc2 = _bdot(q2, blockdiag(ct0.astype(bf16), ct1.astype(bf16)))
        m_new_row = m_row
        for i, (h, m_prev, icol, bcol, m_t, w_inter, a) in enumerate(per):
            hc = slice(i * DK, (i + 1) * DK)
            n_row = n_s[h:h + 1, :]
            den = (jnp.sum(a, axis=-1, keepdims=True)
                   + w_inter * jnp.sum(q2[:, hc].astype(f32) * n_row, axis=-1, keepdims=True))
            num = num2[:, hc] + w_inter * qc2[:, hc]
            hh_s[rows, h * DK:(h + 1) * DK] = num / jnp.maximum(jnp.abs(den), jnp.exp(-m_t))
            m_new = m_t[CHUNK - 1:CHUNK]
            b_last = bcol[CHUNK - 1:CHUNK]
            gcol = jnp.exp(((b_last - bcol) + icol) - m_new)
            decay = jnp.exp(b_last + m_prev - m_new)
            ct = ct0 if i == 0 else ct1
            ct_s[h] = decay * ct + _dot_tn(k2[:, hc], (gcol * v2[:, hc]).astype(bf16))
            n_s[h:h + 1, :] = decay * n_row + jnp.sum(gcol * k2[:, hc].astype(f32), axis=0,
                                                      keepdims=True)
            m_new_row = jnp.where(lane_row == F_LANE + h, m_new, m_new_row)
        m_s[...] = m_new_row

    n_piece = (tm // CHUNK) * N_HEADS
    d_ff = w_ff1_ref.shape[1]
    fstep, rb = d_ff // FFN_PIECES, tm // n_piece
    for j in range(n_piece):
        if j % (n_piece // FFN_PIECES) == 0:
            jf = j // (n_piece // FFN_PIECES)
            ffn_piece(jf * fstep, (jf + 1) * fstep)
        conv_block(j * rb, rb)
        if j % 2 == 0:
            mlstm_pair(j // N_HEADS, (j % N_HEADS) // 2)

    y_ref[...] = _ple_final(acc_s[...], p_ref[...], g_ple_ref, w_gate_ref, w_proj_ref,
                            g_final_ref)
    hm = jnp.concatenate(
        [_head_out(hh_s[:, h * DK:(h + 1) * DK], o_s[:, h * DK:(h + 1) * DK],
                   g_head_ref[:, h * DK:(h + 1) * DK]) for h in range(N_HEADS)], axis=1)
    hm_s[...] = hm.astype(bf16)
    cact_s[...] = _conv_post(cc_s[...], g_cn_ref, b_cn_ref).astype(bf16)
    ext_s[0:HIST_ROWS, :] = ext_s[tm:tm + HIST_ROWS, :]

    @pl.when(jnp.logical_and(t == nt - 1, s < n_tiles))
    def _():
        for h in range(N_HEADS):
            c_out_ref[h] = ct_s[h].T
        n_out_ref[...] = n_s[0:N_HEADS, :]
        m_out_ref[...] = jnp.broadcast_to(m_s[...], m_out_ref.shape)
        conv_out_ref[...] = ext_s[off:HIST_ROWS, :]


def _const_spec(shape):
    return pl.BlockSpec(shape, lambda *_: (0,) * len(shape), pipeline_mode=pl.Buffered(1))


def _weight_specs(ws):
    return [_const_spec(w.shape) for w in ws]


def _prompt_call(x, p, ws):
    B, T, D = x.shape
    tm = PROMPT_TILE_ROWS
    nt = T // tm
    n_tiles = B * nt
    d_conv = ws[4].shape[1]
    front = lambda s: (lax.rem(s, n_tiles) // nt, lax.rem(lax.rem(s, n_tiles), nt), 0)
    back = lambda s: (jnp.maximum(s - 1, 0) // nt, lax.rem(jnp.maximum(s - 1, 0), nt), 0)
    state = lambda s: (jnp.minimum(s, n_tiles - 1) // nt, 0, 0)
    sd = jax.ShapeDtypeStruct
    rows_buf = lambda w, dt: pltpu.VMEM((tm, w), dt)
    return pl.pallas_call(
        functools.partial(_prompt_kernel, nt),
        grid=(n_tiles + 1,),
        in_specs=[pl.BlockSpec((None, tm, D), front), pl.BlockSpec((None, tm, D), back),
                  pl.BlockSpec((None, tm, p.shape[2]), back)] + _weight_specs(ws),
        out_specs=(
            pl.BlockSpec((None, tm, D), back),
            pl.BlockSpec((None, N_HEADS, DK, DK), lambda s: state(s) + (0,)),
            pl.BlockSpec((None, N_HEADS, DK), state),
            pl.BlockSpec((None, SUBLANES, GATE_LANES), state),
            pl.BlockSpec((None, CONV_BUF, d_conv), state),
        ),
        out_shape=(
            sd((B, T, D), f32),
            sd((B, N_HEADS, DK, DK), f32),
            sd((B, N_HEADS, DK), f32),
            sd((B, SUBLANES, GATE_LANES), f32),
            sd((B, CONV_BUF, d_conv), f32),
        ),
        scratch_shapes=[
            pltpu.VMEM((N_HEADS, DK, DK), f32),
            pltpu.VMEM((SUBLANES, DK), f32),
            pltpu.VMEM((1, GATE_LANES), f32),
            pltpu.VMEM((HIST_ROWS + tm, d_conv), f32),
            rows_buf(D_MLSTM, bf16), rows_buf(d_conv, bf16),
            rows_buf(D, f32), rows_buf(D, bf16),
            rows_buf(D_MLSTM, bf16), rows_buf(D_MLSTM, bf16),
            rows_buf(D_MLSTM, f32), rows_buf(D_MLSTM, f32),
            rows_buf(GATE_LANES, f32),
            pltpu.VMEM((3, CHUNK, GATE_LANES), f32),
            rows_buf(D_MLSTM, f32), rows_buf(d_conv, f32),
        ],
        compiler_params=pltpu.CompilerParams(
            dimension_semantics=("arbitrary",), vmem_limit_bytes=VMEM_LIMIT_BYTES),
        name="prompt_layer",
    )(x, x, p, *ws)


def _sample_in_kernel(x_ref, g_mix_ref, w_in_ref, b_in_ref, qkvo_ref, gates_ref, u_ref):
    nb, seq_len, d_model = x_ref.shape
    x = x_ref[...].reshape(nb * seq_len, d_model)
    q, k, v, o, gates, u = _in_proj(x, g_mix_ref, w_in_ref, b_in_ref)
    qkvo_ref[...] = jnp.concatenate([q, k, v, o], axis=1)
    gates_ref[...] = gates
    u_ref[...] = u


def _sample_in_call(x, g_mix, w_in, b_in):
    bs, seq_len, D = x.shape
    R = bs * seq_len
    tm = TILE_ROWS
    d_conv = (w_in.shape[1] - 4 * D_MLSTM - GATE_LANES) // 2
    sd = jax.ShapeDtypeStruct
    return pl.pallas_call(
        _sample_in_kernel,
        grid=(R // tm,),
        in_specs=[pl.BlockSpec((tm // seq_len, seq_len, D), lambda i: (i, 0, 0))]
        + _weight_specs((g_mix, w_in, b_in)),
        out_specs=(pl.BlockSpec((tm, 4 * D_MLSTM), lambda i: (i, 0)),
                   pl.BlockSpec((tm, GATE_LANES), lambda i: (i, 0)),
                   pl.BlockSpec((tm, d_conv), lambda i: (i, 0))),
        out_shape=(sd((R, 4 * D_MLSTM), f32), sd((R, GATE_LANES), f32), sd((R, d_conv), f32)),
        compiler_params=pltpu.CompilerParams(
            dimension_semantics=("arbitrary",), vmem_limit_bytes=VMEM_LIMIT_BYTES),
        name="sample_in_proj",
    )(x, g_mix, w_in, b_in)


def _sample_rec_kernel(seq_len, qkvo_ref, gates_ref, u_ref, mrep_ref, c_ref, n_ref, hist_ref,
                       g_head_ref, w_dw_ref, b_dw_ref, g_cn_ref, b_cn_ref,
                       hm_ref, cact_ref, c_out_ref, n_out_ref, mt_ref, hist_out_ref,
                       numi_s, dec_s, gk_s):
    rows = qkvo_ref.shape[0]
    nb = rows // seq_len
    shift = seq_len.bit_length() - 1
    d = D_MLSTM

    rid = lax.broadcasted_iota(jnp.int32, (rows, rows), 0)
    cid = lax.broadcasted_iota(jnp.int32, (rows, rows), 1)
    same = (rid >> shift) == (cid >> shift)
    mask = jnp.logical_and(same, cid <= rid)
    lcum16 = jnp.where(mask, 1.0, 0.0).astype(bf16)
    bsum16 = jnp.where(same, 1.0, 0.0).astype(bf16)
    plast16 = jnp.where(cid == ((rid >> shift) << shift) + (seq_len - 1), 1.0, 0.0).astype(bf16)
    lane_g = lax.broadcasted_iota(jnp.int32, (rows, GATE_LANES), 1)

    gates = gates_ref[...]
    lf, bc, gates_t, bc_t = _gate_cumsum(gates, lcum16)
    blast = _exact_left(bsum16, lf)
    mrep = mrep_ref[...]

    q = qkvo_ref[:, 0:d]
    k = qkvo_ref[:, d:2 * d]
    v = qkvo_ref[:, 2 * d:3 * d]
    o = qkvo_ref[:, 3 * d:4 * d]
    q16 = q.astype(bf16)
    k16 = k.astype(bf16)

    per_head = []
    for h in range(N_HEADS):
        cols = slice(h * DK, (h + 1) * DK)
        m_prev = mrep[:, F_LANE + h:F_LANE + h + 1]
        icol, bcol, m_t, w_intra, w_inter = _chunk_weights(gates, bc, gates_t, bc_t, h, mask, m_prev)
        vaug = _v_aug(v[:, cols])
        nd = _intra(q16[:, cols], k16[:, cols], vaug.astype(bf16), w_intra)
        per_head.append((icol, bcol, m_t, w_inter, nd))
    mt_all = _lane_pick([ph[2] for ph in per_head], lane_g)
    mt_ref[...] = mt_all
    mnew = _exact_left(plast16, mt_all)
    dec_s[...] = jnp.exp(blast + mrep - mnew)
    gvt16 = []
    for h in range(N_HEADS):
        cols = slice(h * DK, (h + 1) * DK)
        icol, bcol, _, _, _ = per_head[h]
        lanes = slice(F_LANE + h, F_LANE + h + 1)
        gcol = jnp.exp(((blast[:, lanes] - bcol) + icol) - mnew[:, lanes])
        gvt16.append((gcol * v[:, cols]).T.astype(bf16))
        gk_s[:, cols] = gcol * k[:, cols]

    rsel = lax.broadcasted_iota(jnp.int32, (rows, nb), 0)
    bsel = lax.broadcasted_iota(jnp.int32, (rows, nb), 1)
    expand16 = jnp.where((rsel >> shift) == bsel, 1.0, 0.0).astype(bf16)
    nrep = _exact_left(expand16, n_ref[...])

    grp = lax.broadcasted_iota(jnp.int32, (rows, DK), 0) >> shift
    numi_s[...] = jnp.zeros_like(numi_s)

    sub = SUBLANES
    tile_shift = (sub // seq_len).bit_length() - 1
    in_tile = lax.broadcasted_iota(jnp.int32, (sub, d), 0) >> shift

    def per_seq(b, carry):
        mb = grp == b
        r0 = pl.multiple_of((b >> tile_shift) * sub, sub)
        mine = in_tile == (b & (sub // seq_len - 1))
        q8 = qkvo_ref[pl.ds(r0, sub), 0:d]
        dec_b = dec_s[pl.ds(b * seq_len, 1), :]
        n_b = n_ref[pl.ds(b, 1), :]
        gk8 = jnp.sum(jnp.where(mine, gk_s[pl.ds(r0, sub), :], 0.0), axis=0, keepdims=True)
        n_new = []
        num_i = []
        for h in range(N_HEADS):
            cols = slice(h * DK, (h + 1) * DK)
            cb = c_ref[b, h]
            num_i.append(_dot_nt(q8[:, cols].astype(bf16), cb.astype(bf16)))
            kb = jnp.where(mb, k[:, cols], 0.0).astype(bf16)
            dec = dec_b[:, F_LANE + h:F_LANE + h + 1]
            c_out_ref[b, h] = dec * cb + _bdot(gvt16[h], kb)
            n_new.append(dec * n_b[:, cols] + gk8[:, cols])
        numi_s[pl.ds(r0, sub), :] = jnp.where(mine, jnp.concatenate(num_i, axis=1),
                                              numi_s[pl.ds(r0, sub), :])
        n_out_ref[pl.ds(b, 1), :] = jnp.concatenate(n_new, axis=1)
        return carry

    lax.fori_loop(0, nb, per_seq, 0, unroll=SEQ_UNROLL)

    hm = []
    for h in range(N_HEADS):
        cols = slice(h * DK, (h + 1) * DK)
        _, _, m_t, w_inter, nd = per_head[h]
        den_i = jnp.sum(q[:, cols] * nrep[:, cols], axis=-1, keepdims=True)
        hh = _finish(nd, w_inter, numi_s[:, cols], den_i, m_t)
        hm.append(_head_out(hh, o[:, cols], g_head_ref[:, cols]))
    hm_ref[...] = jnp.concatenate(hm, axis=1)

    rt = lax.broadcasted_iota(jnp.int32, (rows, rows), 0)
    rbm = lax.broadcasted_iota(jnp.int32, (rows, rows), 1)
    nb_shift = nb.bit_length() - 1
    to_tm16 = jnp.where(rbm == ((rt & (nb - 1)) << shift) + (rt >> nb_shift), 1.0, 0.0).astype(bf16)
    to_bm16 = jnp.where(rt == ((rbm & (nb - 1)) << shift) + (rbm >> nb_shift), 1.0, 0.0).astype(bf16)
    u_tm = _exact_left(to_tm16, u_ref[...])
    u_steps = [u_tm[s * nb:(s + 1) * nb] for s in range(seq_len)]
    c_steps = []
    for t in range(seq_len):
        acc = None
        for j in range(t, CONV_BUF):
            term = w_dw_ref[j - t:j - t + 1, :] * hist_ref[j]
            acc = term if acc is None else acc + term
        for s in range(t + 1):
            acc = acc + w_dw_ref[CONV_BUF + s - t:CONV_BUF + s - t + 1, :] * u_steps[s]
        c_steps.append(acc + b_dw_ref[...])
    cact_tm = _conv_post(jnp.concatenate(c_steps, axis=0), g_cn_ref, b_cn_ref)
    cact_ref[...] = _bdot(to_bm16, cact_tm.astype(bf16))
    hist_out_ref[0:CONV_BUF - seq_len] = hist_ref[seq_len:CONV_BUF]
    for s in range(seq_len):
        hist_out_ref[CONV_BUF - seq_len + s] = u_steps[s]


def _sample_rec_call(seq_len, qkvo, gates, u, mrep, c_state, n_state, hist,
                     g_head, w_dw, b_dw, g_cn, b_cn):
    R = qkvo.shape[0]
    nb_total = c_state.shape[0]
    gb = SAMPLE_GROUP
    rows = gb * seq_len
    d_conv = u.shape[1]
    sd = jax.ShapeDtypeStruct
    rspec = lambda w: pl.BlockSpec((rows, w), lambda i: (i, 0))
    return pl.pallas_call(
        functools.partial(_sample_rec_kernel, seq_len),
        grid=(nb_total // gb,),
        in_specs=[rspec(qkvo.shape[1]), rspec(GATE_LANES), rspec(d_conv), rspec(GATE_LANES),
                  pl.BlockSpec((gb, N_HEADS, DK, DK), lambda i: (i, 0, 0, 0)),
                  pl.BlockSpec((gb, D_MLSTM), lambda i: (i, 0)),
                  pl.BlockSpec((CONV_BUF, gb, d_conv), lambda i: (0, i, 0))]
        + _weight_specs((g_head, w_dw, b_dw, g_cn, b_cn)),
        out_specs=(rspec(D_MLSTM), rspec(d_conv),
                   pl.BlockSpec((gb, N_HEADS, DK, DK), lambda i: (i, 0, 0, 0)),
                   pl.BlockSpec((gb, D_MLSTM), lambda i: (i, 0)),
                   rspec(GATE_LANES),
                   pl.BlockSpec((CONV_BUF, gb, d_conv), lambda i: (0, i, 0))),
        out_shape=(sd((R, D_MLSTM), f32), sd((R, d_conv), f32),
                   sd(c_state.shape, f32), sd(n_state.shape, f32),
                   sd((R, GATE_LANES), f32), sd(hist.shape, f32)),
        scratch_shapes=[pltpu.VMEM((rows, D_MLSTM), f32), pltpu.VMEM((rows, GATE_LANES), f32),
                        pltpu.VMEM((rows, D_MLSTM), f32)],
        compiler_params=pltpu.CompilerParams(
            dimension_semantics=("arbitrary",), vmem_limit_bytes=VMEM_LIMIT_BYTES),
        name="sample_recurrent",
    )(qkvo, gates, u, mrep, c_state, n_state, hist, g_head, w_dw, b_dw, g_cn, b_cn)


def _sample_tail_kernel(x_ref, hm_ref, cact_ref, p_ref, w_out_ref, g_ffn_ref, w_ff1_ref, w_ff2_ref,
                        g_ple_ref, w_gate_ref, w_proj_ref, g_final_ref, y_ref, acc_s, xn_s):
    j = pl.program_id(0)
    nb, seq_len, d_model = x_ref.shape
    rows = nb * seq_len

    @pl.when(j == 0)
    def _():
        x1 = _out_proj(x_ref[...].reshape(rows, d_model), hm_ref[...], cact_ref[...], w_out_ref)
        acc_s[...] = x1
        xn_s[...] = _rms(x1, g_ffn_ref[...]).astype(bf16)

    step = w_ff1_ref.shape[1]
    acc_s[...] += _ffn_part(xn_s[...], w_ff1_ref, w_ff2_ref, 0, step)

    @pl.when(j == pl.num_programs(0) - 1)
    def _():
        y = _ple_final(acc_s[...], p_ref[...].reshape(rows, p_ref.shape[2]), g_ple_ref, w_gate_ref,
                       w_proj_ref, g_final_ref)
        y_ref[...] = y.reshape(nb, seq_len, d_model)


def _sample_tail_call(x, hm, cact, p, ws):
    bs, seq_len, D = x.shape
    R = bs * seq_len
    w_out, g_ffn, w_ff1, w_ff2, g_ple, w_gate, w_proj, g_fin = ws
    d_ff = w_ff1.shape[1]
    step = d_ff // SAMPLE_FFN_STEPS
    return pl.pallas_call(
        _sample_tail_kernel,
        grid=(SAMPLE_FFN_STEPS,),
        in_specs=[_const_spec(x.shape), _const_spec(hm.shape), _const_spec(cact.shape),
                  _const_spec(p.shape), _const_spec(w_out.shape), _const_spec(g_ffn.shape),
                  pl.BlockSpec((w_ff1.shape[0], step), lambda j: (0, j)),
                  pl.BlockSpec((step, w_ff2.shape[1]), lambda j: (j, 0)),
                  _const_spec(g_ple.shape), _const_spec(w_gate.shape), _const_spec(w_proj.shape),
                  _const_spec(g_fin.shape)],
        out_specs=_const_spec(x.shape),
        out_shape=jax.ShapeDtypeStruct(x.shape, f32),
        scratch_shapes=[pltpu.VMEM((R, D), f32), pltpu.VMEM((R, D), bf16)],
        compiler_params=pltpu.CompilerParams(
            dimension_semantics=("arbitrary",), vmem_limit_bytes=VMEM_LIMIT_BYTES),
        name="sample_tail",
    )(x, hm, cact, p, *ws)


def _w_in_layout_kernel(d4, n_gate, wt_ref, o_ref):
    blk = TILE_ROWS
    rest = wt_ref.shape[0] - d4 - n_gate
    for c in range(0, d4, blk):
        o_ref[:, c:c + blk] = wt_ref[c:c + blk, :].T.astype(bf16)
    lane = lax.broadcasted_iota(jnp.int32, (wt_ref.shape[1], GATE_LANES), 1)
    gates = wt_ref[d4:d4 + GATE_LANES, :].T
    o_ref[:, d4:d4 + GATE_LANES] = jnp.where(lane < n_gate, gates, 0.0).astype(bf16)
    for c in range(0, rest, blk):
        o_ref[:, d4 + GATE_LANES + c:d4 + GATE_LANES + c + blk] = (
            wt_ref[d4 + n_gate + c:d4 + n_gate + c + blk, :].T.astype(bf16))


def _w_in_layout_call(wt, d4, n_gate):
    width, rows = wt.shape
    out_w = width - n_gate + GATE_LANES
    return pl.pallas_call(
        functools.partial(_w_in_layout_kernel, d4, n_gate),
        grid=(1,),
        in_specs=[_const_spec(wt.shape)],
        out_specs=_const_spec((rows, out_w)),
        out_shape=jax.ShapeDtypeStruct((rows, out_w), bf16),
        compiler_params=pltpu.CompilerParams(
            dimension_semantics=("arbitrary",), vmem_limit_bytes=VMEM_LIMIT_BYTES),
        name="w_in_layout",
    )(wt)


def _layer_weights(i, g_mix, w_in, b_in, g_head, w_dw, b_dw, g_cn, b_cn, w_out, g_ffn, w_ff1,
                   w_ff2, g_ple, w_ple_gate, w_ple_proj):
    row = lambda a: a[i].reshape(1, -1).astype(f32)
    d4 = 4 * D_MLSTM
    n_gate = 2 * N_HEADS
    pad = GATE_LANES - n_gate
    w_in_p = _w_in_layout_call(w_in[i].T, d4, n_gate)
    b = b_in[i]
    b_in_p = jnp.concatenate(
        [b[:d4], jnp.pad(b[d4:d4 + n_gate], (0, pad)), b[d4 + n_gate:]]).reshape(1, -1).astype(f32)
    w_dw_p = jnp.pad(w_dw[i].astype(f32), ((0, HIST_ROWS - CONV_WIDTH), (0, 0)))
    return dict(
        g_mix=row(g_mix), w_in=w_in_p, b_in=b_in_p, g_head=row(g_head), w_dw=w_dw_p,
        b_dw=row(b_dw), g_cn=row(g_cn), b_cn=row(b_cn), w_out=w_out[i].astype(bf16),
        g_ffn=row(g_ffn), w_ff1=w_ff1[i].astype(bf16), w_ff2=w_ff2[i].astype(bf16),
        g_ple=row(g_ple), w_gate=w_ple_gate[i].astype(bf16), w_proj=w_ple_proj[i].astype(bf16))


def kernel(x_prompt, x_sample, state_mlstm_C, state_mlstm_n, state_mlstm_m, cache_conv, p_prompt,
           p_sample, g_mix, w_in, b_in, g_head, w_dw, b_dw, g_cn, b_cn, w_out, g_ffn, w_ff1, w_ff2,
           g_ple, w_ple_gate, w_ple_proj, g_final):
    depth = w_in.shape[0]
    assert depth == 1, "the final norm is fused into the layer kernels"
    bs, seq_len, d_model = x_sample.shape
    assert seq_len & (seq_len - 1) == 0 and seq_len <= SUBLANES
    assert SAMPLE_GROUP & (SAMPLE_GROUP - 1) == 0 and SAMPLE_GROUP * seq_len == CHUNK
    g_fin = g_final.reshape(1, -1).astype(f32)

    i = 0
    lw = _layer_weights(i, g_mix, w_in, b_in, g_head, w_dw, b_dw, g_cn, b_cn, w_out, g_ffn,
                        w_ff1, w_ff2, g_ple, w_ple_gate, w_ple_proj)
    tail_ws = (lw["w_out"], lw["g_ffn"], lw["w_ff1"], lw["w_ff2"], lw["g_ple"], lw["w_gate"],
               lw["w_proj"], g_fin)

    prompt_ws = (lw["g_mix"], lw["w_in"], lw["b_in"], lw["g_head"], lw["w_dw"], lw["b_dw"],
                 lw["g_cn"], lw["b_cn"]) + tail_ws
    y_p, c_p, n_p, m_p, conv_p = _prompt_call(x_prompt, p_prompt[i], prompt_ws)
    m_p = m_p[:, 0, F_LANE:F_LANE + N_HEADS]

    qkvo, gates, u = _sample_in_call(x_sample, lw["g_mix"], lw["w_in"], lw["b_in"])
    m0 = jnp.pad(state_mlstm_m[i].astype(f32), ((0, 0), (F_LANE, GATE_LANES - F_LANE - N_HEADS)))
    mrep = jnp.repeat(m0, seq_len, axis=0)
    hm, cact, c_s, n_s, mt, conv_s = _sample_rec_call(
        seq_len, qkvo, gates, u, mrep, state_mlstm_C[i], state_mlstm_n[i].reshape(bs, -1),
        cache_conv[i].transpose(1, 0, 2), lw["g_head"], lw["w_dw"], lw["b_dw"], lw["g_cn"],
        lw["b_cn"])
    conv_s = conv_s.transpose(1, 0, 2)
    y_s = _sample_tail_call(x_sample, hm, cact, p_sample[i], tail_ws)
    m_s = mt[seq_len - 1::seq_len, F_LANE:F_LANE + N_HEADS]
    n_s = n_s.reshape(bs, N_HEADS, DK)

    stack = lambda a: a[None]
    return (y_p, y_s, stack(c_p), stack(n_p), stack(m_p), stack(conv_p),
            stack(c_s), stack(n_s), stack(m_s), stack(conv_s))
```
